```python
import jax, jax.numpy as jnp
from jax import lax
import numpy as np

D_MODEL = 1024
BATCH = 4
SEQ = 4096
DEPTH = 4
DEC_BATCH = 32
DEC_SEQ = 8
PAST_LEN = 8192
PAGE_SIZE = 128

N_MIXERS = 2
N_RWKV = (DEPTH + 1) // 2
N_DSA = DEPTH // 2
NORM_EPS = 1e-6
D_FF = 4 * D_MODEL
RW_HEAD = 64
RW_HEADS = D_MODEL // RW_HEAD
LORA_DECAY = 64
LORA_AAA = 64
LORA_MV = 32
LORA_GATE = 160
GN_EPS = 64e-5
ATT_HEADS = 16
ATT_HEAD_DIM = 64
KV_HEADS = 4
GROUPS = ATT_HEADS // KV_HEADS
IDX_HEADS = 8
IDX_DIM = 64
TOPK_MAX = 256
QBLK = 128
ROPE_THETA = 500000.0
ROT_FRAC = 4
Q_W = ATT_HEADS * ATT_HEAD_DIM
KV_W = KV_HEADS * ATT_HEAD_DIM
IQ_W = IDX_HEADS * IDX_DIM
P_IN = Q_W + 2 * KV_W + IQ_W + IDX_DIM + IDX_HEADS
SPLITS = [Q_W, Q_W + KV_W, Q_W + 2 * KV_W, Q_W + 2 * KV_W + IQ_W, Q_W + 2 * KV_W + IQ_W + IDX_DIM]
F32 = jnp.float32

kernel_name = 'rwkv7_dsa_hybrid_step'


def rms_norm(x, g):
    xf = x.astype(F32)
    y = xf * lax.rsqrt(jnp.mean(xf * xf, axis=-1, keepdims=True) + NORM_EPS)
    return (y * g.astype(F32)).astype(x.dtype)


def ada_mods(c, w, b):
    m = jax.nn.silu(c) @ w + b
    return jnp.split(m[:, None, :], 6, axis=-1)


def rope_partial(x, pos):
    d = x.shape[-1]
    rd = d // ROT_FRAC
    half = rd // 2
    inv = ROPE_THETA ** (-jnp.arange(half, dtype=F32) * 2.0 / rd)
    ang = pos.astype(F32)[:, None] * inv[None, :]
    cos = jnp.cos(ang)[None, :, None, :]
    sin = jnp.sin(ang)[None, :, None, :]
    xf = x.astype(F32)
    x1 = xf[..., :half]
    x2 = xf[..., half:rd]
    out = jnp.concatenate([x1 * cos - x2 * sin, x2 * cos + x1 * sin, xf[..., rd:]], axis=-1)
    return out.astype(x.dtype)


def gather_rows(rows, idx):
    return jax.vmap(lambda r, i: r[i])(rows, idx)


def wkv7_scan(S0, r, w, k, v, a, b):
    def step(S, inp):
        r_t, w_t, k_t, v_t, a_t, b_t = inp
        sa = jnp.einsum('bhvk,bhk->bhv', S, a_t)
        S = S * w_t[:, :, None, :] + sa[..., None] * b_t[:, :, None, :] + v_t[..., None] * k_t[:, :, None, :]
        y = jnp.einsum('bhvk,bhk->bhv', S, r_t)
        return S, y
    xs = tuple(jnp.moveaxis(t.astype(F32), 1, 0) for t in (r, w, k, v, a, b))
    S, ys = lax.scan(step, S0.astype(F32), xs)
    return S, jnp.moveaxis(ys, 0, 1)


def rwkv7_mix(h, shift_prev, S0, v_first, P, j):
    B, T, D = h.shape
    hd = lambda t: t.reshape(B, T, RW_HEADS, RW_HEAD)
    mix = P['rw_mix'][j]
    h_prev = jnp.concatenate([shift_prev[:, None, :].astype(h.dtype), h[:, :-1]], axis=1)
    dx = h_prev - h
    xr, xw, xk, xv, xa, xg = [h + dx * mix[n] for n in range(6)]
    w_rkv = P['rw_w_rkv'][j]
    r = xr @ w_rkv[0]
    k = xk @ w_rkv[1]
    v = xv @ w_rkv[2]
    w_pre = (P['rw_w0'][j] + jnp.tanh(xw @ P['rw_w1'][j]) @ P['rw_w2'][j]).astype(F32)
    decay = jnp.exp(-jnp.exp(-jax.nn.softplus(-w_pre) - 0.5))
    if j == 0:
        v_first = v
    else:
        v = v + (v_first - v) * jax.nn.sigmoid(P['rw_v0'][j - 1] + (xv @ P['rw_v1'][j - 1]) @ P['rw_v2'][j - 1])
    a = jax.nn.sigmoid(P['rw_a0'][j] + (xa @ P['rw_a1'][j]) @ P['rw_a2'][j])
    g = jax.nn.sigmoid(xg @ P['rw_g1'][j]) @ P['rw_g2'][j]
    kk = hd(k * P['rw_k_k'][j]).astype(F32)
    kk = kk * lax.rsqrt(jnp.maximum(jnp.sum(kk * kk, axis=-1, keepdims=True), 1e-24))
    k = k * (1 + (a - 1) * P['rw_k_a'][j])
    S, y = wkv7_scan(S0, hd(r), hd(decay), hd(k), hd(v), -kk, kk * hd(a).astype(F32))
    mu = jnp.mean(y, axis=-1, keepdims=True)
    var = jnp.mean(jnp.square(y - mu), axis=-1, keepdims=True)
    y = ((y - mu) * lax.rsqrt(var + GN_EPS)).reshape(B, T, D)
    y = y * P['rw_lnx_w'][j].astype(F32) + P['rw_lnx_b'][j].astype(F32)
    bonus = jnp.sum(hd(r * k).astype(F32) * P['rw_r_k'][j].astype(F32), axis=-1, keepdims=True) * hd(v).astype(F32)
    y = (y + bonus.reshape(B, T, D)).astype(h.dtype)
    out = (y * g) @ P['rw_w_o'][j]
    return out, v_first, S, h[:, -1]


def dsa_project(h, pos, P, j):
    B, T, _ = h.shape
    q, k, v, qi, ki, wi = jnp.split(h @ P['att_w_in'][j], SPLITS, axis=-1)
    q = rope_partial(rms_norm(q.reshape(B, T, ATT_HEADS, ATT_HEAD_DIM), P['att_q_norm'][j]), pos)
    k = rope_partial(rms_norm(k.reshape(B, T, KV_HEADS, ATT_HEAD_DIM), P['att_k_norm'][j]), pos)
    v = v.reshape(B, T, KV_HEADS, ATT_HEAD_DIM)
    qi = rope_partial(qi.reshape(B, T, IDX_HEADS, IDX_DIM), pos)
    ki = rope_partial(rms_norm(ki, P['idx_k_norm'][j])[:, :, None, :], pos)[:, :, 0, :]
    wi = wi * IDX_HEADS ** -0.5
    return q, k, v, qi, ki, wi


def select_keys(qi, wi, ki, pos_q, topk):
    L = ki.shape[1]
    logits = jnp.einsum('bthd,bsd->bths', qi.astype(F32), ki.astype(F32)) * IDX_DIM ** -0.5
    score = jnp.einsum('bths,bth->bts', jax.nn.relu(logits), wi.astype(F32))
    pos_k = jnp.arange(L)
    adm = pos_k[None, None, :] <= pos_q[None, :, None]
    _, idx = lax.top_k(jnp.where(adm, score, -jnp.inf), topk)
    valid = idx <= pos_q[None, :, None]
    return idx, valid


def sparse_attend(q, kg, vg, valid):
    B, T = q.shape[:2]
    qg = q.reshape(B, T, KV_HEADS, GROUPS, ATT_HEAD_DIM).astype(F32)
    s = jnp.einsum('bthgd,btshd->bthgs', qg, kg.astype(F32)) * ATT_HEAD_DIM ** -0.5
    s = jnp.where(valid[:, :, None, None, :], s, -jnp.inf)
    p = jax.nn.softmax(s, axis=-1)
    o = jnp.einsum('bthgs,btshd->bthgd', p, vg.astype(F32))
    return o.reshape(B, T, Q_W).astype(q.dtype)


def dsa_prompt(q, k, v, qi, ki, wi):
    B, S = q.shape[:2]
    topk = min(TOPK_MAX, S // 4)
    def block(i):
        start = i * QBLK
        sl = lambda t: lax.dynamic_slice_in_dim(t, start, QBLK, axis=1)
        pos_q = start + jnp.arange(QBLK)
        idx, valid = select_keys(sl(qi), sl(wi), ki, pos_q, topk)
        return sparse_attend(sl(q), gather_rows(k, idx), gather_rows(v, idx), valid)
    o = lax.map(block, jnp.arange(S // QBLK))
    return jnp.moveaxis(o, 0, 1).reshape(B, S, Q_W)


def dsa_sample(q, k, v, qi, ki, wi, pool_k, pool_v, pool_ki, page_table):
    B, T = q.shape[:2]
    n_pages = PAST_LEN // PAGE_SIZE
    L = PAST_LEN + T
    topk = min(TOPK_MAX, L // 4)
    ki_past = pool_ki[page_table].reshape(B, PAST_LEN, IDX_DIM)
    ki_all = jnp.concatenate([ki_past, ki.astype(ki_past.dtype)], axis=1)
    pos_q = PAST_LEN + jnp.arange(T)
    idx, valid = select_keys(qi, wi, ki_all, pos_q, topk)
    in_past = (idx < PAST_LEN)[..., None, None]
    page = jnp.minimum(idx // PAGE_SIZE, n_pages - 1)
    phys = gather_rows(page_table, page) * PAGE_SIZE + idx % PAGE_SIZE
    loc = jnp.clip(idx - PAST_LEN, 0, T - 1)
    k_flat = pool_k.reshape(-1, KV_HEADS, ATT_HEAD_DIM)
    v_flat = pool_v.reshape(-1, KV_HEADS, ATT_HEAD_DIM)
    kg = jnp.where(in_past, k_flat[phys], gather_rows(k, loc))
    vg = jnp.where(in_past, v_flat[phys], gather_rows(v, loc))
    return sparse_attend(q, kg, vg, valid)


def run_trunk(x, c, pos, wkv0, shift0, attn_fn, P):
    v_first = None
    ks, vs, kis, wkvs, shifts = [], [], [], [], []
    for i in range(DEPTH):
        j = i // N_MIXERS
        sh_a, sc_a, g_a, sh_f, sc_f, g_f = ada_mods(c, P['w_ada'][i], P['b_ada'][i])
        h = rms_norm(x, P['norm_g'][i, 0]) * (1 + sc_a) + sh_a
        if i % N_MIXERS == 0:
            o, v_first, S, last = rwkv7_mix(h, shift0[j], wkv0[j], v_first, P, j)
            wkvs.append(S)
            shifts.append(last)
        else:
            q, k, v, qi, ki, wi = dsa_project(h, pos, P, j)
            o = attn_fn(j, q, k, v, qi, ki, wi) @ P['att_w_o'][j]
            ks.append(k)
            vs.append(v)
            kis.append(ki)
        x = x + g_a * o
        h = rms_norm(x, P['norm_g'][i, 1]) * (1 + sc_f) + sh_f
        u = jax.nn.relu(h @ P['w_up'][i])
        x = x + g_f * ((u * u) @ P['w_down'][i])
    return x, jnp.stack(ks), jnp.stack(vs), jnp.stack(kis), jnp.stack(wkvs), jnp.stack(shifts)


def setup_inputs(seed: int = 0) -> dict:
    key = jax.random.key(seed)
    ks = iter(jax.random.split(key, 48))
    D = D_MODEL
    def nrm(shape, scale):
        return scale * jax.random.normal(next(ks), shape, F32)
    n_pages = PAST_LEN // PAGE_SIZE
    in_use = DEC_BATCH * n_pages
    pool = in_use + (in_use + 3) // 4
    x_prompt = nrm((BATCH, SEQ, D), 1.0)
    x_sample = nrm((DEC_BATCH, DEC_SEQ, D), 1.0)
    cache_k = nrm((N_DSA, pool, PAGE_SIZE, KV_HEADS, ATT_HEAD_DIM), 1.0)
    cache_v = nrm((N_DSA, pool, PAGE_SIZE, KV_HEADS, ATT_HEAD_DIM), 1.0)
    cache_idx_k = nrm((N_DSA, pool, PAGE_SIZE, IDX_DIM), 1.0)
    state_wkv = nrm((N_RWKV, DEC_BATCH, RW_HEADS, RW_HEAD, RW_HEAD), 0.3)
    state_shift = nrm((N_RWKV, DEC_BATCH, D), 1.0)
    page_table = jax.random.permutation(next(ks), pool)[:in_use].reshape(DEC_BATCH, n_pages).astype(jnp.int32)
    c_prompt = nrm((BATCH, D), 1.0)
    c_sample = nrm((DEC_BATCH, D), 1.0)
    norm_g = 1.0 + nrm((DEPTH, 2, D), 0.05)
    w_ada = nrm((DEPTH, D, 6 * D), 0.5 * D ** -0.5)
    b_ada = nrm((DEPTH, 6 * D), 0.02)
    w_up = nrm((DEPTH, D, D_FF), D ** -0.5)
    w_down = nrm((DEPTH, D_FF, D), D_FF ** -0.5)
    rw_mix = jax.random.uniform(next(ks), (N_RWKV, 6, D), F32, 0.0, 1.0)
    rw_w_rkv = nrm((N_RWKV, 3, D, D), D ** -0.5)
    rw_w_o = nrm((N_RWKV, D, D), D ** -0.5)
    rw_w0 = jnp.linspace(-6.5, -1.5, D, dtype=F32)[None, :] + nrm((N_RWKV, D), 0.1)
    rw_w1 = nrm((N_RWKV, D, LORA_DECAY), D ** -0.5)
    rw_w2 = nrm((N_RWKV, LORA_DECAY, D), 0.3 * LORA_DECAY ** -0.5)
    rw_a0 = nrm((N_RWKV, D), 0.1)
    rw_a1 = nrm((N_RWKV, D, LORA_AAA), D ** -0.5)
    rw_a2 = nrm((N_RWKV, LORA_AAA, D), 0.5 * LORA_AAA ** -0.5)
    rw_v0 = 1.0 + nrm((N_RWKV - 1, D), 0.1)
    rw_v1 = nrm((N_RWKV - 1, D, LORA_MV), D ** -0.5)
    rw_v2 = nrm((N_RWKV - 1, LORA_MV, D), 0.5 * LORA_MV ** -0.5)
    rw_g1 = nrm((N_RWKV, D, LORA_GATE), D ** -0.5)
    rw_g2 = nrm((N_RWKV, LORA_GATE, D), LORA_GATE ** -0.5)
    rw_k_k = 0.85 + nrm((N_RWKV, D), 0.05)
    rw_k_a = 1.0 + nrm((N_RWKV, D), 0.05)
    rw_r_k = nrm((N_RWKV, RW_HEADS, RW_HEAD), 0.1)
    rw_lnx_w = 1.0 + nrm((N_RWKV, D), 0.05)
    rw_lnx_b = nrm((N_RWKV, D), 0.02)
    att_w_in = nrm((N_DSA, D, P_IN), D ** -0.5)
    att_w_o = nrm((N_DSA, Q_W, D), Q_W ** -0.5)
    att_q_norm = 1.0 + nrm((N_DSA, ATT_HEAD_DIM), 0.05)
    att_k_norm = 1.0 + nrm((N_DSA, ATT_HEAD_DIM), 0.05)
    idx_k_norm = 1.0 + nrm((N_DSA, IDX_DIM), 0.05)
    return {'x_prompt': x_prompt, 'x_sample': x_sample, 'cache_k': cache_k, 'cache_v': cache_v,
            'cache_idx_k': cache_idx_k, 'state_wkv': state_wkv, 'state_shift': state_shift,
            'page_table': page_table, 'c_prompt': c_prompt, 'c_sample': c_sample,
            'norm_g': norm_g, 'w_ada': w_ada, 'b_ada': b_ada, 'w_up': w_up, 'w_down': w_down,
            'rw_mix': rw_mix, 'rw_w_rkv': rw_w_rkv, 'rw_w_o': rw_w_o, 'rw_w0': rw_w0, 'rw_w1': rw_w1,
            'rw_w2': rw_w2, 'rw_a0': rw_a0, 'rw_a1': rw_a1, 'rw_a2': rw_a2, 'rw_v0': rw_v0,
            'rw_v1': rw_v1, 'rw_v2': rw_v2, 'rw_g1': rw_g1, 'rw_g2': rw_g2, 'rw_k_k': rw_k_k,
            'rw_k_a': rw_k_a, 'rw_r_k': rw_r_k, 'rw_lnx_w': rw_lnx_w, 'rw_lnx_b': rw_lnx_b,
            'att_w_in': att_w_in, 'att_w_o': att_w_o, 'att_q_norm': att_q_norm,
            'att_k_norm': att_k_norm, 'idx_k_norm': idx_k_norm}


def reference(x_prompt, x_sample, cache_k, cache_v, cache_idx_k, state_wkv, state_shift, page_table,
              c_prompt, c_sample, norm_g, w_ada, b_ada, w_up, w_down, rw_mix, rw_w_rkv, rw_w_o,
              rw_w0, rw_w1, rw_w2, rw_a0, rw_a1, rw_a2, rw_v0, rw_v1, rw_v2, rw_g1, rw_g2,
              rw_k_k, rw_k_a, rw_r_k, rw_lnx_w, rw_lnx_b, att_w_in, att_w_o, att_q_norm,
              att_k_norm, idx_k_norm):
    P = dict(norm_g=norm_g, w_ada=w_ada, b_ada=b_ada, w_up=w_up, w_down=w_down,
             rw_mix=rw_mix, rw_w_rkv=rw_w_rkv, rw_w_o=rw_w_o, rw_w0=rw_w0, rw_w1=rw_w1,
             rw_w2=rw_w2, rw_a0=rw_a0, rw_a1=rw_a1, rw_a2=rw_a2, rw_v0=rw_v0, rw_v1=rw_v1,
             rw_v2=rw_v2, rw_g1=rw_g1, rw_g2=rw_g2, rw_k_k=rw_k_k, rw_k_a=rw_k_a, rw_r_k=rw_r_k,
             rw_lnx_w=rw_lnx_w, rw_lnx_b=rw_lnx_b, att_w_in=att_w_in, att_w_o=att_w_o,
             att_q_norm=att_q_norm, att_k_norm=att_k_norm, idx_k_norm=idx_k_norm)

    def prompt_attn(j, q, k, v, qi, ki, wi):
        return dsa_prompt(q, k, v, qi, ki, wi)

    def sample_attn(j, q, k, v, qi, ki, wi):
        return dsa_sample(q, k, v, qi, ki, wi, cache_k[j], cache_v[j], cache_idx_k[j], page_table)

    wkv0 = jnp.zeros((N_RWKV, BATCH, RW_HEADS, RW_HEAD, RW_HEAD), F32)
    shift0 = jnp.zeros((N_RWKV, BATCH, D_MODEL), x_prompt.dtype)
    y_prompt, k_p, v_p, ki_p, wkv_p, shift_p = run_trunk(
        x_prompt, c_prompt, jnp.arange(SEQ), wkv0, shift0, prompt_attn, P)
    y_sample, k_s, v_s, ki_s, wkv_s, shift_s = run_trunk(
        x_sample, c_sample, PAST_LEN + jnp.arange(DEC_SEQ), state_wkv, state_shift, sample_attn, P)
    return (y_prompt, y_sample, k_p, v_p, ki_p, wkv_p, shift_p, k_s, v_s, ki_s, wkv_s, shift_s)
```

```python
import functools

import jax
import jax.numpy as jnp
import numpy as np
from jax import lax
from jax.experimental import pallas as pl
from jax.experimental.pallas import tpu as pltpu

F32 = jnp.float32
BF16 = jnp.bfloat16
I32 = jnp.int32

NORM_EPS = 1e-6
GN_EPS = 64e-5
RW_HEAD = 64
ATT_HEADS = 16
ATT_HEAD_DIM = 64
KV_HEADS = 4
GROUPS = ATT_HEADS // KV_HEADS
IDX_HEADS = 8
IDX_DIM = 64
TOPK_MAX = 256
ROPE_THETA = 500000.0
ROT_FRAC = 4
PAGE = 128

LANES = 128
MXU_DIM = 256
VMEM_LIMIT = 56 * 1024 * 1024

INT_MIN = -(2 ** 31)
NEG_INF_KEY = INT_MIN + 0x7FFFFF
MASK_BIAS = -1e30


def _cparams(*sem):
    return pltpu.CompilerParams(dimension_semantics=sem, vmem_limit_bytes=VMEM_LIMIT)


def _const_spec(shape):
    n = len(shape)
    return pl.BlockSpec(shape, lambda *_: (0,) * n, pipeline_mode=pl.Buffered(1))


def _bdot(a, b):
    return jnp.dot(a.astype(BF16), b.astype(BF16), preferred_element_type=F32)


def _bdot_nt(a, b):
    return lax.dot_general(a.astype(BF16), b.astype(BF16), (((1,), (1,)), ((), ())),
                           preferred_element_type=F32)


def _rms_mod(x, g, sc, sh):
    ms = jnp.mean(x * x, axis=-1, keepdims=True)
    return x * lax.rsqrt(ms + NORM_EPS) * g * (1.0 + sc) + sh


def _seg_sum64(x, ones):
    hi = x.astype(BF16)
    lo = (x - hi.astype(F32)).astype(BF16)
    w = ones.shape[0]
    outs = []
    for c in range(x.shape[-1] // w):
        sl = slice(c * w, (c + 1) * w)
        outs.append(jnp.dot(hi[:, sl], ones, preferred_element_type=F32)
                    + jnp.dot(lo[:, sl], ones, preferred_element_type=F32))
    return outs[0] if len(outs) == 1 else jnp.concatenate(outs, axis=-1)


def _block_ones(width):
    i = np.arange(width) // 64
    return jnp.asarray((i[:, None] == i[None, :]).astype(np.float32), dtype=BF16)


def _ada_kernel(c_ref, w_ref, b_ref, o_ref):
    c = c_ref[...]
    s = c * jax.nn.sigmoid(c)
    o_ref[0] = _bdot(s, w_ref[0]) + b_ref[0]


def _ada(c_all, w_ada, b_ada):
    depth, d, d6 = w_ada.shape
    nb = c_all.shape[0]
    tn = 1536
    return pl.pallas_call(
        _ada_kernel,
        grid=(depth, d6 // tn),
        in_specs=[pl.BlockSpec((nb, d), lambda i, j: (0, 0)),
                  pl.BlockSpec((1, d, tn), lambda i, j: (i, 0, j)),
                  pl.BlockSpec((1, 1, tn), lambda i, j: (i, 0, j))],
        out_specs=pl.BlockSpec((1, nb, tn), lambda i, j: (i, 0, j)),
        out_shape=jax.ShapeDtypeStruct((depth, nb, d6), F32),
        compiler_params=_cparams("parallel", "parallel"),
        name="ada",
    )(c_all, w_ada, b_ada.reshape(depth, 1, d6))


class _Group:
    def __init__(self, n_seq, seq_len, tile):
        self.n_seq, self.seq_len = n_seq, seq_len
        self.rows = n_seq * seq_len
        self.tile = min(tile, self.rows)
        assert self.rows % self.tile == 0
        assert seq_len % self.tile == 0 or self.tile % seq_len == 0
        self.tiles_per_seq = max(seq_len // self.tile, 1)
        self.per_seq = seq_len >= self.tile
        self.n_tiles = self.rows // self.tile

    def seq_rows(self, m):
        d = m.shape[-1]
        if self.per_seq:
            return m.reshape(self.n_seq, 1, d)
        return jnp.repeat(m, self.seq_len, axis=0).reshape(self.n_tiles, self.tile, d)

    def seq_spec(self, d):
        r = 1 if self.per_seq else self.tile
        tps = self.tiles_per_seq
        return pl.BlockSpec((1, r, d), lambda i, *_: (i // tps, 0, 0))

    def pos_rows(self, tab):
        if self.per_seq:
            return tab
        return jnp.tile(tab, (self.tile // self.seq_len, 1))

    def pos_spec(self, w):
        tps = self.tiles_per_seq
        return pl.BlockSpec((self.tile, w), lambda i, *_: (i % tps, 0))

    def row_spec(self, w):
        return pl.BlockSpec((self.tile, w), lambda i, *_: (i, 0))


def _ffn_kernel(x_ref, g_ref, sc_ref, sh_ref, gate_ref, wu_ref, wd_ref, o_ref, h_sc, acc_sc):
    j = pl.program_id(1)

    @pl.when(j == 0)
    def _():
        h_sc[...] = _rms_mod(x_ref[...], g_ref[...], sc_ref[0], sh_ref[0]).astype(BF16)
        acc_sc[...] = jnp.zeros_like(acc_sc)

    u = jnp.maximum(jnp.dot(h_sc[...], wu_ref[...], preferred_element_type=F32), 0.0)
    acc_sc[...] += jnp.dot((u * u).astype(BF16), wd_ref[...], preferred_element_type=F32)

    @pl.when(j == pl.num_programs(1) - 1)
    def _():
        o_ref[...] = x_ref[...] + gate_ref[0] * acc_sc[...]


def _ffn(grp, x, g, sc, sh, gate, w_up, w_down):
    d, dff = w_up.shape
    tf = 512
    tm = grp.tile
    return pl.pallas_call(
        _ffn_kernel,
        grid=(grp.n_tiles, dff // tf),
        in_specs=[grp.row_spec(d), pl.BlockSpec((1, d), lambda i, j: (0, 0)),
                  grp.seq_spec(d), grp.seq_spec(d), grp.seq_spec(d),
                  pl.BlockSpec((d, tf), lambda i, j: (0, j)),
                  pl.BlockSpec((tf, d), lambda i, j: (j, 0))],
        out_specs=grp.row_spec(d),
        out_shape=jax.ShapeDtypeStruct(x.shape, F32),
        scratch_shapes=[pltpu.VMEM((tm, d), BF16), pltpu.VMEM((tm, d), F32)],
        compiler_params=_cparams("parallel", "arbitrary"),
        name="ffn",
    )(x, g, grp.seq_rows(sc), grp.seq_rows(sh), grp.seq_rows(gate), w_up, w_down)


def _rwkv_proj_kernel(has_vfirst, seq_len, *refs):
    if has_vfirst:
        (x_ref, g_ref, sc_ref, sh_ref, shift_ref, mix_ref, wr_ref, wk_ref, wv_ref,
         w1_ref, w2_ref, a1_ref, a2_ref, g1_ref, g2_ref, vec_ref, ones_ref,
         v1_ref, v2_ref, vf_ref,
         r_o, w_o, k_o, v_o, a_o, b_o, g_o, h_o, carry_sc) = refs
    else:
        (x_ref, g_ref, sc_ref, sh_ref, shift_ref, mix_ref, wr_ref, wk_ref, wv_ref,
         w1_ref, w2_ref, a1_ref, a2_ref, g1_ref, g2_ref, vec_ref, ones_ref,
         r_o, w_o, k_o, v_o, a_o, b_o, g_o, h_o, carry_sc) = refs
    i = pl.program_id(0)
    tt = x_ref.shape[0]

    @pl.when(i == 0)
    def _():
        carry_sc[...] = jnp.zeros_like(carry_sc)

    h = _rms_mod(x_ref[...], g_ref[...], sc_ref[0], sh_ref[0])
    h_o[...] = h
    row = lax.broadcasted_iota(I32, h.shape, 0)
    hp = pltpu.roll(h, 1, axis=0)
    hp = jnp.where(row == 0, carry_sc[...], hp)
    hp = jnp.where(((row + i * tt) & (seq_len - 1)) == 0, shift_ref[0], hp)
    carry_sc[...] = h[tt - 1:tt, :]
    dx = hp - h
    mix = mix_ref[...]
    xr, xw, xk, xv, xa, xg = [h + dx * mix[n:n + 1, :] for n in range(6)]
    vec = vec_ref[...]
    w0, a0, k_k, k_a, v0 = [vec[n:n + 1, :] for n in range(5)]

    r = jnp.dot(xr.astype(BF16), wr_ref[...], preferred_element_type=F32)
    k = jnp.dot(xk.astype(BF16), wk_ref[...], preferred_element_type=F32)
    v = jnp.dot(xv.astype(BF16), wv_ref[...], preferred_element_type=F32)
    w_pre = w0 + _bdot(jnp.tanh(_bdot(xw, w1_ref[...])), w2_ref[...])
    decay = jnp.exp(-float(np.exp(-0.5)) * jax.nn.sigmoid(w_pre))
    if has_vfirst:
        vgate = jax.nn.sigmoid(v0 + _bdot(_bdot(xv, v1_ref[...]), v2_ref[...]))
        v = v + (vf_ref[...] - v) * vgate
    a = jax.nn.sigmoid(a0 + _bdot(_bdot(xa, a1_ref[...]), a2_ref[...]))
    g_o[...] = _bdot(jax.nn.sigmoid(_bdot(xg, g1_ref[...])), g2_ref[...])
    kk = k * k_k
    kk = kk * lax.rsqrt(jnp.maximum(_seg_sum64(kk * kk, ones_ref[...]), 1e-24))
    r_o[...] = r
    w_o[...] = decay
    k_o[...] = k * (1.0 + (a - 1.0) * k_a)
    v_o[...] = v
    a_o[...] = -kk
    b_o[...] = kk * a


def _pad_to(w, axis, mult):
    n = w.shape[axis]
    pad = (-n) % mult
    if pad == 0:
        return w
    cfg = [(0, 0)] * w.ndim
    cfg[axis] = (0, pad)
    return jnp.pad(w, cfg)


def _lora_pair(w_in, w_out):
    return (_pad_to(w_in, 1, LANES).astype(BF16), _pad_to(w_out, 0, LANES).astype(BF16))


def _rwkv_proj(grp, x, g, sc, sh, shift, P, j, v_first):
    d = x.shape[-1]
    assert grp.seq_len & (grp.seq_len - 1) == 0, "sequence-start test uses a bit mask"
    has_vfirst = v_first is not None
    w1, w2 = _lora_pair(P['rw_w1'][j], P['rw_w2'][j])
    a1, a2 = _lora_pair(P['rw_a1'][j], P['rw_a2'][j])
    g1, g2 = _lora_pair(P['rw_g1'][j], P['rw_g2'][j])
    v0 = P['rw_v0'][j - 1] if has_vfirst else jnp.zeros((d,), F32)
    vec = jnp.stack([P['rw_w0'][j], P['rw_a0'][j], P['rw_k_k'][j], P['rw_k_a'][j], v0,
                     jnp.zeros((d,), F32), jnp.zeros((d,), F32), jnp.zeros((d,), F32)])
    mix = _pad_to(P['rw_mix'][j], 0, 8)
    wrkv = P['rw_w_rkv'][j].astype(BF16)
    ones = _block_ones(MXU_DIM)
    args = [x, g, grp.seq_rows(sc), grp.seq_rows(sh), grp.seq_rows(shift), mix,
            wrkv[0], wrkv[1], wrkv[2], w1, w2, a1, a2, g1, g2, vec, ones]
    specs = [grp.row_spec(d), _const_spec((1, d)), grp.seq_spec(d), grp.seq_spec(d), grp.seq_spec(d),
             _const_spec(mix.shape)] + [_const_spec(a.shape) for a in args[6:]]
    if has_vfirst:
        v1, v2 = _lora_pair(P['rw_v1'][j - 1], P['rw_v2'][j - 1])
        args += [v1, v2, v_first]
        specs += [_const_spec(v1.shape), _const_spec(v2.shape), grp.row_spec(d)]
    out = jax.ShapeDtypeStruct(x.shape, F32)
    return pl.pallas_call(
        functools.partial(_rwkv_proj_kernel, has_vfirst, grp.seq_len),
        grid=(grp.n_tiles,),
        in_specs=specs,
        out_specs=[grp.row_spec(d)] * 8,
        out_shape=[out] * 8,
        scratch_shapes=[pltpu.VMEM((1, d), F32)],
        compiler_params=_cparams("arbitrary"),
        name="rwkv_proj",
    )(*args)


def _wkv_kernel(r_ref, w_ref, k_ref, a_ref, b_ref, v_ref, s0_ref, y_ref, s_ref):
    c = pl.program_id(1)
    tc, nk, _ = r_ref.shape

    @pl.when(c == 0)
    def _():
        s_ref[...] = s0_ref[...]

    def step(t, carry):
        vt = v_ref[t]
        parts = [None] * 4
        for kk in range(nk):
            term = s_ref[kk] * a_ref[t, pl.ds(kk, 1), :]
            parts[kk % 4] = term if parts[kk % 4] is None else parts[kk % 4] + term
        sa = (parts[0] + parts[1]) + (parts[2] + parts[3])
        parts = [None] * 4
        for kk in range(nk):
            s_new = (s_ref[kk] * w_ref[t, pl.ds(kk, 1), :] + sa * b_ref[t, pl.ds(kk, 1), :]
                     + vt * k_ref[t, pl.ds(kk, 1), :])
            s_ref[kk] = s_new
            term = s_new * r_ref[t, pl.ds(kk, 1), :]
            parts[kk % 4] = term if parts[kk % 4] is None else parts[kk % 4] + term
        y_ref[t] = (parts[0] + parts[1]) + (parts[2] + parts[3])
        return carry

    lax.fori_loop(0, tc, step, 0)


def _wkv_scan(r, w, k, a, b, v, s0, tc):
    t_len, nk, lanes = r.shape
    nv = v.shape[1]
    tc = min(tc, t_len)
    kspec = pl.BlockSpec((tc, nk, LANES), lambda l, c: (c, 0, l))
    vspec = pl.BlockSpec((tc, nv, LANES), lambda l, c: (c, 0, l))
    sspec = pl.BlockSpec((nk, nv, LANES), lambda l, c: (0, 0, l))
    return pl.pallas_call(
        _wkv_kernel,
        grid=(lanes // LANES, t_len // tc),
        in_specs=[kspec] * 5 + [vspec, sspec],
        out_specs=[vspec, sspec],
        out_shape=[jax.ShapeDtypeStruct(v.shape, F32), jax.ShapeDtypeStruct(s0.shape, F32)],
        compiler_params=_cparams("parallel", "arbitrary"),
        name="wkv_scan",
    )(r, w, k, a, b, v, s0)


def _to_klanes(x, n_seq, seq_len, dup):
    h = x.shape[-1] // RW_HEAD
    y = x.reshape(n_seq, seq_len, h, RW_HEAD).transpose(1, 3, 0, 2).reshape(seq_len, RW_HEAD, n_seq * h)
    return jnp.concatenate([y] * dup, axis=-1) if dup > 1 else y


def _to_vlanes(x, n_seq, seq_len, dup):
    h = x.shape[-1] // RW_HEAD
    y = x.reshape(n_seq, seq_len, h, dup, RW_HEAD // dup).transpose(1, 4, 3, 0, 2)
    return y.reshape(seq_len, RW_HEAD // dup, dup * n_seq * h)


def _from_vlanes(y, n_seq, seq_len, dup):
    h = y.shape[-1] // (dup * n_seq)
    y = y.reshape(seq_len, RW_HEAD // dup, dup, n_seq, h).transpose(3, 0, 4, 2, 1)
    return y.reshape(n_seq * seq_len, h * RW_HEAD)


def _state_to_lanes(s, dup):
    n_seq, h = s.shape[:2]
    y = s.reshape(n_seq, h, dup, RW_HEAD // dup, RW_HEAD).transpose(4, 3, 2, 0, 1)
    return y.reshape(RW_HEAD, RW_HEAD // dup, dup * n_seq * h)


def _state_from_lanes(s, n_seq, dup):
    h = s.shape[-1] // (dup * n_seq)
    y = s.reshape(RW_HEAD, RW_HEAD // dup, dup, n_seq, h).transpose(3, 4, 2, 1, 0)
    return y.reshape(n_seq, h, RW_HEAD, RW_HEAD)


def _wkv(grp, r, w, k, v, a, b, s0):
    n_seq, seq_len = grp.n_seq, grp.seq_len
    heads = r.shape[-1] // RW_HEAD
    dup = max(1, LANES // (n_seq * heads))
    kl = [_to_klanes(t, n_seq, seq_len, dup) for t in (r, w, k, a, b)]
    y, s = _wkv_scan(*kl, _to_vlanes(v, n_seq, seq_len, dup), _state_to_lanes(s0, dup), tc=64)
    return _from_vlanes(y, n_seq, seq_len, dup), _state_from_lanes(s, n_seq, dup)


def _rwkv_post_kernel(x_ref, y_ref, r_ref, k_ref, v_ref, g_ref, gate_ref, vec_ref, ones_ref, wo_ref, o_ref):
    vec = vec_ref[...]
    lnx_w, lnx_b, r_k = [vec[n:n + 1, :] for n in range(3)]
    y = y_ref[...]
    ones = ones_ref[...]
    inv = 1.0 / RW_HEAD
    mu = _seg_sum64(y, ones) * inv
    yc = y - mu
    var = _seg_sum64(yc * yc, ones) * inv
    yn = yc * lax.rsqrt(var + GN_EPS) * lnx_w + lnx_b
    bonus = _seg_sum64(r_ref[...] * k_ref[...] * r_k, ones) * v_ref[...]
    out = jnp.dot(((yn + bonus) * g_ref[...]).astype(BF16), wo_ref[...], preferred_element_type=F32)
    o_ref[...] = x_ref[...] + gate_ref[0] * out


def _rwkv_post(grp, x, y, r, k, v, g, gate, P, j):
    d = x.shape[-1]
    vec = _pad_to(jnp.stack([P['rw_lnx_w'][j], P['rw_lnx_b'][j], P['rw_r_k'][j].reshape(d)]), 0, 8)
    ones = _block_ones(MXU_DIM)
    wo = P['rw_w_o'][j].astype(BF16)
    return pl.pallas_call(
        _rwkv_post_kernel,
        grid=(grp.n_tiles,),
        in_specs=[grp.row_spec(d)] * 6 + [grp.seq_spec(d), _const_spec(vec.shape), _const_spec(ones.shape),
                                            _const_spec(wo.shape)],
        out_specs=grp.row_spec(d),
        out_shape=jax.ShapeDtypeStruct(x.shape, F32),
        compiler_params=_cparams("parallel"),
        name="rwkv_post",
    )(x, y, r, k, v, g, grp.seq_rows(gate), vec, ones, wo)


Q_W = ATT_HEADS * ATT_HEAD_DIM
KV_W = KV_HEADS * ATT_HEAD_DIM
IQ_W = IDX_HEADS * IDX_DIM
OFF_K = Q_W
OFF_V = Q_W + KV_W
OFF_QI = Q_W + 2 * KV_W
OFF_KI = OFF_QI + IQ_W


def _rope_tables(pos):
    rd = ATT_HEAD_DIM // ROT_FRAC
    half = rd // 2
    inv = ROPE_THETA ** (-jnp.arange(half, dtype=F32) * 2.0 / rd)
    ang = pos.astype(F32)[:, None] * inv[None, :]
    cos, sin = jnp.cos(ang), jnp.sin(ang)
    n = pos.shape[0]
    rest = ATT_HEAD_DIM - rd
    c = jnp.concatenate([cos, cos, jnp.ones((n, rest), F32)], axis=-1)
    s_up = jnp.concatenate([-sin, jnp.zeros((n, half + rest), F32)], axis=-1)
    s_dn = jnp.concatenate([jnp.zeros((n, half), F32), sin, jnp.zeros((n, rest), F32)], axis=-1)
    return [jnp.concatenate([t, t], axis=-1) for t in (c, s_up, s_dn)]


def _rope128(x, c, s_up, s_dn):
    half = ATT_HEAD_DIM // ROT_FRAC // 2
    return x * c + pltpu.roll(x, LANES - half, axis=1) * s_up + pltpu.roll(x, half, axis=1) * s_dn


def _dsa_proj_kernel(x_ref, g_ref, sc_ref, sh_ref, w_ref, c_ref, su_ref, sd_ref, qg_ref, kg_ref, ig_ref,
                     ones_ref, q_o, k_o, v_o, qi_o, kiwi_o):
    h = _rms_mod(x_ref[...], g_ref[...], sc_ref[0], sh_ref[0])
    z = jnp.dot(h.astype(BF16), w_ref[...], preferred_element_type=F32)
    c, s_up, s_dn = c_ref[...], su_ref[...], sd_ref[...]
    ones = ones_ref[...]
    inv = 1.0 / ATT_HEAD_DIM
    lane = lax.broadcasted_iota(I32, (z.shape[0], LANES), 1)
    low = lane < ATT_HEAD_DIM

    def head_norm(t, gain, blk=ones):
        return t * lax.rsqrt(_seg_sum64(t * t, blk) * inv + NORM_EPS) * gain

    def pairs(t):
        return [t[:, n * LANES:(n + 1) * LANES] for n in range(t.shape[-1] // LANES)]

    q = head_norm(z[:, :Q_W], qg_ref[...]) * (ATT_HEAD_DIM ** -0.5)
    for n, t in enumerate(pairs(q)):
        t = _rope128(t, c, s_up, s_dn).astype(BF16)
        q_o[0, 2 * n] = t[:, :ATT_HEAD_DIM]
        q_o[0, 2 * n + 1] = t[:, ATT_HEAD_DIM:]
    k = head_norm(z[:, OFF_K:OFF_V], kg_ref[...])
    k_o[...] = jnp.concatenate([_rope128(t, c, s_up, s_dn) for t in pairs(k)], axis=-1)
    v_o[...] = z[:, OFF_V:OFF_QI]
    qi = z[:, OFF_QI:OFF_KI] * (IDX_DIM ** -0.5)
    for n, t in enumerate(pairs(qi)):
        t = _rope128(t, c, s_up, s_dn)
        qi_o[0, 2 * n] = jnp.where(low, t, 0.0).astype(BF16)
        qi_o[0, 2 * n + 1] = jnp.where(low, pltpu.roll(t, ATT_HEAD_DIM, axis=1), 0.0).astype(BF16)
    kw = z[:, OFF_KI:OFF_KI + LANES]
    ki = _rope128(head_norm(kw, ig_ref[...], ones[:LANES, :LANES]), c, s_up, s_dn)
    kiwi_o[...] = jnp.where(low, ki, kw * (IDX_HEADS ** -0.5))


def _dsa_proj(grp, x, g, sc, sh, pos, P, j):
    d = x.shape[-1]
    w = _pad_to(P['att_w_in'][j], 1, LANES).astype(BF16)
    tabs = [grp.pos_rows(t) for t in _rope_tables(pos)]
    qg = jnp.tile(P['att_q_norm'][j], ATT_HEADS).reshape(1, Q_W)
    kg = jnp.tile(P['att_k_norm'][j], KV_HEADS).reshape(1, KV_W)
    ig = _pad_to(P['idx_k_norm'][j], 0, LANES).reshape(1, LANES)
    ones = _block_ones(MXU_DIM)
    nsb = grp.n_seq if grp.per_seq else grp.n_tiles
    rows = grp.seq_len if grp.per_seq else grp.tile
    tps = grp.tiles_per_seq
    tt = grp.tile

    def head_spec(nh, wd):
        return pl.BlockSpec((1, nh, tt, wd), lambda i: (i // tps, 0, i % tps, 0))

    return pl.pallas_call(
        _dsa_proj_kernel,
        grid=(grp.n_tiles,),
        in_specs=[grp.row_spec(d), _const_spec((1, d)), grp.seq_spec(d), grp.seq_spec(d), _const_spec(w.shape),
                  grp.pos_spec(LANES), grp.pos_spec(LANES), grp.pos_spec(LANES),
                  _const_spec(qg.shape), _const_spec(kg.shape), _const_spec(ig.shape), _const_spec(ones.shape)],
        out_specs=[head_spec(ATT_HEADS, ATT_HEAD_DIM), grp.row_spec(KV_W), grp.row_spec(KV_W),
                   head_spec(IDX_HEADS, LANES), grp.row_spec(LANES)],
        out_shape=[jax.ShapeDtypeStruct((nsb, ATT_HEADS, rows, ATT_HEAD_DIM), BF16),
                   jax.ShapeDtypeStruct((grp.rows, KV_W), F32),
                   jax.ShapeDtypeStruct((grp.rows, KV_W), F32),
                   jax.ShapeDtypeStruct((nsb, IDX_HEADS, rows, LANES), BF16),
                   jax.ShapeDtypeStruct((grp.rows, LANES), F32)],
        compiler_params=_cparams("parallel"),
        name="dsa_proj",
    )(x, g, grp.seq_rows(sc), grp.seq_rows(sh), w, *tabs, qg, kg, ig, ones)


def _score_key(score, admissible):
    bits = pltpu.bitcast(score + 0.0, I32)
    key = jnp.where(bits >= 0, bits, bits ^ 0x7FFFFFFF)
    return jnp.where(admissible, key, NEG_INF_KEY)


def _topk_threshold(count, shape, topk, idx_bits):
    kf = float(topk)
    theta = jnp.where(count(lambda k, i: k >= 0) >= kf, jnp.full(shape, 0, I32), jnp.full(shape, INT_MIN, I32))

    def value_bit(n, theta):
        cand = theta + jnp.left_shift(jnp.int32(1), 30 - n)
        return jnp.where(count(lambda k, i: k >= cand) >= kf, cand, theta)

    theta = lax.fori_loop(0, 31, value_bit, theta)
    need = kf - count(lambda k, i: k > theta)

    def index_bit(n, cut):
        cand = cut + jnp.left_shift(jnp.int32(1), idx_bits - 1 - n)
        below = count(lambda k, i: jnp.where(k == theta, i, cand) < cand)
        return jnp.where(below < need, cand, cut)

    cut = lax.fori_loop(0, idx_bits, index_bit, jnp.zeros(shape, I32))
    return theta, cut


def _select_bias(key, idx, theta, cut):
    tie = jnp.where(idx <= cut, 0.0, MASK_BIAS)
    bias = jnp.where(key > theta, 0.0, jnp.where(key == theta, tie, MASK_BIAS))
    return jnp.where(key > NEG_INF_KEY, bias, MASK_BIAS)


def _dsa_prompt_kernel(topk, q_ref, qi_ref, kiwiq_ref, x_ref, gate_ref, k_ref, v_ref, kiwi_ref, place_ref,
                       wo_ref, o_ref, key_sc, bias_sc, m_sc, l_sc, acc_sc):
    qb = pl.program_id(1)
    tq = x_ref.shape[0]
    n_chunks, _, kc = key_sc.shape
    lt = kc // LANES
    nkc = (qb * tq + tq - 1) // kc + 1
    lane = lax.broadcasted_iota(I32, (tq, LANES), 1)
    row = lax.broadcasted_iota(I32, (tq, LANES), 0)
    qpos = qb * tq + row

    qi = qi_ref[0].reshape(IDX_HEADS * tq, LANES)
    wq = kiwiq_ref[...]
    wib = [jnp.broadcast_to(wq[:, IDX_DIM + h:IDX_DIM + h + 1], (tq, LANES)) for h in range(IDX_HEADS)]

    def score_chunk(c, carry):
        kic = kiwi_ref[pl.ds(pl.multiple_of(c * kc, kc), kc), :]
        logits = _bdot_nt(qi, kic)
        keys = []
        for j in range(lt):
            acc = None
            for h in range(IDX_HEADS):
                t = jnp.maximum(logits[h * tq:(h + 1) * tq, j * LANES:(j + 1) * LANES], 0.0) * wib[h]
                acc = t if acc is None else acc + t
            kpos = c * kc + j * LANES + lane
            keys.append(_score_key(acc, kpos <= qpos))
        key_sc[c] = jnp.concatenate(keys, axis=-1)
        return carry

    lax.fori_loop(0, nkc, score_chunk, 0)

    def count(pred):
        def body(c, acc):
            k = key_sc[c]
            for j in range(lt):
                m = pred(k[:, j * LANES:(j + 1) * LANES], c * kc + j * LANES + lane)
                acc = acc + jnp.where(m, 1.0, 0.0)
            return acc
        acc = lax.fori_loop(0, nkc, body, jnp.zeros((tq, LANES), F32))
        return jnp.sum(acc, axis=1, keepdims=True)

    theta, cut = _topk_threshold(count, (tq, LANES), topk, (n_chunks * kc - 1).bit_length())

    def bias_chunk(c, carry):
        k = key_sc[c]
        parts = [_select_bias(k[:, j * LANES:(j + 1) * LANES], c * kc + j * LANES + lane, theta, cut)
                 for j in range(lt)]
        bias_sc[c] = jnp.concatenate(parts, axis=-1)
        return carry

    lax.fori_loop(0, nkc, bias_chunk, 0)

    out = jnp.zeros((tq, o_ref.shape[1]), F32)
    for g in range(KV_HEADS):
        qg = q_ref[0, g * GROUPS:(g + 1) * GROUPS].reshape(GROUPS * tq, ATT_HEAD_DIM)
        qblk = jnp.dot(qg, place_ref[g], preferred_element_type=F32).astype(BF16)
        m_sc[...] = jnp.full_like(m_sc, MASK_BIAS)
        l_sc[...] = jnp.zeros_like(l_sc)
        acc_sc[...] = jnp.zeros_like(acc_sc)

        def att_chunk(c, carry):
            rows = pl.ds(pl.multiple_of(c * kc, kc), kc)
            s = _bdot_nt(qblk, k_ref[rows, :])
            s = (s.reshape(GROUPS, tq, kc) + bias_sc[c][None]).reshape(GROUPS * tq, kc)
            m_old = m_sc[...]
            m_new = jnp.maximum(m_old, jnp.max(s, axis=1, keepdims=True))
            p = jnp.exp(s - m_new)
            alpha = jnp.exp(m_old - m_new)
            l_sc[...] = alpha * l_sc[...] + jnp.sum(p, axis=1, keepdims=True)
            acc_sc[...] = alpha * acc_sc[...] + _bdot(p, v_ref[rows, :])
            m_sc[...] = m_new
            return carry

        lax.fori_loop(0, nkc, att_chunk, 0)
        o = acc_sc[...] / l_sc[...]
        for hg in range(GROUPS):
            head = g * GROUPS + hg
            oh = o[hg * tq:(hg + 1) * tq, g * ATT_HEAD_DIM:(g + 1) * ATT_HEAD_DIM]
            out = out + _bdot(oh, wo_ref[head * ATT_HEAD_DIM:(head + 1) * ATT_HEAD_DIM, :])
    o_ref[...] = x_ref[...] + gate_ref[0] * out


def _head_placement():
    p = np.zeros((KV_HEADS, ATT_HEAD_DIM, KV_W), np.float32)
    for g in range(KV_HEADS):
        p[g, np.arange(ATT_HEAD_DIM), g * ATT_HEAD_DIM + np.arange(ATT_HEAD_DIM)] = 1.0
    return jnp.asarray(p, dtype=BF16)


def _dsa_prompt(n_seq, seq_len, x, gate, q, k, v, qi, kiwi, w_o):
    d = x.shape[-1]
    tq, kc = 128, 512
    assert seq_len % kc == 0
    topk = min(TOPK_MAX, seq_len // 4)
    nq = seq_len // tq
    wo = w_o.astype(BF16)
    place = _head_placement()
    return pl.pallas_call(
        functools.partial(_dsa_prompt_kernel, topk),
        grid=(n_seq, nq),
        in_specs=[pl.BlockSpec((1, ATT_HEADS, tq, ATT_HEAD_DIM), lambda b, i: (b, 0, i, 0)),
                  pl.BlockSpec((1, IDX_HEADS, tq, LANES), lambda b, i: (b, 0, i, 0)),
                  pl.BlockSpec((tq, LANES), lambda b, i: (b * nq + i, 0)),
                  pl.BlockSpec((tq, d), lambda b, i: (b * nq + i, 0)),
                  pl.BlockSpec((1, 1, d), lambda b, i: (b, 0, 0)),
                  pl.BlockSpec((seq_len, KV_W), lambda b, i: (b, 0)),
                  pl.BlockSpec((seq_len, KV_W), lambda b, i: (b, 0)),
                  pl.BlockSpec((seq_len, LANES), lambda b, i: (b, 0)),
                  _const_spec(place.shape), _const_spec(wo.shape)],
        out_specs=pl.BlockSpec((tq, d), lambda b, i: (b * nq + i, 0)),
        out_shape=jax.ShapeDtypeStruct(x.shape, F32),
        scratch_shapes=[pltpu.VMEM((seq_len // kc, tq, kc), I32), pltpu.VMEM((seq_len // kc, tq, kc), F32),
                        pltpu.VMEM((GROUPS * tq, 1), F32), pltpu.VMEM((GROUPS * tq, 1), F32),
                        pltpu.VMEM((GROUPS * tq, KV_W), F32)],
        compiler_params=_cparams("parallel", "arbitrary"),
        name="dsa_prompt",
    )(q, qi, kiwi, x, gate.reshape(n_seq, 1, d), k, v, kiwi, place, wo)


def _dsa_sel_kernel(topk, pt_ref, qi_ref, wib_ref, page_ref, kinew_ref, bias_ref, key_sc):
    p = pl.program_id(1)
    npg = pl.num_programs(1)
    t_new = bias_ref.shape[2]
    qi = qi_ref[0]
    wib = wib_ref[0]

    def score(keys):
        t = jnp.maximum(_bdot_nt(qi, keys), 0.0) * wib
        acc = t[0:t_new]
        for h in range(1, IDX_HEADS):
            acc = acc + t[h * t_new:(h + 1) * t_new]
        return acc

    key_sc[p] = _score_key(score(page_ref[0]), True)

    @pl.when(p == npg - 1)
    def _():
        lane = lax.broadcasted_iota(I32, (t_new, LANES), 1)
        row = lax.broadcasted_iota(I32, (t_new, LANES), 0)
        key_sc[npg] = _score_key(score(kinew_ref[0]), lane <= row)
        keys = key_sc[...]
        idx = (lax.broadcasted_iota(I32, keys.shape, 0) * LANES + lax.broadcasted_iota(I32, keys.shape, 2))

        def count(pred):
            c = jnp.sum(jnp.where(pred(keys, idx), 1.0, 0.0), axis=0)
            return jnp.sum(c, axis=1, keepdims=True)

        theta, cut = _topk_threshold(count, (t_new, LANES), topk, (keys.shape[0] * LANES - 1).bit_length())
        bias_ref[0] = _select_bias(keys, idx, theta, cut)


def _dsa_sel(page_table, qi, wib, cache_ki, kinew, t_new):
    n_seq, npg = page_table.shape
    topk = min(TOPK_MAX, (npg * PAGE + t_new) // 4)
    rows = qi.shape[1]
    grid_spec = pltpu.PrefetchScalarGridSpec(
        num_scalar_prefetch=1,
        grid=(n_seq, npg),
        in_specs=[pl.BlockSpec((1, rows, IDX_DIM), lambda b, p, pt: (b, 0, 0)),
                  pl.BlockSpec((1, rows, LANES), lambda b, p, pt: (b, 0, 0)),
                  pl.BlockSpec((1, PAGE, IDX_DIM), lambda b, p, pt: (pt[b, p], 0, 0)),
                  pl.BlockSpec((1, PAGE, IDX_DIM), lambda b, p, pt: (b, 0, 0))],
        out_specs=pl.BlockSpec((1, npg + 1, t_new, LANES), lambda b, p, pt: (b, 0, 0, 0)),
        scratch_shapes=[pltpu.VMEM((npg + 1, t_new, LANES), I32)])
    return pl.pallas_call(
        functools.partial(_dsa_sel_kernel, topk),
        grid_spec=grid_spec,
        out_shape=jax.ShapeDtypeStruct((n_seq, npg + 1, t_new, LANES), F32),
        compiler_params=_cparams("parallel", "arbitrary"),
        name="dsa_sel",
    )(page_table, qi, wib, cache_ki, kinew)


def _dsa_att_kernel(pt_ref, q_ref, kpage_ref, vpage_ref, knew_ref, vnew_ref, bias_ref, o_ref, m_sc, l_sc, acc_sc):
    p = pl.program_id(1)
    last = pl.num_programs(1) - 1
    rows = q_ref.shape[1]
    t_new = bias_ref.shape[2]

    @pl.when(p == 0)
    def _():
        m_sc[...] = jnp.full_like(m_sc, MASK_BIAS)
        l_sc[...] = jnp.zeros_like(l_sc)
        acc_sc[...] = jnp.zeros_like(acc_sc)

    def attend(kk, vv):
        s = _bdot_nt(q_ref[0], kk)
        s = (s.reshape(rows // t_new, t_new, LANES) + bias_ref[0, 0][None]).reshape(rows, LANES)
        m_old = m_sc[...]
        m_new = jnp.maximum(m_old, jnp.max(s, axis=1, keepdims=True))
        pr = jnp.exp(s - m_new)
        alpha = jnp.exp(m_old - m_new)
        l_sc[...] = alpha * l_sc[...] + jnp.sum(pr, axis=1, keepdims=True)
        acc_sc[...] = alpha * acc_sc[...] + _bdot(pr, vv)
        m_sc[...] = m_new

    @pl.when(p < last)
    def _():
        attend(kpage_ref[0], vpage_ref[0])

    @pl.when(p == last)
    def _():
        attend(knew_ref[0], vnew_ref[0])
        o_ref[0] = acc_sc[...] / l_sc[...]


def _dsa_att(page_table, qblk, cache_k, cache_v, knew, vnew, bias):
    n_seq, npg = page_table.shape
    rows = qblk.shape[1]
    t_new = bias.shape[2]

    def page_map(b, p, pt):
        return (pt[b, jnp.minimum(p, npg - 1)], 0, 0)

    grid_spec = pltpu.PrefetchScalarGridSpec(
        num_scalar_prefetch=1,
        grid=(n_seq, npg + 1),
        in_specs=[pl.BlockSpec((1, rows, KV_W), lambda b, p, pt: (b, 0, 0)),
                  pl.BlockSpec((1, PAGE, KV_W), page_map),
                  pl.BlockSpec((1, PAGE, KV_W), page_map),
                  pl.BlockSpec((1, PAGE, KV_W), lambda b, p, pt: (b, 0, 0)),
                  pl.BlockSpec((1, PAGE, KV_W), lambda b, p, pt: (b, 0, 0)),
                  pl.BlockSpec((1, 1, t_new, LANES), lambda b, p, pt: (b, p, 0, 0))],
        out_specs=pl.BlockSpec((1, rows, KV_W), lambda b, p, pt: (b, 0, 0)),
        scratch_shapes=[pltpu.VMEM((rows, 1), F32), pltpu.VMEM((rows, 1), F32), pltpu.VMEM((rows, KV_W), F32)])
    return pl.pallas_call(
        _dsa_att_kernel,
        grid_spec=grid_spec,
        out_shape=jax.ShapeDtypeStruct((n_seq, rows, KV_W), F32),
        compiler_params=_cparams("parallel", "arbitrary"),
        name="dsa_att",
    )(page_table, qblk, cache_k, cache_v, knew, vnew, bias)


def _linear_res_kernel(x_ref, a_ref, gate_ref, w_ref, o_ref):
    o_ref[...] = x_ref[...] + gate_ref[0] * _bdot(a_ref[...], w_ref[...])


def _linear_res(grp, x, a, gate, w):
    d = x.shape[-1]
    wb = w.astype(BF16)
    return pl.pallas_call(
        _linear_res_kernel,
        grid=(grp.n_tiles,),
        in_specs=[grp.row_spec(d), grp.row_spec(a.shape[-1]), grp.seq_spec(d), _const_spec(wb.shape)],
        out_specs=grp.row_spec(d),
        out_shape=jax.ShapeDtypeStruct(x.shape, F32),
        compiler_params=_cparams("parallel"),
        name="linear_res",
    )(x, a, grp.seq_rows(gate), wb)


def _dsa_sample(grp, x, gate, q, k, v, qi, kiwi, cache_k, cache_v, cache_ki, page_table, w_o):
    n_seq, t_new = grp.n_seq, grp.seq_len
    assert t_new <= PAGE and grp.n_tiles == 1
    pool = cache_k.shape[0]

    def pad_rows(t):
        return _pad_to(t.reshape(n_seq, t_new, t.shape[-1]), 1, PAGE)

    qi_b = qi[0, :, :, :IDX_DIM].reshape(IDX_HEADS, n_seq, t_new, IDX_DIM).transpose(1, 0, 2, 3)
    qi_b = qi_b.reshape(n_seq, IDX_HEADS * t_new, IDX_DIM)
    wi = kiwi[:, IDX_DIM:IDX_DIM + IDX_HEADS].reshape(n_seq, t_new, IDX_HEADS).transpose(0, 2, 1)
    wib = jnp.broadcast_to(wi.reshape(n_seq, IDX_HEADS * t_new, 1), (n_seq, IDX_HEADS * t_new, LANES))
    bias = _dsa_sel(page_table, qi_b, wib, cache_ki, pad_rows(kiwi[:, :IDX_DIM]), t_new)
    q_b = q[0].reshape(KV_HEADS, GROUPS, n_seq, t_new, ATT_HEAD_DIM).transpose(2, 0, 1, 3, 4)
    eye = jnp.eye(KV_HEADS, dtype=q_b.dtype)
    qblk = (q_b[:, :, :, :, None, :] * eye[None, :, None, None, :, None])
    qblk = qblk.reshape(n_seq, ATT_HEADS * t_new, KV_W)
    o = _dsa_att(page_table, qblk, cache_k.reshape(pool, PAGE, KV_W), cache_v.reshape(pool, PAGE, KV_W),
                 pad_rows(k), pad_rows(v), bias)
    o = o.reshape(n_seq, KV_HEADS, GROUPS, t_new, KV_HEADS, ATT_HEAD_DIM)
    o = jnp.stack([o[:, g, :, :, g, :] for g in range(KV_HEADS)], axis=1)
    o = o.transpose(0, 3, 1, 2, 4).reshape(n_seq * t_new, Q_W)
    return _linear_res(grp, x, o, gate, w_o)


TOKEN_TILE = 256
FFN_TILE = 1024


def _trunk(x, mods, pos, wkv0, shift0, attn_fn, P):
    n_seq, seq_len, d = x.shape
    depth = mods.shape[0]
    grp = _Group(n_seq, seq_len, TOKEN_TILE)
    grp_ffn = _Group(n_seq, seq_len, FFN_TILE)
    x = x.reshape(n_seq * seq_len, d)
    v_first = None
    ks, vs, kis, wkvs, shifts = [], [], [], [], []
    for i in range(depth):
        j = i // 2
        sh_a, sc_a, g_a, sh_f, sc_f, g_f = jnp.split(mods[i], 6, axis=-1)
        g_att = P['norm_g'][i, 0].reshape(1, d)
        g_ffn = P['norm_g'][i, 1].reshape(1, d)
        if i % 2 == 0:
            r, w, k, v, a, b, gate, h = _rwkv_proj(grp, x, g_att, sc_a, sh_a, shift0[j], P, j, v_first)
            if j == 0:
                v_first = v
            y, state = _wkv(grp, r, w, k, v, a, b, wkv0[j])
            x = _rwkv_post(grp, x, y, r, k, v, gate, g_a, P, j)
            wkvs.append(state)
            shifts.append(h.reshape(n_seq, seq_len, d)[:, -1])
        else:
            q, k, v, qi, kiwi = _dsa_proj(grp, x, g_att, sc_a, sh_a, pos, P, j)
            x = attn_fn(j, grp, x, g_a, q, k, v, qi, kiwi)
            ks.append(k.reshape(n_seq, seq_len, KV_HEADS, ATT_HEAD_DIM))
            vs.append(v.reshape(n_seq, seq_len, KV_HEADS, ATT_HEAD_DIM))
            kis.append(kiwi[:, :IDX_DIM].reshape(n_seq, seq_len, IDX_DIM))
        x = _ffn(grp_ffn, x, g_ffn, sc_f, sh_f, g_f, P['w_up'][i], P['w_down'][i])
    return (x.reshape(n_seq, seq_len, d), jnp.stack(ks), jnp.stack(vs), jnp.stack(kis),
            jnp.stack(wkvs), jnp.stack(shifts))


def kernel(x_prompt, x_sample, cache_k, cache_v, cache_idx_k, state_wkv, state_shift, page_table,
           c_prompt, c_sample, norm_g, w_ada, b_ada, w_up, w_down, rw_mix, rw_w_rkv, rw_w_o,
           rw_w0, rw_w1, rw_w2, rw_a0, rw_a1, rw_a2, rw_v0, rw_v1, rw_v2, rw_g1, rw_g2,
           rw_k_k, rw_k_a, rw_r_k, rw_lnx_w, rw_lnx_b, att_w_in, att_w_o, att_q_norm,
           att_k_norm, idx_k_norm):
    P = dict(norm_g=norm_g, w_up=w_up.astype(BF16), w_down=w_down.astype(BF16),
             rw_mix=rw_mix, rw_w_rkv=rw_w_rkv, rw_w_o=rw_w_o, rw_w0=rw_w0, rw_w1=rw_w1,
             rw_w2=rw_w2, rw_a0=rw_a0, rw_a1=rw_a1, rw_a2=rw_a2, rw_v0=rw_v0, rw_v1=rw_v1,
             rw_v2=rw_v2, rw_g1=rw_g1, rw_g2=rw_g2, rw_k_k=rw_k_k, rw_k_a=rw_k_a, rw_r_k=rw_r_k,
             rw_lnx_w=rw_lnx_w, rw_lnx_b=rw_lnx_b, att_w_in=att_w_in, att_w_o=att_w_o,
             att_q_norm=att_q_norm, att_k_norm=att_k_norm, idx_k_norm=idx_k_norm)
    n_p, seq, d = x_prompt.shape
    n_s, dec_seq, _ = x_sample.shape
    n_rwkv = state_wkv.shape[0]
    heads = d // RW_HEAD
    past_len = page_table.shape[1] * PAGE

    mods = _ada(jnp.concatenate([c_prompt, c_sample], axis=0), w_ada, b_ada)

    def prompt_attn(j, grp, x, gate, q, k, v, qi, kiwi):
        return _dsa_prompt(grp.n_seq, grp.seq_len, x, gate, q, k, v, qi, kiwi, att_w_o[j])

    def sample_attn(j, grp, x, gate, q, k, v, qi, kiwi):
        return _dsa_sample(grp, x, gate, q, k, v, qi, kiwi, cache_k[j], cache_v[j], cache_idx_k[j],
                           page_table, att_w_o[j])

    wkv0 = jnp.zeros((n_rwkv, n_p, heads, RW_HEAD, RW_HEAD), F32)
    shift0 = jnp.zeros((n_rwkv, n_p, d), F32)
    y_p, k_p, v_p, ki_p, wkv_p, shift_p = _trunk(
        x_prompt, mods[:, :n_p], jnp.arange(seq), wkv0, shift0, prompt_attn, P)
    y_s, k_s, v_s, ki_s, wkv_s, shift_s = _trunk(
        x_sample, mods[:, n_p:], past_len + jnp.arange(dec_seq), state_wkv, state_shift, sample_attn, P)
    return (y_p, y_s, k_p, v_p, ki_p, wkv_p, shift_p, k_s, v_s, ki_s, wkv_s, shift_s)
```

```python
import functools

import jax
import jax.numpy as jnp
import numpy as np
from jax import lax
from jax.experimental import pallas as pl
from jax.experimental.pallas import tpu as pltpu

F32 = jnp.float32
BF16 = jnp.bfloat16
I32 = jnp.int32

NORM_EPS = 1e-6
GN_EPS = 64e-5
RW_HEAD = 64
ATT_HEADS = 16
ATT_HEAD_DIM = 64
KV_HEADS = 4
GROUPS = ATT_HEADS // KV_HEADS
IDX_HEADS = 8
IDX_DIM = 64
TOPK_MAX = 256
ROPE_THETA = 500000.0
ROT_FRAC = 4
PAGE = 128

LANES = 128
MXU_DIM = 256
VMEM_LIMIT = 56 * 1024 * 1024

LOG2E = 1.4426950408889634
INT_MIN = -(2 ** 31)
NEG_INF_KEY = INT_MIN + 0x7FFFFF
MASK_BIAS = -1e30


def _cparams(*sem):
    return pltpu.CompilerParams(dimension_semantics=sem, vmem_limit_bytes=VMEM_LIMIT)


def _const_spec(shape):
    n = len(shape)
    return pl.BlockSpec(shape, lambda *_: (0,) * n, pipeline_mode=pl.Buffered(1))


def _bdot(a, b):
    return jnp.dot(a.astype(BF16), b.astype(BF16), preferred_element_type=F32)


def _bdot_nt(a, b):
    return lax.dot_general(a.astype(BF16), b.astype(BF16), (((1,), (1,)), ((), ())),
                           preferred_element_type=F32)


def _rms_mod(x, g, sc, sh):
    ms = jnp.mean(x * x, axis=-1, keepdims=True)
    return x * lax.rsqrt(ms + NORM_EPS) * g * (1.0 + sc) + sh


def _seg_sum64(x, ones):
    hi = x.astype(BF16)
    lo = (x - hi.astype(F32)).astype(BF16)
    w = ones.shape[0]
    outs = []
    for c in range(x.shape[-1] // w):
        sl = slice(c * w, (c + 1) * w)
        outs.append(jnp.dot(hi[:, sl], ones, preferred_element_type=F32)
                    + jnp.dot(lo[:, sl], ones, preferred_element_type=F32))
    return outs[0] if len(outs) == 1 else jnp.concatenate(outs, axis=-1)


def _block_ones(width):
    i = np.arange(width) // 64
    return jnp.asarray((i[:, None] == i[None, :]).astype(np.float32), dtype=BF16)


def _ada_kernel(c_ref, w_ref, b_ref, o_ref):
    c = c_ref[...]
    s = c * jax.nn.sigmoid(c)
    o_ref[0] = _bdot(s, w_ref[0]) + b_ref[0]


def _ada(c_all, w_ada, b_ada):
    depth, d, d6 = w_ada.shape
    nb = c_all.shape[0]
    tn = 1536
    return pl.pallas_call(
        _ada_kernel,
        grid=(depth, d6 // tn),
        in_specs=[pl.BlockSpec((nb, d), lambda i, j: (0, 0)),
                  pl.BlockSpec((1, d, tn), lambda i, j: (i, 0, j)),
                  pl.BlockSpec((1, 1, tn), lambda i, j: (i, 0, j))],
        out_specs=pl.BlockSpec((1, nb, tn), lambda i, j: (i, 0, j)),
        out_shape=jax.ShapeDtypeStruct((depth, nb, d6), F32),
        compiler_params=_cparams("parallel", "parallel"),
        name="ada",
    )(c_all, w_ada, b_ada.reshape(depth, 1, d6))


class _Group:
    def __init__(self, n_seq, seq_len, tile):
        self.n_seq, self.seq_len = n_seq, seq_len
        self.rows = n_seq * seq_len
        self.tile = min(tile, self.rows)
        assert self.rows % self.tile == 0
        assert seq_len % self.tile == 0 or self.tile % seq_len == 0
        self.tiles_per_seq = max(seq_len // self.tile, 1)
        self.per_seq = seq_len >= self.tile
        self.n_tiles = self.rows // self.tile

    def seq_rows(self, m):
        d = m.shape[-1]
        if self.per_seq:
            return m.reshape(self.n_seq, 1, d)
        return jnp.repeat(m, self.seq_len, axis=0).reshape(self.n_tiles, self.tile, d)

    def seq_spec(self, d):
        r = 1 if self.per_seq else self.tile
        tps = self.tiles_per_seq
        return pl.BlockSpec((1, r, d), lambda i, *_: (i // tps, 0, 0))

    def pos_rows(self, tab):
        if self.per_seq:
            return tab
        return jnp.tile(tab, (self.tile // self.seq_len, 1))

    def pos_spec(self, w):
        tps = self.tiles_per_seq
        return pl.BlockSpec((self.tile, w), lambda i, *_: (i % tps, 0))

    def row_spec(self, w):
        return pl.BlockSpec((self.tile, w), lambda i, *_: (i, 0))


def _ffn_kernel(x_ref, g_ref, sc_ref, sh_ref, gate_ref, wu_ref, wd_ref, o_ref, h_sc, acc_sc):
    j = pl.program_id(1)

    @pl.when(j == 0)
    def _():
        h_sc[...] = _rms_mod(x_ref[...], g_ref[...], sc_ref[0], sh_ref[0]).astype(BF16)
        acc_sc[...] = jnp.zeros_like(acc_sc)

    u = jnp.maximum(jnp.dot(h_sc[...], wu_ref[...], preferred_element_type=F32), 0.0)
    acc_sc[...] += jnp.dot((u * u).astype(BF16), wd_ref[...], preferred_element_type=F32)

    @pl.when(j == pl.num_programs(1) - 1)
    def _():
        o_ref[...] = x_ref[...] + gate_ref[0] * acc_sc[...]


def _ffn(grp, x, g, sc, sh, gate, w_up, w_down):
    d, dff = w_up.shape
    tf = 512
    tm = grp.tile
    return pl.pallas_call(
        _ffn_kernel,
        grid=(grp.n_tiles, dff // tf),
        in_specs=[grp.row_spec(d), pl.BlockSpec((1, d), lambda i, j: (0, 0)),
                  grp.seq_spec(d), grp.seq_spec(d), grp.seq_spec(d),
                  pl.BlockSpec((d, tf), lambda i, j: (0, j)),
                  pl.BlockSpec((tf, d), lambda i, j: (j, 0))],
        out_specs=grp.row_spec(d),
        out_shape=jax.ShapeDtypeStruct(x.shape, F32),
        scratch_shapes=[pltpu.VMEM((tm, d), BF16), pltpu.VMEM((tm, d), F32)],
        compiler_params=_cparams("parallel", "arbitrary"),
        name="ffn",
    )(x, g, grp.seq_rows(sc), grp.seq_rows(sh), grp.seq_rows(gate), w_up, w_down)


def _rwkv_proj_kernel(has_vfirst, t_minor, seq_len, *refs):
    (x_ref, g_ref, sc_ref, sh_ref, shift_ref, mix_ref, wr_ref, wk_ref, wv_ref,
     w1_ref, w2_ref, a1_ref, a2_ref, g1_ref, g2_ref, vec_ref, ones_ref) = refs[:17]
    refs = refs[17:]
    if has_vfirst:
        v1_ref, v2_ref, vf_ref = refs[:3]
        refs = refs[3:]
    scan_o = refs[:6] if t_minor else refs[:5]
    v_o, g_o, bonus_o, h_o, carry_sc = refs[len(scan_o):]
    i = pl.program_id(0)
    tt = x_ref.shape[0]

    @pl.when(i == 0)
    def _():
        carry_sc[...] = jnp.zeros_like(carry_sc)

    h = _rms_mod(x_ref[...], g_ref[...], sc_ref[0], sh_ref[0])
    h_o[...] = h
    row = lax.broadcasted_iota(I32, h.shape, 0)
    hp = pltpu.roll(h, 1, axis=0)
    hp = jnp.where(row == 0, carry_sc[...], hp)
    hp = jnp.where(((row + i * tt) & (seq_len - 1)) == 0, shift_ref[0], hp)
    carry_sc[...] = h[tt - 1:tt, :]
    dx = hp - h
    mix = mix_ref[...]
    xr, xw, xk, xv, xa, xg = [h + dx * mix[n:n + 1, :] for n in range(6)]
    vec = vec_ref[...]
    w0, a0, k_k, k_a, v0, r_k = [vec[n:n + 1, :] for n in range(6)]
    ones = ones_ref[...]

    r = jnp.dot(xr.astype(BF16), wr_ref[...], preferred_element_type=F32)
    k = jnp.dot(xk.astype(BF16), wk_ref[...], preferred_element_type=F32)
    v = jnp.dot(xv.astype(BF16), wv_ref[...], preferred_element_type=F32)
    w_pre = w0 + _bdot(jnp.tanh(_bdot(xw, w1_ref[...])), w2_ref[...])
    decay = jnp.exp(-float(np.exp(-0.5)) * jax.nn.sigmoid(w_pre))
    if has_vfirst:
        vgate = jax.nn.sigmoid(v0 + _bdot(_bdot(xv, v1_ref[...]), v2_ref[...]))
        v = v + (vf_ref[...] - v) * vgate
    a = jax.nn.sigmoid(a0 + _bdot(_bdot(xa, a1_ref[...]), a2_ref[...]))
    g_o[...] = _bdot(jax.nn.sigmoid(_bdot(xg, g1_ref[...])), g2_ref[...])
    kk = k * k_k
    kk = kk * lax.rsqrt(jnp.maximum(_seg_sum64(kk * kk, ones), 1e-24))
    k = k * (1.0 + (a - 1.0) * k_a)
    v_o[...] = v
    bonus_o[...] = _seg_sum64(r * k * r_k, ones) * v
    scan = (r, decay, k, -kk, kk * a)
    if t_minor:
        for o_ref, t in zip(scan_o, scan + (v,)):
            o_ref[0] = t.T
    else:
        for o_ref, t in zip(scan_o, scan):
            o_ref[...] = t


def _pad_to(w, axis, mult):
    n = w.shape[axis]
    pad = (-n) % mult
    if pad == 0:
        return w
    cfg = [(0, 0)] * w.ndim
    cfg[axis] = (0, pad)
    return jnp.pad(w, cfg)


def _lora_pair(w_in, w_out):
    return (_pad_to(w_in, 1, LANES).astype(BF16), _pad_to(w_out, 0, LANES).astype(BF16))


def _rwkv_proj(grp, x, g, sc, sh, shift, P, j, v_first, t_minor):
    d = x.shape[-1]
    assert grp.seq_len & (grp.seq_len - 1) == 0, "sequence-start test uses a bit mask"
    has_vfirst = v_first is not None
    w1, w2 = _lora_pair(P['rw_w1'][j], P['rw_w2'][j])
    a1, a2 = _lora_pair(P['rw_a1'][j], P['rw_a2'][j])
    g1, g2 = _lora_pair(P['rw_g1'][j], P['rw_g2'][j])
    v0 = P['rw_v0'][j - 1] if has_vfirst else jnp.zeros((d,), F32)
    vec = jnp.stack([P['rw_w0'][j], P['rw_a0'][j], P['rw_k_k'][j], P['rw_k_a'][j], v0,
                     P['rw_r_k'][j].reshape(d), jnp.zeros((d,), F32), jnp.zeros((d,), F32)])
    mix = _pad_to(P['rw_mix'][j], 0, 8)
    wrkv = P['rw_w_rkv'][j].astype(BF16)
    ones = _block_ones(MXU_DIM)
    args = [x, g, grp.seq_rows(sc), grp.seq_rows(sh), grp.seq_rows(shift), mix,
            wrkv[0], wrkv[1], wrkv[2], w1, w2, a1, a2, g1, g2, vec, ones]
    specs = [grp.row_spec(d), _const_spec((1, d)), grp.seq_spec(d), grp.seq_spec(d), grp.seq_spec(d),
             _const_spec(mix.shape)] + [_const_spec(a.shape) for a in args[6:]]
    if has_vfirst:
        v1, v2 = _lora_pair(P['rw_v1'][j - 1], P['rw_v2'][j - 1])
        args += [v1, v2, v_first]
        specs += [_const_spec(v1.shape), _const_spec(v2.shape), grp.row_spec(d)]
    out = jax.ShapeDtypeStruct(x.shape, F32)
    if t_minor:
        assert grp.per_seq and grp.tile % LANES == 0
        tps = grp.tiles_per_seq
        scan_specs = [pl.BlockSpec((1, d, grp.tile), lambda i: (i // tps, 0, i % tps))] * 6
        scan_shapes = [jax.ShapeDtypeStruct((grp.n_seq, d, grp.seq_len), F32)] * 6
    else:
        scan_specs = [grp.row_spec(d)] * 5
        scan_shapes = [out] * 5
    return pl.pallas_call(
        functools.partial(_rwkv_proj_kernel, has_vfirst, t_minor, grp.seq_len),
        grid=(grp.n_tiles,),
        in_specs=specs,
        out_specs=scan_specs + [grp.row_spec(d)] * 4,
        out_shape=scan_shapes + [out] * 4,
        scratch_shapes=[pltpu.VMEM((1, d), F32)],
        compiler_params=_cparams("arbitrary"),
        name="rwkv_proj",
    )(*args)


def _wkv_kernel(r_ref, w_ref, k_ref, a_ref, b_ref, v_ref, s0_ref, y_ref, s_ref):
    c = pl.program_id(1)
    tc, nk, _ = r_ref.shape

    @pl.when(c == 0)
    def _():
        s_ref[...] = s0_ref[...]

    def step(t, carry):
        vt = v_ref[t]
        parts = [None] * 4
        for kk in range(nk):
            term = s_ref[kk] * a_ref[t, pl.ds(kk, 1), :]
            parts[kk % 4] = term if parts[kk % 4] is None else parts[kk % 4] + term
        sa = (parts[0] + parts[1]) + (parts[2] + parts[3])
        parts = [None] * 4
        for kk in range(nk):
            s_new = (s_ref[kk] * w_ref[t, pl.ds(kk, 1), :] + sa * b_ref[t, pl.ds(kk, 1), :]
                     + vt * k_ref[t, pl.ds(kk, 1), :])
            s_ref[kk] = s_new
            term = s_new * r_ref[t, pl.ds(kk, 1), :]
            parts[kk % 4] = term if parts[kk % 4] is None else parts[kk % 4] + term
        y_ref[t] = (parts[0] + parts[1]) + (parts[2] + parts[3])
        return carry

    lax.fori_loop(0, tc, step, 0)


def _wkv_scan(r, w, k, a, b, v, s0, tc):
    t_len, nk, lanes = r.shape
    nv = v.shape[1]
    tc = min(tc, t_len)
    kspec = pl.BlockSpec((tc, nk, LANES), lambda l, c: (c, 0, l))
    vspec = pl.BlockSpec((tc, nv, LANES), lambda l, c: (c, 0, l))
    sspec = pl.BlockSpec((nk, nv, LANES), lambda l, c: (0, 0, l))
    return pl.pallas_call(
        _wkv_kernel,
        grid=(lanes // LANES, t_len // tc),
        in_specs=[kspec] * 5 + [vspec, sspec],
        out_specs=[vspec, sspec],
        out_shape=[jax.ShapeDtypeStruct(v.shape, F32), jax.ShapeDtypeStruct(s0.shape, F32)],
        compiler_params=_cparams("parallel", "arbitrary"),
        name="wkv_scan",
    )(r, w, k, a, b, v, s0)


def _to_klanes(x, n_seq, seq_len, dup):
    h = x.shape[-1] // RW_HEAD
    y = x.reshape(n_seq, seq_len, h, RW_HEAD).transpose(1, 3, 0, 2).reshape(seq_len, RW_HEAD, n_seq * h)
    return jnp.concatenate([y] * dup, axis=-1) if dup > 1 else y


def _to_vlanes(x, n_seq, seq_len, dup):
    h = x.shape[-1] // RW_HEAD
    y = x.reshape(n_seq, seq_len, h, dup, RW_HEAD // dup).transpose(1, 4, 3, 0, 2)
    return y.reshape(seq_len, RW_HEAD // dup, dup * n_seq * h)


def _from_vlanes(y, n_seq, seq_len, dup):
    h = y.shape[-1] // (dup * n_seq)
    y = y.reshape(seq_len, RW_HEAD // dup, dup, n_seq, h).transpose(3, 0, 4, 2, 1)
    return y.reshape(n_seq * seq_len, h * RW_HEAD)


def _state_to_lanes(s, dup):
    n_seq, h = s.shape[:2]
    y = s.reshape(n_seq, h, dup, RW_HEAD // dup, RW_HEAD).transpose(4, 3, 2, 0, 1)
    return y.reshape(RW_HEAD, RW_HEAD // dup, dup * n_seq * h)


def _state_from_lanes(s, n_seq, dup):
    h = s.shape[-1] // (dup * n_seq)
    y = s.reshape(RW_HEAD, RW_HEAD // dup, dup, n_seq, h).transpose(3, 4, 2, 1, 0)
    return y.reshape(n_seq, h, RW_HEAD, RW_HEAD)


def _wkv(grp, r, w, k, v, a, b, s0):
    n_seq, seq_len = grp.n_seq, grp.seq_len
    heads = r.shape[-1] // RW_HEAD
    dup = max(1, LANES // (n_seq * heads))
    kl = [_to_klanes(t, n_seq, seq_len, dup) for t in (r, w, k, a, b)]
    y, s = _wkv_scan(*kl, _to_vlanes(v, n_seq, seq_len, dup), _state_to_lanes(s0, dup), tc=64)
    return _from_vlanes(y, n_seq, seq_len, dup), _state_from_lanes(s, n_seq, dup)


def _wkv_t_kernel(r_ref, w_ref, k_ref, a_ref, b_ref, v_ref, s0_ref, y_ref, s_ref, ks_sc, vs_sc, ys_sc):
    c = pl.program_id(0)
    n_seq, d, tc = r_ref.shape
    nk = RW_HEAD
    heads = d // nk
    dup = LANES // (n_seq * heads)
    nv = nk // dup

    @pl.when(c == 0)
    def _():
        s_ref[...] = s0_ref[...]

    def head_rows(ref, ch):
        return [ref[b, pl.ds(ch, heads, stride=nk), :] for b in range(n_seq)]

    def load_k(kk, carry):
        for n, ref in enumerate((r_ref, w_ref, k_ref, a_ref, b_ref)):
            ks_sc[n, kk] = jnp.concatenate(head_rows(ref, kk) * dup, axis=0).T
        return carry

    lax.fori_loop(0, nk, load_k, 0, unroll=2)

    def load_v(vr, carry):
        rows = []
        for part in range(dup):
            rows += head_rows(v_ref, part * nv + vr)
        vs_sc[pl.ds(vr, tc, stride=nv), :] = jnp.concatenate(rows, axis=0).T
        return carry

    lax.fori_loop(0, nv, load_v, 0)

    def step(t, carry):
        trow = pl.ds(t, 1)
        vt = vs_sc[pl.ds(pl.multiple_of(t * nv, nv), nv), :]
        parts = [None] * 4
        for kk in range(nk):
            term = s_ref[kk] * ks_sc[3, kk, trow, :]
            parts[kk % 4] = term if parts[kk % 4] is None else parts[kk % 4] + term
        sa = (parts[0] + parts[1]) + (parts[2] + parts[3])
        parts = [None] * 4
        for kk in range(nk):
            s_new = (s_ref[kk] * ks_sc[1, kk, trow, :] + sa * ks_sc[4, kk, trow, :]
                     + vt * ks_sc[2, kk, trow, :])
            s_ref[kk] = s_new
            term = s_new * ks_sc[0, kk, trow, :]
            parts[kk % 4] = term if parts[kk % 4] is None else parts[kk % 4] + term
        ys_sc[pl.ds(pl.multiple_of(t * nv, nv), nv), :] = (parts[0] + parts[1]) + (parts[2] + parts[3])
        return carry

    lax.fori_loop(0, tc, step, 0)

    def store_y(vr, carry):
        yt = ys_sc[pl.ds(vr, tc, stride=nv), :].T
        for part in range(dup):
            for b in range(n_seq):
                r0 = (part * n_seq + b) * heads
                y_ref[b, pl.ds(part * nv + vr, heads, stride=nk), :] = yt[r0:r0 + heads]
        return carry

    lax.fori_loop(0, nv, store_y, 0)


WKV_CHUNK = 128


def _wkv_t(r, w, k, a, b, v, s0):
    n_seq, d, t_len = r.shape
    heads = d // RW_HEAD
    assert LANES % (n_seq * heads) == 0 and t_len % WKV_CHUNK == 0
    dup = LANES // (n_seq * heads)
    nv = RW_HEAD // dup
    tc = WKV_CHUNK
    one = pl.Buffered(1)
    xspec = pl.BlockSpec((n_seq, d, tc), lambda c: (0, 0, c), pipeline_mode=one)
    sspec = pl.BlockSpec((RW_HEAD, nv, LANES), lambda c: (0, 0, 0), pipeline_mode=one)
    y, s = pl.pallas_call(
        _wkv_t_kernel,
        grid=(t_len // tc,),
        in_specs=[xspec] * 6 + [sspec],
        out_specs=[pl.BlockSpec((n_seq, d, tc), lambda c: (0, 0, c)),
                   pl.BlockSpec((RW_HEAD, nv, LANES), lambda c: (0, 0, 0))],
        out_shape=[jax.ShapeDtypeStruct(r.shape, F32), jax.ShapeDtypeStruct((RW_HEAD, nv, LANES), F32)],
        scratch_shapes=[pltpu.VMEM((5, RW_HEAD, tc, LANES), F32), pltpu.VMEM((tc * nv, LANES), F32),
                        pltpu.VMEM((tc * nv, LANES), F32)],
        compiler_params=_cparams("arbitrary"),
        name="wkv_scan_t",
    )(r, w, k, a, b, v, _state_to_lanes(s0, dup))
    return y, _state_from_lanes(s, n_seq, dup)


def _rwkv_post_kernel(t_minor, x_ref, y_ref, bonus_ref, g_ref, gate_ref, vec_ref, ones_ref, wo_ref, o_ref):
    vec = vec_ref[...]
    lnx_w, lnx_b = vec[0:1, :], vec[1:2, :]
    y = y_ref[0].T if t_minor else y_ref[...]
    ones = ones_ref[...]
    inv = 1.0 / RW_HEAD
    mu = _seg_sum64(y, ones) * inv
    yc = y - mu
    var = _seg_sum64(yc * yc, ones) * inv
    yn = yc * lax.rsqrt(var + GN_EPS) * lnx_w + lnx_b
    out = jnp.dot(((yn + bonus_ref[...]) * g_ref[...]).astype(BF16), wo_ref[...], preferred_element_type=F32)
    o_ref[...] = x_ref[...] + gate_ref[0] * out


def _rwkv_post(grp, x, y, bonus, g, gate, P, j, t_minor):
    d = x.shape[-1]
    vec = _pad_to(jnp.stack([P['rw_lnx_w'][j], P['rw_lnx_b'][j]]), 0, 8)
    ones = _block_ones(MXU_DIM)
    wo = P['rw_w_o'][j].astype(BF16)
    tps = grp.tiles_per_seq
    yspec = (pl.BlockSpec((1, d, grp.tile), lambda i: (i // tps, 0, i % tps)) if t_minor
             else grp.row_spec(d))
    return pl.pallas_call(
        functools.partial(_rwkv_post_kernel, t_minor),
        grid=(grp.n_tiles,),
        in_specs=[grp.row_spec(d), yspec, grp.row_spec(d), grp.row_spec(d), grp.seq_spec(d),
                  _const_spec(vec.shape), _const_spec(ones.shape), _const_spec(wo.shape)],
        out_specs=grp.row_spec(d),
        out_shape=jax.ShapeDtypeStruct(x.shape, F32),
        compiler_params=_cparams("parallel"),
        name="rwkv_post",
    )(x, y, bonus, g, grp.seq_rows(gate), vec, ones, wo)


Q_W = ATT_HEADS * ATT_HEAD_DIM
KV_W = KV_HEADS * ATT_HEAD_DIM
IQ_W = IDX_HEADS * IDX_DIM
OFF_K = Q_W
OFF_V = Q_W + KV_W
OFF_QI = Q_W + 2 * KV_W
OFF_KI = OFF_QI + IQ_W


def _rope_tables(pos):
    rd = ATT_HEAD_DIM // ROT_FRAC
    half = rd // 2
    inv = ROPE_THETA ** (-jnp.arange(half, dtype=F32) * 2.0 / rd)
    ang = pos.astype(F32)[:, None] * inv[None, :]
    cos, sin = jnp.cos(ang), jnp.sin(ang)
    n = pos.shape[0]
    rest = ATT_HEAD_DIM - rd
    c = jnp.concatenate([cos, cos, jnp.ones((n, rest), F32)], axis=-1)
    s_up = jnp.concatenate([-sin, jnp.zeros((n, half + rest), F32)], axis=-1)
    s_dn = jnp.concatenate([jnp.zeros((n, half), F32), sin, jnp.zeros((n, rest), F32)], axis=-1)
    return [jnp.concatenate([t, t], axis=-1) for t in (c, s_up, s_dn)]


def _rope128(x, c, s_up, s_dn):
    half = ATT_HEAD_DIM // ROT_FRAC // 2
    return x * c + pltpu.roll(x, LANES - half, axis=1) * s_up + pltpu.roll(x, half, axis=1) * s_dn


def _dsa_proj_kernel(x_ref, g_ref, sc_ref, sh_ref, w_ref, c_ref, su_ref, sd_ref, qg_ref, kg_ref, ig_ref,
                     ones_ref, q_o, k_o, v_o, qi_o, kiwi_o):
    h = _rms_mod(x_ref[...], g_ref[...], sc_ref[0], sh_ref[0])
    z = jnp.dot(h.astype(BF16), w_ref[...], preferred_element_type=F32)
    c, s_up, s_dn = c_ref[...], su_ref[...], sd_ref[...]
    ones = ones_ref[...]
    inv = 1.0 / ATT_HEAD_DIM
    lane = lax.broadcasted_iota(I32, (z.shape[0], LANES), 1)
    low = lane < ATT_HEAD_DIM

    def head_norm(t, gain, blk=ones):
        return t * lax.rsqrt(_seg_sum64(t * t, blk) * inv + NORM_EPS) * gain

    def pairs(t):
        return [t[:, n * LANES:(n + 1) * LANES] for n in range(t.shape[-1] // LANES)]

    q = head_norm(z[:, :Q_W], qg_ref[...]) * (LOG2E * ATT_HEAD_DIM ** -0.5)
    for n, t in enumerate(pairs(q)):
        t = _rope128(t, c, s_up, s_dn).astype(BF16)
        q_o[0, 2 * n] = t[:, :ATT_HEAD_DIM]
        q_o[0, 2 * n + 1] = t[:, ATT_HEAD_DIM:]
    k = head_norm(z[:, OFF_K:OFF_V], kg_ref[...])
    k_o[...] = jnp.concatenate([_rope128(t, c, s_up, s_dn) for t in pairs(k)], axis=-1)
    v_o[...] = z[:, OFF_V:OFF_QI]
    qi = z[:, OFF_QI:OFF_KI] * (IDX_DIM ** -0.5)
    for n, t in enumerate(pairs(qi)):
        t = _rope128(t, c, s_up, s_dn)
        qi_o[0, 2 * n] = jnp.where(low, t, 0.0).astype(BF16)
        qi_o[0, 2 * n + 1] = jnp.where(low, pltpu.roll(t, ATT_HEAD_DIM, axis=1), 0.0).astype(BF16)
    kw = z[:, OFF_KI:OFF_KI + LANES]
    ki = _rope128(head_norm(kw, ig_ref[...], ones[:LANES, :LANES]), c, s_up, s_dn)
    kiwi_o[...] = jnp.where(low, ki, kw * (IDX_HEADS ** -0.5))


def _dsa_proj(grp, x, g, sc, sh, pos, P, j):
    d = x.shape[-1]
    w = _pad_to(P['att_w_in'][j], 1, LANES).astype(BF16)
    tabs = [grp.pos_rows(t) for t in _rope_tables(pos)]
    qg = jnp.tile(P['att_q_norm'][j], ATT_HEADS).reshape(1, Q_W)
    kg = jnp.tile(P['att_k_norm'][j], KV_HEADS).reshape(1, KV_W)
    ig = _pad_to(P['idx_k_norm'][j], 0, LANES).reshape(1, LANES)
    ones = _block_ones(MXU_DIM)
    nsb = grp.n_seq if grp.per_seq else grp.n_tiles
    rows = grp.seq_len if grp.per_seq else grp.tile
    tps = grp.tiles_per_seq
    tt = grp.tile

    def head_spec(nh, wd):
        return pl.BlockSpec((1, nh, tt, wd), lambda i: (i // tps, 0, i % tps, 0))

    return pl.pallas_call(
        _dsa_proj_kernel,
        grid=(grp.n_tiles,),
        in_specs=[grp.row_spec(d), _const_spec((1, d)), grp.seq_spec(d), grp.seq_spec(d), _const_spec(w.shape),
                  grp.pos_spec(LANES), grp.pos_spec(LANES), grp.pos_spec(LANES),
                  _const_spec(qg.shape), _const_spec(kg.shape), _const_spec(ig.shape), _const_spec(ones.shape)],
        out_specs=[head_spec(ATT_HEADS, ATT_HEAD_DIM), grp.row_spec(KV_W), grp.row_spec(KV_W),
                   head_spec(IDX_HEADS, LANES), grp.row_spec(LANES)],
        out_shape=[jax.ShapeDtypeStruct((nsb, ATT_HEADS, rows, ATT_HEAD_DIM), BF16),
                   jax.ShapeDtypeStruct((grp.rows, KV_W), F32),
                   jax.ShapeDtypeStruct((grp.rows, KV_W), F32),
                   jax.ShapeDtypeStruct((nsb, IDX_HEADS, rows, LANES), BF16),
                   jax.ShapeDtypeStruct((grp.rows, LANES), F32)],
        compiler_params=_cparams("parallel"),
        name="dsa_proj",
    )(x, g, grp.seq_rows(sc), grp.seq_rows(sh), w, *tabs, qg, kg, ig, ones)


def _score_key(score, admissible):
    bits = pltpu.bitcast(score + 0.0, I32)
    key = jnp.where(bits >= 0, bits, bits ^ 0x7FFFFFFF)
    return jnp.where(admissible, key, NEG_INF_KEY)


def _topk_threshold(count, shape, topk, idx_bits):
    kf = float(topk)
    theta = jnp.where(count(lambda k, i: k >= 0) >= kf, jnp.full(shape, 0, I32), jnp.full(shape, INT_MIN, I32))

    def value_bit(n, theta):
        cand = theta + jnp.left_shift(jnp.int32(1), 30 - n)
        return jnp.where(count(lambda k, i: k >= cand) >= kf, cand, theta)

    theta = lax.fori_loop(0, 31, value_bit, theta)
    need = kf - count(lambda k, i: k > theta)

    def index_bit(n, cut):
        cand = cut + jnp.left_shift(jnp.int32(1), idx_bits - 1 - n)
        below = count(lambda k, i: jnp.where(k == theta, i, cand) < cand)
        return jnp.where(below < need, cand, cut)

    surplus = count(lambda k, i: k == theta) - need
    tied = jnp.where(theta > NEG_INF_KEY, surplus, 0.0)
    cut = lax.cond(jnp.max(tied) > 0.0,
                   lambda: lax.fori_loop(0, idx_bits, index_bit, jnp.zeros(shape, I32)),
                   lambda: jnp.full(shape, 2 ** 30, I32))
    return theta, cut


def _select_bias(key, idx, theta, cut):
    tie = jnp.where(idx <= cut, 0.0, MASK_BIAS)
    bias = jnp.where(key > theta, 0.0, jnp.where(key == theta, tie, MASK_BIAS))
    return jnp.where(key > NEG_INF_KEY, bias, MASK_BIAS)


ATT_ROW_BLOCK = 32


def _dsa_prompt_kernel(topk, q_ref, qi_ref, kiwiq_ref, x_ref, gate_ref, k_ref, v_ref, kiwi_ref, place_ref,
                       wo_ref, o_ref, key_sc, bias_sc, qblk_sc, kb_sc, vb_sc, s_sc, p_sc, m_sc, l_sc, alpha_sc,
                       acc_sc):
    qb = pl.program_id(1)
    tq = x_ref.shape[0]
    n_chunks, _, kc = key_sc.shape
    lt = kc // LANES
    nkc = (qb * tq + tq - 1) // kc + 1
    lane = lax.broadcasted_iota(I32, (tq, LANES), 1)
    row = lax.broadcasted_iota(I32, (tq, LANES), 0)
    qpos = qb * tq + row

    qi = qi_ref[0].reshape(IDX_HEADS * tq, LANES)
    wq = kiwiq_ref[...]
    wib = [jnp.broadcast_to(wq[:, IDX_DIM + h:IDX_DIM + h + 1], (tq, LANES)) for h in range(IDX_HEADS)]

    def score_chunk(c, carry):
        kic = kiwi_ref[pl.ds(pl.multiple_of(c * kc, kc), kc), :]
        logits = _bdot_nt(qi, kic)
        keys = []
        for j in range(lt):
            acc = None
            for h in range(IDX_HEADS):
                t = jnp.maximum(logits[h * tq:(h + 1) * tq, j * LANES:(j + 1) * LANES], 0.0) * wib[h]
                acc = t if acc is None else acc + t
            kpos = c * kc + j * LANES + lane
            keys.append(_score_key(acc, kpos <= qpos))
        key_sc[c] = jnp.concatenate(keys, axis=-1)
        return carry

    lax.fori_loop(0, nkc, score_chunk, 0)

    def count(pred):
        def body(c, acc):
            k = key_sc[c]
            for j in range(lt):
                m = pred(k[:, j * LANES:(j + 1) * LANES], c * kc + j * LANES + lane)
                acc = acc + jnp.where(m, 1.0, 0.0)
            return acc
        acc = lax.fori_loop(0, nkc, body, jnp.zeros((tq, LANES), F32))
        return jnp.sum(acc, axis=1, keepdims=True)

    theta, cut = _topk_threshold(count, (tq, LANES), topk, (n_chunks * kc - 1).bit_length())

    def bias_chunk(c, carry):
        k = key_sc[c]
        parts = [_select_bias(k[:, j * LANES:(j + 1) * LANES], c * kc + j * LANES + lane, theta, cut)
                 for j in range(lt)]
        bias_sc[c] = jnp.concatenate(parts, axis=-1)
        return carry

    lax.fori_loop(0, nkc, bias_chunk, 0)

    gq = GROUPS * tq
    for g in range(KV_HEADS):
        qg = q_ref[0, g * GROUPS:(g + 1) * GROUPS].reshape(gq, ATT_HEAD_DIM)
        qblk_sc[g] = jnp.dot(qg, place_ref[g], preferred_element_type=F32).astype(BF16)
    m_sc[...] = jnp.full_like(m_sc, MASK_BIAS)
    l_sc[...] = jnp.zeros_like(l_sc)
    acc_sc[...] = jnp.zeros_like(acc_sc)

    def att_chunk(c, carry):
        rows = pl.ds(pl.multiple_of(c * kc, kc), kc)
        kb_sc[...] = k_ref[rows, :].astype(BF16)
        vb_sc[...] = v_ref[rows, :].astype(BF16)

        for g in range(KV_HEADS):
            buf = g % 2
            base = g * gq
            s_sc[buf] = _bdot_nt(qblk_sc[g], kb_sc[...])
            for r0 in range(0, gq, ATT_ROW_BLOCK):
                t0 = r0 % tq
                rb = slice(r0, r0 + ATT_ROW_BLOCK)
                ms = slice(base + r0, base + r0 + ATT_ROW_BLOCK)
                mx = None
                for j in range(lt):
                    cols = slice(j * LANES, (j + 1) * LANES)
                    t = s_sc[buf, rb, cols] + bias_sc[c, t0:t0 + ATT_ROW_BLOCK, cols]
                    s_sc[buf, rb, cols] = t
                    mx = t if mx is None else jnp.maximum(mx, t)
                m_old = m_sc[ms, :]
                m_new = jnp.maximum(m_old, jnp.max(mx, axis=1, keepdims=True))
                alpha_sc[buf, rb, :] = jnp.exp2(m_old - m_new)
                m_sc[ms, :] = m_new
            for r0 in range(0, gq, ATT_ROW_BLOCK):
                rb = slice(r0, r0 + ATT_ROW_BLOCK)
                ms = slice(base + r0, base + r0 + ATT_ROW_BLOCK)
                m_new = m_sc[ms, :]
                tot = None
                for j in range(lt):
                    cols = slice(j * LANES, (j + 1) * LANES)
                    p = jnp.exp2(s_sc[buf, rb, cols] - m_new)
                    p_sc[buf, rb, cols] = p.astype(BF16)
                    tot = p if tot is None else tot + p
                l_sc[ms, :] = alpha_sc[buf, rb, :] * l_sc[ms, :] + jnp.sum(tot, axis=1, keepdims=True)
            a = alpha_sc[buf]
            rs = slice(base, base + gq)
            acc_sc[rs, :] = (acc_sc[rs, :] * jnp.concatenate([a] * (KV_W // LANES), axis=-1)
                             + jnp.dot(p_sc[buf], vb_sc[...], preferred_element_type=F32))
        return carry

    lax.fori_loop(0, nkc, att_chunk, 0)
    out = None
    for head in range(ATT_HEADS):
        rs = slice(head * tq, (head + 1) * tq)
        inv_l = 1.0 / l_sc[rs, :]
        o = acc_sc[rs, :] * jnp.concatenate([inv_l] * (KV_W // LANES), axis=-1)
        t = jnp.dot(o.astype(BF16), wo_ref[head], preferred_element_type=F32)
        out = t if out is None else out + t
    o_ref[...] = x_ref[...] + gate_ref[0] * out


def _head_placement():
    p = np.zeros((KV_HEADS, ATT_HEAD_DIM, KV_W), np.float32)
    for g in range(KV_HEADS):
        p[g, np.arange(ATT_HEAD_DIM), g * ATT_HEAD_DIM + np.arange(ATT_HEAD_DIM)] = 1.0
    return jnp.asarray(p, dtype=BF16)


def _dsa_prompt(n_seq, seq_len, x, gate, q, k, v, qi, kiwi, w_o):
    d = x.shape[-1]
    tq, kc = 128, 512
    assert seq_len % kc == 0
    topk = min(TOPK_MAX, seq_len // 4)
    nq = seq_len // tq
    gq = GROUPS * tq
    wo_h = w_o.astype(BF16).reshape(KV_HEADS, GROUPS, 1, ATT_HEAD_DIM, d)
    sel = jnp.eye(KV_HEADS, dtype=BF16).reshape(KV_HEADS, 1, KV_HEADS, 1, 1)
    wo = (wo_h * sel).reshape(ATT_HEADS, KV_W, d)
    place = _head_placement()
    return pl.pallas_call(
        functools.partial(_dsa_prompt_kernel, topk),
        grid=(n_seq, nq),
        in_specs=[pl.BlockSpec((1, ATT_HEADS, tq, ATT_HEAD_DIM), lambda b, i: (b, 0, i, 0)),
                  pl.BlockSpec((1, IDX_HEADS, tq, LANES), lambda b, i: (b, 0, i, 0)),
                  pl.BlockSpec((tq, LANES), lambda b, i: (b * nq + i, 0)),
                  pl.BlockSpec((tq, d), lambda b, i: (b * nq + i, 0)),
                  pl.BlockSpec((1, 1, d), lambda b, i: (b, 0, 0)),
                  pl.BlockSpec((seq_len, KV_W), lambda b, i: (b, 0)),
                  pl.BlockSpec((seq_len, KV_W), lambda b, i: (b, 0)),
                  pl.BlockSpec((seq_len, LANES), lambda b, i: (b, 0)),
                  _const_spec(place.shape), _const_spec(wo.shape)],
        out_specs=pl.BlockSpec((tq, d), lambda b, i: (b * nq + i, 0)),
        out_shape=jax.ShapeDtypeStruct(x.shape, F32),
        scratch_shapes=[pltpu.VMEM((seq_len // kc, tq, kc), I32), pltpu.VMEM((seq_len // kc, tq, kc), F32),
                        pltpu.VMEM((KV_HEADS, gq, KV_W), BF16), pltpu.VMEM((kc, KV_W), BF16),
                        pltpu.VMEM((kc, KV_W), BF16), pltpu.VMEM((2, gq, kc), F32),
                        pltpu.VMEM((2, gq, kc), BF16), pltpu.VMEM((ATT_HEADS * tq, LANES), F32),
                        pltpu.VMEM((ATT_HEADS * tq, LANES), F32), pltpu.VMEM((2, gq, LANES), F32),
                        pltpu.VMEM((ATT_HEADS * tq, KV_W), F32)],
        compiler_params=_cparams("parallel", "arbitrary"),
        name="dsa_prompt",
    )(q, qi, kiwi, x, gate.reshape(n_seq, 1, d), k, v, kiwi, place, wo)


PAGES_PER_STEP = 8


def _dsa_sel_kernel(topk, pt_ref, qi_ref, wib_ref, *refs):
    page_refs = refs[:PAGES_PER_STEP]
    kinew_ref, bias_ref, key_sc = refs[PAGES_PER_STEP:]
    p = pl.program_id(1)
    nsteps = pl.num_programs(1)
    npg = key_sc.shape[0] - 1
    t_new = bias_ref.shape[2]
    qi = qi_ref[0]
    wib = wib_ref[0]

    def score(keys):
        t = jnp.maximum(_bdot_nt(qi, keys), 0.0) * wib
        acc = t[0:t_new]
        for h in range(1, IDX_HEADS):
            acc = acc + t[h * t_new:(h + 1) * t_new]
        return acc

    for n, page_ref in enumerate(page_refs):
        key_sc[p * PAGES_PER_STEP + n] = _score_key(score(page_ref[0]), True)

    @pl.when(p == nsteps - 1)
    def _():
        lane = lax.broadcasted_iota(I32, (t_new, LANES), 1)
        row = lax.broadcasted_iota(I32, (t_new, LANES), 0)
        key_sc[npg] = _score_key(score(kinew_ref[0]), lane <= row)
        keys = key_sc[...]
        idx = (lax.broadcasted_iota(I32, keys.shape, 0) * LANES + lax.broadcasted_iota(I32, keys.shape, 2))

        def count(pred):
            c = jnp.sum(jnp.where(pred(keys, idx), 1.0, 0.0), axis=0)
            return jnp.sum(c, axis=1, keepdims=True)

        theta, cut = _topk_threshold(count, (t_new, LANES), topk, (keys.shape[0] * LANES - 1).bit_length())
        bias_ref[0] = _select_bias(keys, idx, theta, cut)


def _page_spec(width, n, npg):
    return pl.BlockSpec((1, PAGE, width), lambda b, p, pt: (pt[b, p * PAGES_PER_STEP + n], 0, 0))


def _dsa_sel(page_table, qi, wib, cache_ki, kinew, t_new):
    n_seq, npg = page_table.shape
    assert npg % PAGES_PER_STEP == 0
    topk = min(TOPK_MAX, (npg * PAGE + t_new) // 4)
    rows = qi.shape[1]
    grid_spec = pltpu.PrefetchScalarGridSpec(
        num_scalar_prefetch=1,
        grid=(n_seq, npg // PAGES_PER_STEP),
        in_specs=[pl.BlockSpec((1, rows, IDX_DIM), lambda b, p, pt: (b, 0, 0)),
                  pl.BlockSpec((1, rows, LANES), lambda b, p, pt: (b, 0, 0))]
        + [_page_spec(IDX_DIM, n, npg) for n in range(PAGES_PER_STEP)]
        + [pl.BlockSpec((1, PAGE, IDX_DIM), lambda b, p, pt: (b, 0, 0))],
        out_specs=pl.BlockSpec((1, npg + 1, t_new, LANES), lambda b, p, pt: (b, 0, 0, 0)),
        scratch_shapes=[pltpu.VMEM((npg + 1, t_new, LANES), I32)])
    return pl.pallas_call(
        functools.partial(_dsa_sel_kernel, topk),
        grid_spec=grid_spec,
        out_shape=jax.ShapeDtypeStruct((n_seq, npg + 1, t_new, LANES), F32),
        compiler_params=_cparams("parallel", "arbitrary"),
        name="dsa_sel",
    )(page_table, qi, wib, *([cache_ki] * PAGES_PER_STEP), kinew)


def _dsa_att_kernel(pt_ref, q_ref, *refs):
    kpage_refs = refs[:PAGES_PER_STEP]
    vpage_refs = refs[PAGES_PER_STEP:2 * PAGES_PER_STEP]
    knew_ref, vnew_ref, bias_ref, biasnew_ref, o_ref, m_sc, l_sc, acc_sc = refs[2 * PAGES_PER_STEP:]
    p = pl.program_id(1)
    last = pl.num_programs(1) - 1
    rows = q_ref.shape[1]
    t_new = bias_ref.shape[2]

    @pl.when(p == 0)
    def _():
        m_sc[...] = jnp.full_like(m_sc, MASK_BIAS)
        l_sc[...] = jnp.zeros_like(l_sc)
        acc_sc[...] = jnp.zeros_like(acc_sc)

    def attend(kv_bias):
        q = q_ref[0]
        ss = []
        for kk, _, bias in kv_bias:
            s = _bdot_nt(q, kk)
            ss.append((s.reshape(rows // t_new, t_new, LANES) + bias[None]).reshape(rows, LANES))
        mx = ss[0]
        for s in ss[1:]:
            mx = jnp.maximum(mx, s)
        m_old = m_sc[...]
        m_new = jnp.maximum(m_old, jnp.max(mx, axis=1, keepdims=True))
        alpha = jnp.exp2(m_old - m_new)
        tot = None
        pv = None
        for s, (_, vv, _) in zip(ss, kv_bias):
            pr = jnp.exp2(s - m_new)
            tot = pr if tot is None else tot + pr
            t = _bdot(pr, vv)
            pv = t if pv is None else pv + t
        l_sc[...] = alpha * l_sc[...] + jnp.sum(tot, axis=1, keepdims=True)
        acc_sc[...] = alpha * acc_sc[...] + pv
        m_sc[...] = m_new

    attend([(kpage_refs[n][0], vpage_refs[n][0], bias_ref[0, n]) for n in range(PAGES_PER_STEP)])

    @pl.when(p == last)
    def _():
        attend([(knew_ref[0], vnew_ref[0], biasnew_ref[0, 0])])
        o_ref[0] = acc_sc[...] / l_sc[...]


def _dsa_att(page_table, qblk, cache_k, cache_v, knew, vnew, bias):
    n_seq, npg = page_table.shape
    rows = qblk.shape[1]
    t_new = bias.shape[2]
    grid_spec = pltpu.PrefetchScalarGridSpec(
        num_scalar_prefetch=1,
        grid=(n_seq, npg // PAGES_PER_STEP),
        in_specs=[pl.BlockSpec((1, rows, KV_W), lambda b, p, pt: (b, 0, 0))]
        + [_page_spec(KV_W, n, npg) for n in range(PAGES_PER_STEP)] * 2
        + [pl.BlockSpec((1, PAGE, KV_W), lambda b, p, pt: (b, 0, 0)),
           pl.BlockSpec((1, PAGE, KV_W), lambda b, p, pt: (b, 0, 0)),
           pl.BlockSpec((1, PAGES_PER_STEP, t_new, LANES), lambda b, p, pt: (b, p, 0, 0)),
           pl.BlockSpec((1, 1, t_new, LANES), lambda b, p, pt: (b, npg, 0, 0))],
        out_specs=pl.BlockSpec((1, rows, KV_W), lambda b, p, pt: (b, 0, 0)),
        scratch_shapes=[pltpu.VMEM((rows, 1), F32), pltpu.VMEM((rows, 1), F32), pltpu.VMEM((rows, KV_W), F32)])
    return pl.pallas_call(
        _dsa_att_kernel,
        grid_spec=grid_spec,
        out_shape=jax.ShapeDtypeStruct((n_seq, rows, KV_W), F32),
        compiler_params=_cparams("parallel", "arbitrary"),
        name="dsa_att",
    )(page_table, qblk, *([cache_k] * PAGES_PER_STEP), *([cache_v] * PAGES_PER_STEP), knew, vnew, bias, bias)


def _linear_res_kernel(x_ref, a_ref, gate_ref, w_ref, o_ref):
    o_ref[...] = x_ref[...] + gate_ref[0] * _bdot(a_ref[...], w_ref[...])


def _linear_res(grp, x, a, gate, w):
    d = x.shape[-1]
    wb = w.astype(BF16)
    return pl.pallas_call(
        _linear_res_kernel,
        grid=(grp.n_tiles,),
        in_specs=[grp.row_spec(d), grp.row_spec(a.shape[-1]), grp.seq_spec(d), _const_spec(wb.shape)],
        out_specs=grp.row_spec(d),
        out_shape=jax.ShapeDtypeStruct(x.shape, F32),
        compiler_params=_cparams("parallel"),
        name="linear_res",
    )(x, a, grp.seq_rows(gate), wb)


def _dsa_sample(grp, x, gate, q, k, v, qi, kiwi, cache_k, cache_v, cache_ki, page_table, w_o):
    n_seq, t_new = grp.n_seq, grp.seq_len
    assert t_new <= PAGE and grp.n_tiles == 1
    pool = cache_k.shape[0]

    def pad_rows(t):
        return _pad_to(t.reshape(n_seq, t_new, t.shape[-1]), 1, PAGE)

    qi_b = qi[0, :, :, :IDX_DIM].reshape(IDX_HEADS, n_seq, t_new, IDX_DIM).transpose(1, 0, 2, 3)
    qi_b = qi_b.reshape(n_seq, IDX_HEADS * t_new, IDX_DIM)
    wi = kiwi[:, IDX_DIM:IDX_DIM + IDX_HEADS].reshape(n_seq, t_new, IDX_HEADS).transpose(0, 2, 1)
    wib = jnp.broadcast_to(wi.reshape(n_seq, IDX_HEADS * t_new, 1), (n_seq, IDX_HEADS * t_new, LANES))
    bias = _dsa_sel(page_table, qi_b, wib, cache_ki, pad_rows(kiwi[:, :IDX_DIM]), t_new)
    q_b = q[0].reshape(KV_HEADS, GROUPS, n_seq, t_new, ATT_HEAD_DIM).transpose(2, 0, 1, 3, 4)
    eye = jnp.eye(KV_HEADS, dtype=q_b.dtype)
    qblk = (q_b[:, :, :, :, None, :] * eye[None, :, None, None, :, None])
    qblk = qblk.reshape(n_seq, ATT_HEADS * t_new, KV_W)
    o = _dsa_att(page_table, qblk, cache_k.reshape(pool, PAGE, KV_W), cache_v.reshape(pool, PAGE, KV_W),
                 pad_rows(k), pad_rows(v), bias)
    o = o.reshape(n_seq, KV_HEADS, GROUPS, t_new, KV_HEADS, ATT_HEAD_DIM)
    o = jnp.stack([o[:, g, :, :, g, :] for g in range(KV_HEADS)], axis=1)
    o = o.transpose(0, 3, 1, 2, 4).reshape(n_seq * t_new, Q_W)
    return _linear_res(grp, x, o, gate, w_o)


TOKEN_TILE = 256
FFN_TILE = 1024


def _trunk(x, mods, pos, wkv0, shift0, attn_fn, P):
    n_seq, seq_len, d = x.shape
    depth = mods.shape[0]
    grp = _Group(n_seq, seq_len, TOKEN_TILE)
    grp_ffn = _Group(n_seq, seq_len, FFN_TILE)
    x = x.reshape(n_seq * seq_len, d)
    v_first = None
    ks, vs, kis, wkvs, shifts = [], [], [], [], []
    for i in range(depth):
        j = i // 2
        sh_a, sc_a, g_a, sh_f, sc_f, g_f = jnp.split(mods[i], 6, axis=-1)
        g_att = P['norm_g'][i, 0].reshape(1, d)
        g_ffn = P['norm_g'][i, 1].reshape(1, d)
        if i % 2 == 0:
            t_minor = grp.per_seq and seq_len % WKV_CHUNK == 0 and LANES % (n_seq * (d // RW_HEAD)) == 0
            outs = _rwkv_proj(grp, x, g_att, sc_a, sh_a, shift0[j], P, j, v_first, t_minor)
            v, gate, bonus, h = outs[-4:]
            if j == 0:
                v_first = v
            if t_minor:
                y, state = _wkv_t(*outs[:6], wkv0[j])
            else:
                r, w, k, a, b = outs[:5]
                y, state = _wkv(grp, r, w, k, v, a, b, wkv0[j])
            x = _rwkv_post(grp, x, y, bonus, gate, g_a, P, j, t_minor)
            wkvs.append(state)
            shifts.append(h.reshape(n_seq, seq_len, d)[:, -1])
        else:
            q, k, v, qi, kiwi = _dsa_proj(grp, x, g_att, sc_a, sh_a, pos, P, j)
            x = attn_fn(j, grp, x, g_a, q, k, v, qi, kiwi)
            ks.append(k.reshape(n_seq, seq_len, KV_HEADS, ATT_HEAD_DIM))
            vs.append(v.reshape(n_seq, seq_len, KV_HEADS, ATT_HEAD_DIM))
            kis.append(kiwi[:, :IDX_DIM].reshape(n_seq, seq_len, IDX_DIM))
        x = _ffn(grp_ffn, x, g_ffn, sc_f, sh_f, g_f, P['w_up'][i], P['w_down'][i])
    return (x.reshape(n_seq, seq_len, d), jnp.stack(ks), jnp.stack(vs), jnp.stack(kis),
            jnp.stack(wkvs), jnp.stack(shifts))


def kernel(x_prompt, x_sample, cache_k, cache_v, cache_idx_k, state_wkv, state_shift, page_table,
           c_prompt, c_sample, norm_g, w_ada, b_ada, w_up, w_down, rw_mix, rw_w_rkv, rw_w_o,
           rw_w0, rw_w1, rw_w2, rw_a0, rw_a1, rw_a2, rw_v0, rw_v1, rw_v2, rw_g1, rw_g2,
           rw_k_k, rw_k_a, rw_r_k, rw_lnx_w, rw_lnx_b, att_w_in, att_w_o, att_q_norm,
           att_k_norm, idx_k_norm):
    P = dict(norm_g=norm_g, w_up=w_up.astype(BF16), w_down=w_down.astype(BF16),
             rw_mix=rw_mix, rw_w_rkv=rw_w_rkv, rw_w_o=rw_w_o, rw_w0=rw_w0, rw_w1=rw_w1,
             rw_w2=rw_w2, rw_a0=rw_a0, rw_a1=rw_a1, rw_a2=rw_a2, rw_v0=rw_v0, rw_v1=rw_v1,
             rw_v2=rw_v2, rw_g1=rw_g1, rw_g2=rw_g2, rw_k_k=rw_k_k, rw_k_a=rw_k_a, rw_r_k=rw_r_k,
             rw_lnx_w=rw_lnx_w, rw_lnx_b=rw_lnx_b, att_w_in=att_w_in, att_w_o=att_w_o,
             att_q_norm=att_q_norm, att_k_norm=att_k_norm, idx_k_norm=idx_k_norm)
    n_p, seq, d = x_prompt.shape
    n_s, dec_seq, _ = x_sample.shape
    n_rwkv = state_wkv.shape[0]
    heads = d // RW_HEAD
    past_len = page_table.shape[1] * PAGE

    mods = _ada(jnp.concatenate([c_prompt, c_sample], axis=0), w_ada, b_ada)

    def prompt_attn(j, grp, x, gate, q, k, v, qi, kiwi):
        return _dsa_prompt(grp.n_seq, grp.seq_len, x, gate, q, k, v, qi, kiwi, att_w_o[j])

    def sample_attn(j, grp, x, gate, q, k, v, qi, kiwi):
        return _dsa_sample(grp, x, gate, q, k, v, qi, kiwi, cache_k[j], cache_v[j], cache_idx_k[j],
                           page_table, att_w_o[j])

    wkv0 = jnp.zeros((n_rwkv, n_p, heads, RW_HEAD, RW_HEAD), F32)
    shift0 = jnp.zeros((n_rwkv, n_p, d), F32)
    y_p, k_p, v_p, ki_p, wkv_p, shift_p = _trunk(
        x_prompt, mods[:, :n_p], jnp.arange(seq), wkv0, shift0, prompt_attn, P)
    y_s, k_s, v_s, ki_s, wkv_s, shift_s = _trunk(
        x_sample, mods[:, n_p:], past_len + jnp.arange(dec_seq), state_wkv, state_shift, sample_attn, P)
    return (y_p, y_s, k_p, v_p, ki_p, wkv_p, shift_p, k_s, v_s, ki_s, wkv_s, shift_s)
```

```python
import functools

import jax
import jax.numpy as jnp
import numpy as np
from jax import lax
from jax.experimental import pallas as pl
from jax.experimental.pallas import tpu as pltpu

F32 = jnp.float32
BF16 = jnp.bfloat16
I32 = jnp.int32

NORM_EPS = 1e-6
GN_EPS = 64e-5
RW_HEAD = 64
ATT_HEADS = 16
ATT_HEAD_DIM = 64
KV_HEADS = 4
GROUPS = ATT_HEADS // KV_HEADS
IDX_HEADS = 8
IDX_DIM = 64
TOPK_MAX = 256
ROPE_THETA = 500000.0
ROT_FRAC = 4
PAGE = 128

LANES = 128
MXU_DIM = 256
VMEM_LIMIT = 56 * 1024 * 1024

LOG2E = 1.4426950408889634
INT_MIN = -(2 ** 31)
NEG_INF_KEY = INT_MIN + 0x7FFFFF
MASK_BIAS = -1e30


def _cparams(*sem):
    return pltpu.CompilerParams(dimension_semantics=sem, vmem_limit_bytes=VMEM_LIMIT)


def _const_spec(shape):
    n = len(shape)
    return pl.BlockSpec(shape, lambda *_: (0,) * n, pipeline_mode=pl.Buffered(1))


def _bdot(a, b):
    return jnp.dot(a.astype(BF16), b.astype(BF16), preferred_element_type=F32)


def _bdot_nt(a, b):
    return lax.dot_general(a.astype(BF16), b.astype(BF16), (((1,), (1,)), ((), ())),
                           preferred_element_type=F32)


def _rms_mod(x, g, sc, sh):
    ms = jnp.mean(x * x, axis=-1, keepdims=True)
    return x * lax.rsqrt(ms + NORM_EPS) * g * (1.0 + sc) + sh


def _seg_sum64(x, ones):
    hi = x.astype(BF16)
    lo = (x - hi.astype(F32)).astype(BF16)
    w = ones.shape[0]
    outs = []
    for c in range(x.shape[-1] // w):
        sl = slice(c * w, (c + 1) * w)
        outs.append(jnp.dot(hi[:, sl], ones, preferred_element_type=F32)
                    + jnp.dot(lo[:, sl], ones, preferred_element_type=F32))
    return outs[0] if len(outs) == 1 else jnp.concatenate(outs, axis=-1)


def _block_ones(width):
    i = np.arange(width) // 64
    return jnp.asarray((i[:, None] == i[None, :]).astype(np.float32), dtype=BF16)


def _ada_kernel(c_ref, w_ref, b_ref, o_ref):
    c = c_ref[...]
    s = c * jax.nn.sigmoid(c)
    o_ref[0] = _bdot(s, w_ref[0]) + b_ref[0]


def _ada(c_all, w_ada, b_ada):
    depth, d, d6 = w_ada.shape
    nb = c_all.shape[0]
    tn = 1536
    return pl.pallas_call(
        _ada_kernel,
        grid=(depth, d6 // tn),
        in_specs=[pl.BlockSpec((nb, d), lambda i, j: (0, 0)),
                  pl.BlockSpec((1, d, tn), lambda i, j: (i, 0, j)),
                  pl.BlockSpec((1, 1, tn), lambda i, j: (i, 0, j))],
        out_specs=pl.BlockSpec((1, nb, tn), lambda i, j: (i, 0, j)),
        out_shape=jax.ShapeDtypeStruct((depth, nb, d6), F32),
        compiler_params=_cparams("parallel", "parallel"),
        name="ada",
    )(c_all, w_ada, b_ada.reshape(depth, 1, d6))


class _Group:
    def __init__(self, n_seq, seq_len, tile):
        self.n_seq, self.seq_len = n_seq, seq_len
        self.rows = n_seq * seq_len
        self.tile = min(tile, self.rows)
        assert self.rows % self.tile == 0
        assert seq_len % self.tile == 0 or self.tile % seq_len == 0
        self.tiles_per_seq = max(seq_len // self.tile, 1)
        self.per_seq = seq_len >= self.tile
        self.n_tiles = self.rows // self.tile

    def seq_rows(self, m):
        d = m.shape[-1]
        if self.per_seq:
            return m.reshape(self.n_seq, 1, d)
        return jnp.repeat(m, self.seq_len, axis=0).reshape(self.n_tiles, self.tile, d)

    def seq_spec(self, d):
        r = 1 if self.per_seq else self.tile
        tps = self.tiles_per_seq
        return pl.BlockSpec((1, r, d), lambda i, *_: (i // tps, 0, 0))

    def pos_rows(self, tab):
        if self.per_seq:
            return tab
        return jnp.tile(tab, (self.tile // self.seq_len, 1))

    def pos_spec(self, w):
        tps = self.tiles_per_seq
        return pl.BlockSpec((self.tile, w), lambda i, *_: (i % tps, 0))

    def row_spec(self, w):
        return pl.BlockSpec((self.tile, w), lambda i, *_: (i, 0))


def _ffn_kernel(x_ref, g_ref, sc_ref, sh_ref, gate_ref, wu_ref, wd_ref, o_ref, h_sc, acc_sc):
    j = pl.program_id(1)

    @pl.when(j == 0)
    def _():
        h_sc[...] = _rms_mod(x_ref[...], g_ref[...], sc_ref[0], sh_ref[0]).astype(BF16)
        acc_sc[...] = jnp.zeros_like(acc_sc)

    u = jnp.maximum(jnp.dot(h_sc[...], wu_ref[...], preferred_element_type=F32), 0.0)
    acc_sc[...] += jnp.dot((u * u).astype(BF16), wd_ref[...], preferred_element_type=F32)

    @pl.when(j == pl.num_programs(1) - 1)
    def _():
        o_ref[...] = x_ref[...] + gate_ref[0] * acc_sc[...]


def _ffn(grp, x, g, sc, sh, gate, w_up, w_down):
    d, dff = w_up.shape
    tf = 512
    tm = grp.tile
    return pl.pallas_call(
        _ffn_kernel,
        grid=(grp.n_tiles, dff // tf),
        in_specs=[grp.row_spec(d), pl.BlockSpec((1, d), lambda i, j: (0, 0)),
                  grp.seq_spec(d), grp.seq_spec(d), grp.seq_spec(d),
                  pl.BlockSpec((d, tf), lambda i, j: (0, j)),
                  pl.BlockSpec((tf, d), lambda i, j: (j, 0))],
        out_specs=grp.row_spec(d),
        out_shape=jax.ShapeDtypeStruct(x.shape, F32),
        scratch_shapes=[pltpu.VMEM((tm, d), BF16), pltpu.VMEM((tm, d), F32)],
        compiler_params=_cparams("parallel", "arbitrary"),
        name="ffn",
    )(x, g, grp.seq_rows(sc), grp.seq_rows(sh), grp.seq_rows(gate), w_up, w_down)


def _rwkv_proj_kernel(has_vfirst, t_minor, seq_len, *refs):
    (x_ref, g_ref, sc_ref, sh_ref, shift_ref, mix_ref, wr_ref, wk_ref, wv_ref,
     w1_ref, w2_ref, a1_ref, a2_ref, g1_ref, g2_ref, vec_ref, ones_ref) = refs[:17]
    refs = refs[17:]
    if has_vfirst:
        v1_ref, v2_ref, vf_ref = refs[:3]
        refs = refs[3:]
    scan_o = refs[:6] if t_minor else refs[:5]
    v_o, g_o, bonus_o, h_o, carry_sc = refs[len(scan_o):]
    i = pl.program_id(0)
    tt = x_ref.shape[0]

    @pl.when(i == 0)
    def _():
        carry_sc[...] = jnp.zeros_like(carry_sc)

    h = _rms_mod(x_ref[...], g_ref[...], sc_ref[0], sh_ref[0])
    h_o[...] = h
    row = lax.broadcasted_iota(I32, h.shape, 0)
    hp = pltpu.roll(h, 1, axis=0)
    hp = jnp.where(row == 0, carry_sc[...], hp)
    hp = jnp.where(((row + i * tt) & (seq_len - 1)) == 0, shift_ref[0], hp)
    carry_sc[...] = h[tt - 1:tt, :]
    dx = hp - h
    mix = mix_ref[...]
    xr, xw, xk, xv, xa, xg = [h + dx * mix[n:n + 1, :] for n in range(6)]
    vec = vec_ref[...]
    w0, a0, k_k, k_a, v0, r_k = [vec[n:n + 1, :] for n in range(6)]
    ones = ones_ref[...]

    r = jnp.dot(xr.astype(BF16), wr_ref[...], preferred_element_type=F32)
    k = jnp.dot(xk.astype(BF16), wk_ref[...], preferred_element_type=F32)
    v = jnp.dot(xv.astype(BF16), wv_ref[...], preferred_element_type=F32)
    w_pre = w0 + _bdot(jnp.tanh(_bdot(xw, w1_ref[...])), w2_ref[...])
    decay = jnp.exp(-float(np.exp(-0.5)) * jax.nn.sigmoid(w_pre))
    if has_vfirst:
        vgate = jax.nn.sigmoid(v0 + _bdot(_bdot(xv, v1_ref[...]), v2_ref[...]))
        v = v + (vf_ref[...] - v) * vgate
    a = jax.nn.sigmoid(a0 + _bdot(_bdot(xa, a1_ref[...]), a2_ref[...]))
    g_o[...] = _bdot(jax.nn.sigmoid(_bdot(xg, g1_ref[...])), g2_ref[...])
    kk = k * k_k
    kk = kk * lax.rsqrt(jnp.maximum(_seg_sum64(kk * kk, ones), 1e-24))
    k = k * (1.0 + (a - 1.0) * k_a)
    v_o[...] = v
    bonus_o[...] = _seg_sum64(r * k * r_k, ones) * v
    scan = (r, decay, k, -kk, kk * a)
    if t_minor:
        for o_ref, t in zip(scan_o, scan + (v,)):
            o_ref[0] = t.T
    else:
        for o_ref, t in zip(scan_o, scan):
            o_ref[...] = t


def _pad_to(w, axis, mult):
    n = w.shape[axis]
    pad = (-n) % mult
    if pad == 0:
        return w
    cfg = [(0, 0)] * w.ndim
    cfg[axis] = (0, pad)
    return jnp.pad(w, cfg)


def _lora_pair(w_in, w_out):
    return (_pad_to(w_in, 1, LANES).astype(BF16), _pad_to(w_out, 0, LANES).astype(BF16))


def _rwkv_proj(grp, x, g, sc, sh, shift, P, j, v_first, t_minor):
    d = x.shape[-1]
    assert grp.seq_len & (grp.seq_len - 1) == 0, "sequence-start test uses a bit mask"
    has_vfirst = v_first is not None
    w1, w2 = _lora_pair(P['rw_w1'][j], P['rw_w2'][j])
    a1, a2 = _lora_pair(P['rw_a1'][j], P['rw_a2'][j])
    g1, g2 = _lora_pair(P['rw_g1'][j], P['rw_g2'][j])
    v0 = P['rw_v0'][j - 1] if has_vfirst else jnp.zeros((d,), F32)
    vec = jnp.stack([P['rw_w0'][j], P['rw_a0'][j], P['rw_k_k'][j], P['rw_k_a'][j], v0,
                     P['rw_r_k'][j].reshape(d), jnp.zeros((d,), F32), jnp.zeros((d,), F32)])
    mix = _pad_to(P['rw_mix'][j], 0, 8)
    wrkv = P['rw_w_rkv'][j].astype(BF16)
    ones = _block_ones(MXU_DIM)
    args = [x, g, grp.seq_rows(sc), grp.seq_rows(sh), grp.seq_rows(shift), mix,
            wrkv[0], wrkv[1], wrkv[2], w1, w2, a1, a2, g1, g2, vec, ones]
    specs = [grp.row_spec(d), _const_spec((1, d)), grp.seq_spec(d), grp.seq_spec(d), grp.seq_spec(d),
             _const_spec(mix.shape)] + [_const_spec(a.shape) for a in args[6:]]
    if has_vfirst:
        v1, v2 = _lora_pair(P['rw_v1'][j - 1], P['rw_v2'][j - 1])
        args += [v1, v2, v_first]
        specs += [_const_spec(v1.shape), _const_spec(v2.shape), grp.row_spec(d)]
    out = jax.ShapeDtypeStruct(x.shape, F32)
    if t_minor:
        assert grp.per_seq and grp.tile % LANES == 0
        tps = grp.tiles_per_seq
        scan_specs = [pl.BlockSpec((1, d, grp.tile), lambda i: (i // tps, 0, i % tps))] * 6
        scan_shapes = [jax.ShapeDtypeStruct((grp.n_seq, d, grp.seq_len), F32)] * 6
    else:
        scan_specs = [grp.row_spec(d)] * 5
        scan_shapes = [out] * 5
    return pl.pallas_call(
        functools.partial(_rwkv_proj_kernel, has_vfirst, t_minor, grp.seq_len),
        grid=(grp.n_tiles,),
        in_specs=specs,
        out_specs=scan_specs + [grp.row_spec(d)] * 4,
        out_shape=scan_shapes + [out] * 4,
        scratch_shapes=[pltpu.VMEM((1, d), F32)],
        compiler_params=_cparams("arbitrary"),
        name="rwkv_proj",
    )(*args)


def _wkv_kernel(r_ref, w_ref, k_ref, a_ref, b_ref, v_ref, s0_ref, y_ref, s_ref):
    c = pl.program_id(1)
    tc, nk, _ = r_ref.shape

    @pl.when(c == 0)
    def _():
        s_ref[...] = s0_ref[...]

    def step(t, carry):
        vt = v_ref[t]
        parts = [None] * 4
        for kk in range(nk):
            term = s_ref[kk] * a_ref[t, pl.ds(kk, 1), :]
            parts[kk % 4] = term if parts[kk % 4] is None else parts[kk % 4] + term
        sa = (parts[0] + parts[1]) + (parts[2] + parts[3])
        parts = [None] * 4
        for kk in range(nk):
            s_new = (s_ref[kk] * w_ref[t, pl.ds(kk, 1), :] + sa * b_ref[t, pl.ds(kk, 1), :]
                     + vt * k_ref[t, pl.ds(kk, 1), :])
            s_ref[kk] = s_new
            term = s_new * r_ref[t, pl.ds(kk, 1), :]
            parts[kk % 4] = term if parts[kk % 4] is None else parts[kk % 4] + term
        y_ref[t] = (parts[0] + parts[1]) + (parts[2] + parts[3])
        return carry

    lax.fori_loop(0, tc, step, 0)


def _wkv_scan(r, w, k, a, b, v, s0, tc):
    t_len, nk, lanes = r.shape
    nv = v.shape[1]
    tc = min(tc, t_len)
    kspec = pl.BlockSpec((tc, nk, LANES), lambda l, c: (c, 0, l))
    vspec = pl.BlockSpec((tc, nv, LANES), lambda l, c: (c, 0, l))
    sspec = pl.BlockSpec((nk, nv, LANES), lambda l, c: (0, 0, l))
    return pl.pallas_call(
        _wkv_kernel,
        grid=(lanes // LANES, t_len // tc),
        in_specs=[kspec] * 5 + [vspec, sspec],
        out_specs=[vspec, sspec],
        out_shape=[jax.ShapeDtypeStruct(v.shape, F32), jax.ShapeDtypeStruct(s0.shape, F32)],
        compiler_params=_cparams("parallel", "arbitrary"),
        name="wkv_scan",
    )(r, w, k, a, b, v, s0)


def _to_klanes(x, n_seq, seq_len, dup):
    h = x.shape[-1] // RW_HEAD
    y = x.reshape(n_seq, seq_len, h, RW_HEAD).transpose(1, 3, 0, 2).reshape(seq_len, RW_HEAD, n_seq * h)
    return jnp.concatenate([y] * dup, axis=-1) if dup > 1 else y


def _to_vlanes(x, n_seq, seq_len, dup):
    h = x.shape[-1] // RW_HEAD
    y = x.reshape(n_seq, seq_len, h, dup, RW_HEAD // dup).transpose(1, 4, 3, 0, 2)
    return y.reshape(seq_len, RW_HEAD // dup, dup * n_seq * h)


def _from_vlanes(y, n_seq, seq_len, dup):
    h = y.shape[-1] // (dup * n_seq)
    y = y.reshape(seq_len, RW_HEAD // dup, dup, n_seq, h).transpose(3, 0, 4, 2, 1)
    return y.reshape(n_seq * seq_len, h * RW_HEAD)


def _state_to_lanes(s, dup):
    n_seq, h = s.shape[:2]
    y = s.reshape(n_seq, h, dup, RW_HEAD // dup, RW_HEAD).transpose(4, 3, 2, 0, 1)
    return y.reshape(RW_HEAD, RW_HEAD // dup, dup * n_seq * h)


def _state_from_lanes(s, n_seq, dup):
    h = s.shape[-1] // (dup * n_seq)
    y = s.reshape(RW_HEAD, RW_HEAD // dup, dup, n_seq, h).transpose(3, 4, 2, 1, 0)
    return y.reshape(n_seq, h, RW_HEAD, RW_HEAD)


def _wkv(grp, r, w, k, v, a, b, s0):
    n_seq, seq_len = grp.n_seq, grp.seq_len
    heads = r.shape[-1] // RW_HEAD
    dup = max(1, LANES // (n_seq * heads))
    kl = [_to_klanes(t, n_seq, seq_len, dup) for t in (r, w, k, a, b)]
    y, s = _wkv_scan(*kl, _to_vlanes(v, n_seq, seq_len, dup), _state_to_lanes(s0, dup), tc=64)
    return _from_vlanes(y, n_seq, seq_len, dup), _state_from_lanes(s, n_seq, dup)


def _wkv_t_kernel(r_ref, w_ref, k_ref, a_ref, b_ref, v_ref, s0_ref, y_ref, s_ref, ks_sc, vs_sc, ys_sc):
    c = pl.program_id(0)
    n_seq, d, tc = r_ref.shape
    nk = RW_HEAD
    heads = d // nk
    dup = LANES // (n_seq * heads)
    nv = nk // dup

    @pl.when(c == 0)
    def _():
        s_ref[...] = s0_ref[...]

    def head_rows(ref, ch):
        return [ref[b, pl.ds(ch, heads, stride=nk), :] for b in range(n_seq)]

    def load_k(kk, carry):
        for n, ref in enumerate((r_ref, w_ref, k_ref, a_ref, b_ref)):
            ks_sc[n, kk] = jnp.concatenate(head_rows(ref, kk) * dup, axis=0).T
        return carry

    lax.fori_loop(0, nk, load_k, 0, unroll=4)

    def load_v(vr, carry):
        rows = []
        for part in range(dup):
            rows += head_rows(v_ref, part * nv + vr)
        vs_sc[pl.ds(vr, tc, stride=nv), :] = jnp.concatenate(rows, axis=0).T
        return carry

    lax.fori_loop(0, nv, load_v, 0, unroll=2)

    def tree(parts):
        return (parts[0] + parts[1]) + (parts[2] + parts[3])

    parts = [None] * 4
    for kk in range(nk):
        term = s_ref[kk] * ks_sc[3, kk, 0:1, :]
        parts[kk % 4] = term if parts[kk % 4] is None else parts[kk % 4] + term

    def step(t, sa):
        trow = pl.ds(t, 1)
        nrow = pl.ds(jnp.minimum(t + 1, tc - 1), 1)
        vt = vs_sc[pl.ds(pl.multiple_of(t * nv, nv), nv), :]
        ys = [None] * 4
        sn = [None] * 4
        for kk in range(nk):
            s_new = (s_ref[kk] * ks_sc[1, kk, trow, :] + sa * ks_sc[4, kk, trow, :]
                     + vt * ks_sc[2, kk, trow, :])
            s_ref[kk] = s_new
            ty = s_new * ks_sc[0, kk, trow, :]
            ts = s_new * ks_sc[3, kk, nrow, :]
            ys[kk % 4] = ty if ys[kk % 4] is None else ys[kk % 4] + ty
            sn[kk % 4] = ts if sn[kk % 4] is None else sn[kk % 4] + ts
        ys_sc[pl.ds(pl.multiple_of(t * nv, nv), nv), :] = tree(ys)
        return tree(sn)

    lax.fori_loop(0, tc, step, tree(parts))

    def store_y(vr, carry):
        yt = ys_sc[pl.ds(vr, tc, stride=nv), :].T
        for part in range(dup):
            for b in range(n_seq):
                r0 = (part * n_seq + b) * heads
                y_ref[b, pl.ds(part * nv + vr, heads, stride=nk), :] = yt[r0:r0 + heads]
        return carry

    lax.fori_loop(0, nv, store_y, 0, unroll=2)


WKV_CHUNK = 128


def _wkv_t(r, w, k, a, b, v, s0):
    n_seq, d, t_len = r.shape
    heads = d // RW_HEAD
    assert LANES % (n_seq * heads) == 0 and t_len % WKV_CHUNK == 0
    dup = LANES // (n_seq * heads)
    nv = RW_HEAD // dup
    tc = WKV_CHUNK
    one = pl.Buffered(1)
    xspec = pl.BlockSpec((n_seq, d, tc), lambda c: (0, 0, c), pipeline_mode=one)
    sspec = pl.BlockSpec((RW_HEAD, nv, LANES), lambda c: (0, 0, 0), pipeline_mode=one)
    y, s = pl.pallas_call(
        _wkv_t_kernel,
        grid=(t_len // tc,),
        in_specs=[xspec] * 6 + [sspec],
        out_specs=[pl.BlockSpec((n_seq, d, tc), lambda c: (0, 0, c)),
                   pl.BlockSpec((RW_HEAD, nv, LANES), lambda c: (0, 0, 0))],
        out_shape=[jax.ShapeDtypeStruct(r.shape, F32), jax.ShapeDtypeStruct((RW_HEAD, nv, LANES), F32)],
        scratch_shapes=[pltpu.VMEM((5, RW_HEAD, tc, LANES), F32), pltpu.VMEM((tc * nv, LANES), F32),
                        pltpu.VMEM((tc * nv, LANES), F32)],
        compiler_params=_cparams("arbitrary"),
        name="wkv_scan_t",
    )(r, w, k, a, b, v, _state_to_lanes(s0, dup))
    return y, _state_from_lanes(s, n_seq, dup)


def _rwkv_post_kernel(t_minor, x_ref, y_ref, bonus_ref, g_ref, gate_ref, vec_ref, ones_ref, wo_ref, o_ref):
    vec = vec_ref[...]
    lnx_w, lnx_b = vec[0:1, :], vec[1:2, :]
    y = y_ref[0].T if t_minor else y_ref[...]
    ones = ones_ref[...]
    inv = 1.0 / RW_HEAD
    mu = _seg_sum64(y, ones) * inv
    yc = y - mu
    var = _seg_sum64(yc * yc, ones) * inv
    yn = yc * lax.rsqrt(var + GN_EPS) * lnx_w + lnx_b
    out = jnp.dot(((yn + bonus_ref[...]) * g_ref[...]).astype(BF16), wo_ref[...], preferred_element_type=F32)
    o_ref[...] = x_ref[...] + gate_ref[0] * out


def _rwkv_post(grp, x, y, bonus, g, gate, P, j, t_minor):
    d = x.shape[-1]
    vec = _pad_to(jnp.stack([P['rw_lnx_w'][j], P['rw_lnx_b'][j]]), 0, 8)
    ones = _block_ones(MXU_DIM)
    wo = P['rw_w_o'][j].astype(BF16)
    tps = grp.tiles_per_seq
    yspec = (pl.BlockSpec((1, d, grp.tile), lambda i: (i // tps, 0, i % tps)) if t_minor
             else grp.row_spec(d))
    return pl.pallas_call(
        functools.partial(_rwkv_post_kernel, t_minor),
        grid=(grp.n_tiles,),
        in_specs=[grp.row_spec(d), yspec, grp.row_spec(d), grp.row_spec(d), grp.seq_spec(d),
                  _const_spec(vec.shape), _const_spec(ones.shape), _const_spec(wo.shape)],
        out_specs=grp.row_spec(d),
        out_shape=jax.ShapeDtypeStruct(x.shape, F32),
        compiler_params=_cparams("parallel"),
        name="rwkv_post",
    )(x, y, bonus, g, grp.seq_rows(gate), vec, ones, wo)


Q_W = ATT_HEADS * ATT_HEAD_DIM
KV_W = KV_HEADS * ATT_HEAD_DIM
IQ_W = IDX_HEADS * IDX_DIM
OFF_K = Q_W
OFF_V = Q_W + KV_W
OFF_QI = Q_W + 2 * KV_W
OFF_KI = OFF_QI + IQ_W


def _rope_tables(pos):
    rd = ATT_HEAD_DIM // ROT_FRAC
    half = rd // 2
    inv = ROPE_THETA ** (-jnp.arange(half, dtype=F32) * 2.0 / rd)
    ang = pos.astype(F32)[:, None] * inv[None, :]
    cos, sin = jnp.cos(ang), jnp.sin(ang)
    n = pos.shape[0]
    rest = ATT_HEAD_DIM - rd
    c = jnp.concatenate([cos, cos, jnp.ones((n, rest), F32)], axis=-1)
    s_up = jnp.concatenate([-sin, jnp.zeros((n, half + rest), F32)], axis=-1)
    s_dn = jnp.concatenate([jnp.zeros((n, half), F32), sin, jnp.zeros((n, rest), F32)], axis=-1)
    return [jnp.concatenate([t, t], axis=-1) for t in (c, s_up, s_dn)]


def _rope128(x, c, s_up, s_dn):
    half = ATT_HEAD_DIM // ROT_FRAC // 2
    return x * c + pltpu.roll(x, LANES - half, axis=1) * s_up + pltpu.roll(x, half, axis=1) * s_dn


def _dsa_proj_kernel(x_ref, g_ref, sc_ref, sh_ref, w_ref, c_ref, su_ref, sd_ref, qg_ref, kg_ref, ig_ref,
                     ones_ref, q_o, k_o, v_o, qi_o, kiwi_o):
    h = _rms_mod(x_ref[...], g_ref[...], sc_ref[0], sh_ref[0])
    z = jnp.dot(h.astype(BF16), w_ref[...], preferred_element_type=F32)
    c, s_up, s_dn = c_ref[...], su_ref[...], sd_ref[...]
    ones = ones_ref[...]
    inv = 1.0 / ATT_HEAD_DIM
    lane = lax.broadcasted_iota(I32, (z.shape[0], LANES), 1)
    low = lane < ATT_HEAD_DIM

    def head_norm(t, gain, blk=ones):
        return t * lax.rsqrt(_seg_sum64(t * t, blk) * inv + NORM_EPS) * gain

    def pairs(t):
        return [t[:, n * LANES:(n + 1) * LANES] for n in range(t.shape[-1] // LANES)]

    q = head_norm(z[:, :Q_W], qg_ref[...]) * (LOG2E * ATT_HEAD_DIM ** -0.5)
    for n, t in enumerate(pairs(q)):
        t = _rope128(t, c, s_up, s_dn).astype(BF16)
        q_o[0, 2 * n] = t[:, :ATT_HEAD_DIM]
        q_o[0, 2 * n + 1] = t[:, ATT_HEAD_DIM:]
    k = head_norm(z[:, OFF_K:OFF_V], kg_ref[...])
    k_o[...] = jnp.concatenate([_rope128(t, c, s_up, s_dn) for t in pairs(k)], axis=-1)
    v_o[...] = z[:, OFF_V:OFF_QI]
    qi = z[:, OFF_QI:OFF_KI] * (IDX_DIM ** -0.5)
    for n, t in enumerate(pairs(qi)):
        t = _rope128(t, c, s_up, s_dn)
        qi_o[0, 2 * n] = jnp.where(low, t, 0.0).astype(BF16)
        qi_o[0, 2 * n + 1] = jnp.where(low, pltpu.roll(t, ATT_HEAD_DIM, axis=1), 0.0).astype(BF16)
    kw = z[:, OFF_KI:OFF_KI + LANES]
    ki = _rope128(head_norm(kw, ig_ref[...], ones[:LANES, :LANES]), c, s_up, s_dn)
    kiwi_o[...] = jnp.where(low, ki, kw * (IDX_HEADS ** -0.5))


def _dsa_proj(grp, x, g, sc, sh, pos, P, j):
    d = x.shape[-1]
    w = _pad_to(P['att_w_in'][j], 1, LANES).astype(BF16)
    tabs = [grp.pos_rows(t) for t in _rope_tables(pos)]
    qg = jnp.tile(P['att_q_norm'][j], ATT_HEADS).reshape(1, Q_W)
    kg = jnp.tile(P['att_k_norm'][j], KV_HEADS).reshape(1, KV_W)
    ig = _pad_to(P['idx_k_norm'][j], 0, LANES).reshape(1, LANES)
    ones = _block_ones(MXU_DIM)
    nsb = grp.n_seq if grp.per_seq else grp.n_tiles
    rows = grp.seq_len if grp.per_seq else grp.tile
    tps = grp.tiles_per_seq
    tt = grp.tile

    def head_spec(nh, wd):
        return pl.BlockSpec((1, nh, tt, wd), lambda i: (i // tps, 0, i % tps, 0))

    return pl.pallas_call(
        _dsa_proj_kernel,
        grid=(grp.n_tiles,),
        in_specs=[grp.row_spec(d), _const_spec((1, d)), grp.seq_spec(d), grp.seq_spec(d), _const_spec(w.shape),
                  grp.pos_spec(LANES), grp.pos_spec(LANES), grp.pos_spec(LANES),
                  _const_spec(qg.shape), _const_spec(kg.shape), _const_spec(ig.shape), _const_spec(ones.shape)],
        out_specs=[head_spec(ATT_HEADS, ATT_HEAD_DIM), grp.row_spec(KV_W), grp.row_spec(KV_W),
                   head_spec(IDX_HEADS, LANES), grp.row_spec(LANES)],
        out_shape=[jax.ShapeDtypeStruct((nsb, ATT_HEADS, rows, ATT_HEAD_DIM), BF16),
                   jax.ShapeDtypeStruct((grp.rows, KV_W), F32),
                   jax.ShapeDtypeStruct((grp.rows, KV_W), F32),
                   jax.ShapeDtypeStruct((nsb, IDX_HEADS, rows, LANES), BF16),
                   jax.ShapeDtypeStruct((grp.rows, LANES), F32)],
        compiler_params=_cparams("parallel"),
        name="dsa_proj",
    )(x, g, grp.seq_rows(sc), grp.seq_rows(sh), w, *tabs, qg, kg, ig, ones)


def _score_key(score, admissible):
    bits = pltpu.bitcast(score + 0.0, I32)
    key = jnp.where(bits >= 0, bits, bits ^ 0x7FFFFFFF)
    return jnp.where(admissible, key, NEG_INF_KEY)


def _topk_threshold(count, shape, topk, idx_bits):
    kf = float(topk)
    theta = jnp.where(count(lambda k, i: k >= 0) >= kf, jnp.full(shape, 0, I32), jnp.full(shape, INT_MIN, I32))

    def value_bit(n, theta):
        cand = theta + jnp.left_shift(jnp.int32(1), 30 - n)
        return jnp.where(count(lambda k, i: k >= cand) >= kf, cand, theta)

    theta = lax.fori_loop(0, 31, value_bit, theta)
    need = kf - count(lambda k, i: k > theta)

    def index_bit(n, cut):
        cand = cut + jnp.left_shift(jnp.int32(1), idx_bits - 1 - n)
        below = count(lambda k, i: jnp.where(k == theta, i, cand) < cand)
        return jnp.where(below < need, cand, cut)

    surplus = count(lambda k, i: k == theta) - need
    tied = jnp.where(theta > NEG_INF_KEY, surplus, 0.0)
    cut = lax.cond(jnp.max(tied) > 0.0,
                   lambda: lax.fori_loop(0, idx_bits, index_bit, jnp.zeros(shape, I32)),
                   lambda: jnp.full(shape, 2 ** 30, I32))
    return theta, cut


def _select_bias(key, idx, theta, cut):
    tie = jnp.where(idx <= cut, 0.0, MASK_BIAS)
    bias = jnp.where(key > theta, 0.0, jnp.where(key == theta, tie, MASK_BIAS))
    return jnp.where(key > NEG_INF_KEY, bias, MASK_BIAS)


ATT_ROW_BLOCK = 32


def _dsa_prompt_kernel(topk, q_ref, qi_ref, kiwiq_ref, x_ref, gate_ref, k_ref, v_ref, kiwi_ref, place_ref,
                       wo_ref, o_ref, key_sc, bias_sc, qblk_sc, kb_sc, vb_sc, s_sc, p_sc, m_sc, l_sc, alpha_sc,
                       acc_sc):
    qb = pl.program_id(1)
    tq = x_ref.shape[0]
    n_chunks, _, kc = key_sc.shape
    lt = kc // LANES
    nkc = (qb * tq + tq - 1) // kc + 1
    lane = lax.broadcasted_iota(I32, (tq, LANES), 1)
    row = lax.broadcasted_iota(I32, (tq, LANES), 0)
    qpos = qb * tq + row

    qi = qi_ref[0].reshape(IDX_HEADS * tq, LANES)
    wq = kiwiq_ref[...]
    wib = [jnp.broadcast_to(wq[:, IDX_DIM + h:IDX_DIM + h + 1], (tq, LANES)) for h in range(IDX_HEADS)]

    def score_chunk(c, carry):
        kic = kiwi_ref[pl.ds(pl.multiple_of(c * kc, kc), kc), :]
        logits = _bdot_nt(qi, kic)
        keys = []
        for j in range(lt):
            acc = None
            for h in range(IDX_HEADS):
                t = jnp.maximum(logits[h * tq:(h + 1) * tq, j * LANES:(j + 1) * LANES], 0.0) * wib[h]
                acc = t if acc is None else acc + t
            kpos = c * kc + j * LANES + lane
            keys.append(_score_key(acc, kpos <= qpos))
        key_sc[c] = jnp.concatenate(keys, axis=-1)
        return carry

    lax.fori_loop(0, nkc, score_chunk, 0)

    def count(pred):
        def body(c, acc):
            k = key_sc[c]
            for j in range(lt):
                m = pred(k[:, j * LANES:(j + 1) * LANES], c * kc + j * LANES + lane)
                acc = acc + jnp.where(m, 1.0, 0.0)
            return acc
        acc = lax.fori_loop(0, nkc, body, jnp.zeros((tq, LANES), F32))
        return jnp.sum(acc, axis=1, keepdims=True)

    theta, cut = _topk_threshold(count, (tq, LANES), topk, (n_chunks * kc - 1).bit_length())

    def bias_chunk(c, carry):
        k = key_sc[c]
        parts = [_select_bias(k[:, j * LANES:(j + 1) * LANES], c * kc + j * LANES + lane, theta, cut)
                 for j in range(lt)]
        bias_sc[c] = jnp.concatenate(parts, axis=-1)
        return carry

    lax.fori_loop(0, nkc, bias_chunk, 0)

    gq = GROUPS * tq
    for g in range(KV_HEADS):
        qg = q_ref[0, g * GROUPS:(g + 1) * GROUPS].reshape(gq, ATT_HEAD_DIM)
        qblk_sc[g] = jnp.dot(qg, place_ref[g], preferred_element_type=F32).astype(BF16)
    m_sc[...] = jnp.full_like(m_sc, MASK_BIAS)
    l_sc[...] = jnp.zeros_like(l_sc)
    acc_sc[...] = jnp.zeros_like(acc_sc)

    def att_chunk(c, carry):
        rows = pl.ds(pl.multiple_of(c * kc, kc), kc)
        kb_sc[...] = k_ref[rows, :].astype(BF16)
        vb_sc[...] = v_ref[rows, :].astype(BF16)

        for g in range(KV_HEADS):
            buf = g % 2
            base = g * gq
            s_sc[buf] = _bdot_nt(qblk_sc[g], kb_sc[...])
            for r0 in range(0, gq, ATT_ROW_BLOCK):
                t0 = r0 % tq
                rb = slice(r0, r0 + ATT_ROW_BLOCK)
                ms = slice(base + r0, base + r0 + ATT_ROW_BLOCK)
                mx = None
                for j in range(lt):
                    cols = slice(j * LANES, (j + 1) * LANES)
                    t = s_sc[buf, rb, cols] + bias_sc[c, t0:t0 + ATT_ROW_BLOCK, cols]
                    s_sc[buf, rb, cols] = t
                    mx = t if mx is None else jnp.maximum(mx, t)
                m_old = m_sc[ms, :]
                m_new = jnp.maximum(m_old, jnp.max(mx, axis=1, keepdims=True))
                alpha_sc[buf, rb, :] = jnp.exp2(m_old - m_new)
                m_sc[ms, :] = m_new
            for r0 in range(0, gq, ATT_ROW_BLOCK):
                rb = slice(r0, r0 + ATT_ROW_BLOCK)
                ms = slice(base + r0, base + r0 + ATT_ROW_BLOCK)
                m_new = m_sc[ms, :]
                tot = None
                for j in range(lt):
                    cols = slice(j * LANES, (j + 1) * LANES)
                    p = jnp.exp2(s_sc[buf, rb, cols] - m_new)
                    p_sc[buf, rb, cols] = p.astype(BF16)
                    tot = p if tot is None else tot + p
                l_sc[ms, :] = alpha_sc[buf, rb, :] * l_sc[ms, :] + jnp.sum(tot, axis=1, keepdims=True)
            a = alpha_sc[buf]
            rs = slice(base, base + gq)
            acc_sc[rs, :] = (acc_sc[rs, :] * jnp.concatenate([a] * (KV_W // LANES), axis=-1)
                             + jnp.dot(p_sc[buf], vb_sc[...], preferred_element_type=F32))
        return carry

    lax.fori_loop(0, nkc, att_chunk, 0)
    out = None
    for head in range(ATT_HEADS):
        rs = slice(head * tq, (head + 1) * tq)
        inv_l = 1.0 / l_sc[rs, :]
        o = acc_sc[rs, :] * jnp.concatenate([inv_l] * (KV_W // LANES), axis=-1)
        t = jnp.dot(o.astype(BF16), wo_ref[head], preferred_element_type=F32)
        out = t if out is None else out + t
    o_ref[...] = x_ref[...] + gate_ref[0] * out


def _head_placement():
    p = np.zeros((KV_HEADS, ATT_HEAD_DIM, KV_W), np.float32)
    for g in range(KV_HEADS):
        p[g, np.arange(ATT_HEAD_DIM), g * ATT_HEAD_DIM + np.arange(ATT_HEAD_DIM)] = 1.0
    return jnp.asarray(p, dtype=BF16)


def _dsa_prompt(n_seq, seq_len, x, gate, q, k, v, qi, kiwi, w_o):
    d = x.shape[-1]
    tq, kc = 128, 512
    assert seq_len % kc == 0
    topk = min(TOPK_MAX, seq_len // 4)
    nq = seq_len // tq
    gq = GROUPS * tq
    wo_h = w_o.astype(BF16).reshape(KV_HEADS, GROUPS, 1, ATT_HEAD_DIM, d)
    sel = jnp.eye(KV_HEADS, dtype=BF16).reshape(KV_HEADS, 1, KV_HEADS, 1, 1)
    wo = (wo_h * sel).reshape(ATT_HEADS, KV_W, d)
    place = _head_placement()
    return pl.pallas_call(
        functools.partial(_dsa_prompt_kernel, topk),
        grid=(n_seq, nq),
        in_specs=[pl.BlockSpec((1, ATT_HEADS, tq, ATT_HEAD_DIM), lambda b, i: (b, 0, i, 0)),
                  pl.BlockSpec((1, IDX_HEADS, tq, LANES), lambda b, i: (b, 0, i, 0)),
                  pl.BlockSpec((tq, LANES), lambda b, i: (b * nq + i, 0)),
                  pl.BlockSpec((tq, d), lambda b, i: (b * nq + i, 0)),
                  pl.BlockSpec((1, 1, d), lambda b, i: (b, 0, 0)),
                  pl.BlockSpec((seq_len, KV_W), lambda b, i: (b, 0)),
                  pl.BlockSpec((seq_len, KV_W), lambda b, i: (b, 0)),
                  pl.BlockSpec((seq_len, LANES), lambda b, i: (b, 0)),
                  _const_spec(place.shape), _const_spec(wo.shape)],
        out_specs=pl.BlockSpec((tq, d), lambda b, i: (b * nq + i, 0)),
        out_shape=jax.ShapeDtypeStruct(x.shape, F32),
        scratch_shapes=[pltpu.VMEM((seq_len // kc, tq, kc), I32), pltpu.VMEM((seq_len // kc, tq, kc), F32),
                        pltpu.VMEM((KV_HEADS, gq, KV_W), BF16), pltpu.VMEM((kc, KV_W), BF16),
                        pltpu.VMEM((kc, KV_W), BF16), pltpu.VMEM((2, gq, kc), F32),
                        pltpu.VMEM((2, gq, kc), BF16), pltpu.VMEM((ATT_HEADS * tq, LANES), F32),
                        pltpu.VMEM((ATT_HEADS * tq, LANES), F32), pltpu.VMEM((2, gq, LANES), F32),
                        pltpu.VMEM((ATT_HEADS * tq, KV_W), F32)],
        compiler_params=_cparams("parallel", "arbitrary"),
        name="dsa_prompt",
    )(q, qi, kiwi, x, gate.reshape(n_seq, 1, d), k, v, kiwi, place, wo)


PAGES_PER_STEP = 8


def _dsa_sel_kernel(topk, pt_ref, qi_ref, wib_ref, *refs):
    page_refs = refs[:PAGES_PER_STEP]
    kinew_ref, bias_ref, key_sc = refs[PAGES_PER_STEP:]
    p = pl.program_id(1)
    nsteps = pl.num_programs(1)
    npg = key_sc.shape[0] - 1
    t_new = bias_ref.shape[2]
    qi = qi_ref[0]
    wib = wib_ref[0]

    def score(keys):
        t = jnp.maximum(_bdot(qi, keys), 0.0) * wib
        acc = t[0:t_new]
        for h in range(1, IDX_HEADS):
            acc = acc + t[h * t_new:(h + 1) * t_new]
        return acc

    for n, page_ref in enumerate(page_refs):
        key_sc[p * PAGES_PER_STEP + n] = _score_key(score(page_ref[0, 0]), True)

    @pl.when(p == nsteps - 1)
    def _():
        lane = lax.broadcasted_iota(I32, (t_new, LANES), 1)
        row = lax.broadcasted_iota(I32, (t_new, LANES), 0)
        key_sc[npg] = _score_key(score(kinew_ref[0]), lane <= row)
        keys = key_sc[...]
        idx = (lax.broadcasted_iota(I32, keys.shape, 0) * LANES + lax.broadcasted_iota(I32, keys.shape, 2))

        def count(pred):
            c = jnp.sum(jnp.where(pred(keys, idx), 1.0, 0.0), axis=0)
            return jnp.sum(c, axis=1, keepdims=True)

        theta, cut = _topk_threshold(count, (t_new, LANES), topk, (keys.shape[0] * LANES - 1).bit_length())
        bias_ref[0] = _select_bias(keys, idx, theta, cut)


def _page_spec(width, n, layer):
    return pl.BlockSpec((1, 1, width, PAGE),
                        lambda b, p, pt: (layer, pt[b, p * PAGES_PER_STEP + n], 0, 0))


def _dsa_sel(page_table, qi, wib, cache_ki, layer, kinew, t_new):
    n_seq, npg = page_table.shape
    assert npg % PAGES_PER_STEP == 0
    topk = min(TOPK_MAX, (npg * PAGE + t_new) // 4)
    rows = qi.shape[1]
    grid_spec = pltpu.PrefetchScalarGridSpec(
        num_scalar_prefetch=1,
        grid=(n_seq, npg // PAGES_PER_STEP),
        in_specs=[pl.BlockSpec((1, rows, IDX_DIM), lambda b, p, pt: (b, 0, 0)),
                  pl.BlockSpec((1, rows, LANES), lambda b, p, pt: (b, 0, 0))]
        + [_page_spec(IDX_DIM, n, layer) for n in range(PAGES_PER_STEP)]
        + [pl.BlockSpec((1, IDX_DIM, PAGE), lambda b, p, pt: (b, 0, 0))],
        out_specs=pl.BlockSpec((1, npg + 1, t_new, LANES), lambda b, p, pt: (b, 0, 0, 0)),
        scratch_shapes=[pltpu.VMEM((npg + 1, t_new, LANES), I32)])
    return pl.pallas_call(
        functools.partial(_dsa_sel_kernel, topk),
        grid_spec=grid_spec,
        out_shape=jax.ShapeDtypeStruct((n_seq, npg + 1, t_new, LANES), F32),
        compiler_params=_cparams("parallel", "arbitrary"),
        name="dsa_sel",
    )(page_table, qi, wib, *([cache_ki] * PAGES_PER_STEP), kinew)


def _dsa_att_kernel(pt_ref, q_ref, *refs):
    kpage_refs = refs[:PAGES_PER_STEP]
    vpage_refs = refs[PAGES_PER_STEP:2 * PAGES_PER_STEP]
    knew_ref, vnew_ref, bias_ref, biasnew_ref, o_ref, m_sc, l_sc, acc_sc = refs[2 * PAGES_PER_STEP:]
    p = pl.program_id(1)
    last = pl.num_programs(1) - 1
    rows = q_ref.shape[1]
    t_new = bias_ref.shape[2]

    @pl.when(p == 0)
    def _():
        m_sc[...] = jnp.full_like(m_sc, MASK_BIAS)
        l_sc[...] = jnp.zeros_like(l_sc)
        acc_sc[...] = jnp.zeros_like(acc_sc)

    def attend(kv_bias):
        q = q_ref[0]
        ss = []
        for kk, _, bias in kv_bias:
            s = _bdot(q, kk)
            ss.append((s.reshape(rows // t_new, t_new, LANES) + bias[None]).reshape(rows, LANES))
        mx = ss[0]
        for s in ss[1:]:
            mx = jnp.maximum(mx, s)
        m_old = m_sc[...]
        m_new = jnp.maximum(m_old, jnp.max(mx, axis=1, keepdims=True))
        alpha = jnp.exp2(m_old - m_new)
        tot = None
        pv = None
        for s, (_, vv, _) in zip(ss, kv_bias):
            pr = jnp.exp2(s - m_new)
            tot = pr if tot is None else tot + pr
            t = _bdot_nt(pr, vv)
            pv = t if pv is None else pv + t
        l_sc[...] = alpha * l_sc[...] + jnp.sum(tot, axis=1, keepdims=True)
        acc_sc[...] = alpha * acc_sc[...] + pv
        m_sc[...] = m_new

    attend([(kpage_refs[n][0, 0], vpage_refs[n][0, 0], bias_ref[0, n]) for n in range(PAGES_PER_STEP)])

    @pl.when(p == last)
    def _():
        attend([(knew_ref[0], vnew_ref[0], biasnew_ref[0, 0])])
        o_ref[0] = acc_sc[...] / l_sc[...]


def _dsa_att(page_table, qblk, cache_k, cache_v, layer, knew, vnew, bias):
    n_seq, npg = page_table.shape
    rows = qblk.shape[1]
    t_new = bias.shape[2]
    grid_spec = pltpu.PrefetchScalarGridSpec(
        num_scalar_prefetch=1,
        grid=(n_seq, npg // PAGES_PER_STEP),
        in_specs=[pl.BlockSpec((1, rows, KV_W), lambda b, p, pt: (b, 0, 0))]
        + [_page_spec(KV_W, n, layer) for n in range(PAGES_PER_STEP)] * 2
        + [pl.BlockSpec((1, KV_W, PAGE), lambda b, p, pt: (b, 0, 0)),
           pl.BlockSpec((1, KV_W, PAGE), lambda b, p, pt: (b, 0, 0)),
           pl.BlockSpec((1, PAGES_PER_STEP, t_new, LANES), lambda b, p, pt: (b, p, 0, 0)),
           pl.BlockSpec((1, 1, t_new, LANES), lambda b, p, pt: (b, npg, 0, 0))],
        out_specs=pl.BlockSpec((1, rows, KV_W), lambda b, p, pt: (b, 0, 0)),
        scratch_shapes=[pltpu.VMEM((rows, 1), F32), pltpu.VMEM((rows, 1), F32), pltpu.VMEM((rows, KV_W), F32)])
    return pl.pallas_call(
        _dsa_att_kernel,
        grid_spec=grid_spec,
        out_shape=jax.ShapeDtypeStruct((n_seq, rows, KV_W), F32),
        compiler_params=_cparams("parallel", "arbitrary"),
        name="dsa_att",
    )(page_table, qblk, *([cache_k] * PAGES_PER_STEP), *([cache_v] * PAGES_PER_STEP), knew, vnew, bias, bias)


def _linear_res_kernel(x_ref, a_ref, gate_ref, w_ref, o_ref):
    o_ref[...] = x_ref[...] + gate_ref[0] * _bdot(a_ref[...], w_ref[...])


def _linear_res(grp, x, a, gate, w):
    d = x.shape[-1]
    wb = w.astype(BF16)
    return pl.pallas_call(
        _linear_res_kernel,
        grid=(grp.n_tiles,),
        in_specs=[grp.row_spec(d), grp.row_spec(a.shape[-1]), grp.seq_spec(d), _const_spec(wb.shape)],
        out_specs=grp.row_spec(d),
        out_shape=jax.ShapeDtypeStruct(x.shape, F32),
        compiler_params=_cparams("parallel"),
        name="linear_res",
    )(x, a, grp.seq_rows(gate), wb)


def _dsa_sample(grp, x, gate, q, k, v, qi, kiwi, cache_k, cache_v, cache_ki, layer, page_table, w_o):
    n_seq, t_new = grp.n_seq, grp.seq_len
    assert t_new <= PAGE and grp.n_tiles == 1

    def new_page(t):
        return _pad_to(t.reshape(n_seq, t_new, t.shape[-1]), 1, PAGE).transpose(0, 2, 1)

    qi_b = qi[0, :, :, :IDX_DIM].reshape(IDX_HEADS, n_seq, t_new, IDX_DIM).transpose(1, 0, 2, 3)
    qi_b = qi_b.reshape(n_seq, IDX_HEADS * t_new, IDX_DIM)
    wi = kiwi[:, IDX_DIM:IDX_DIM + IDX_HEADS].reshape(n_seq, t_new, IDX_HEADS).transpose(0, 2, 1)
    wib = jnp.broadcast_to(wi.reshape(n_seq, IDX_HEADS * t_new, 1), (n_seq, IDX_HEADS * t_new, LANES))
    bias = _dsa_sel(page_table, qi_b, wib, cache_ki, layer, new_page(kiwi[:, :IDX_DIM]), t_new)
    q_b = q[0].reshape(KV_HEADS, GROUPS, n_seq, t_new, ATT_HEAD_DIM).transpose(2, 0, 1, 3, 4)
    eye = jnp.eye(KV_HEADS, dtype=q_b.dtype)
    qblk = (q_b[:, :, :, :, None, :] * eye[None, :, None, None, :, None])
    qblk = qblk.reshape(n_seq, ATT_HEADS * t_new, KV_W)
    o = _dsa_att(page_table, qblk, cache_k, cache_v, layer, new_page(k), new_page(v), bias)
    o = o.reshape(n_seq, KV_HEADS, GROUPS, t_new, KV_HEADS, ATT_HEAD_DIM)
    o = jnp.stack([o[:, g, :, :, g, :] for g in range(KV_HEADS)], axis=1)
    o = o.transpose(0, 3, 1, 2, 4).reshape(n_seq * t_new, Q_W)
    return _linear_res(grp, x, o, gate, w_o)


TOKEN_TILE = 256
FFN_TILE = 1024


def _trunk(x, mods, pos, wkv0, shift0, attn_fn, P):
    n_seq, seq_len, d = x.shape
    depth = mods.shape[0]
    grp = _Group(n_seq, seq_len, TOKEN_TILE)
    grp_ffn = _Group(n_seq, seq_len, FFN_TILE)
    x = x.reshape(n_seq * seq_len, d)
    v_first = None
    ks, vs, kis, wkvs, shifts = [], [], [], [], []
    for i in range(depth):
        j = i // 2
        sh_a, sc_a, g_a, sh_f, sc_f, g_f = jnp.split(mods[i], 6, axis=-1)
        g_att = P['norm_g'][i, 0].reshape(1, d)
        g_ffn = P['norm_g'][i, 1].reshape(1, d)
        if i % 2 == 0:
            t_minor = grp.per_seq and seq_len % WKV_CHUNK == 0 and LANES % (n_seq * (d // RW_HEAD)) == 0
            outs = _rwkv_proj(grp, x, g_att, sc_a, sh_a, shift0[j], P, j, v_first, t_minor)
            v, gate, bonus, h = outs[-4:]
            if j == 0:
                v_first = v
            if t_minor:
                y, state = _wkv_t(*outs[:6], wkv0[j])
            else:
                r, w, k, a, b = outs[:5]
                y, state = _wkv(grp, r, w, k, v, a, b, wkv0[j])
            x = _rwkv_post(grp, x, y, bonus, gate, g_a, P, j, t_minor)
            wkvs.append(state)
            shifts.append(h.reshape(n_seq, seq_len, d)[:, -1])
        else:
            q, k, v, qi, kiwi = _dsa_proj(grp, x, g_att, sc_a, sh_a, pos, P, j)
            x = attn_fn(j, grp, x, g_a, q, k, v, qi, kiwi)
            ks.append(k.reshape(n_seq, seq_len, KV_HEADS, ATT_HEAD_DIM))
            vs.append(v.reshape(n_seq, seq_len, KV_HEADS, ATT_HEAD_DIM))
            kis.append(kiwi[:, :IDX_DIM].reshape(n_seq, seq_len, IDX_DIM))
        x = _ffn(grp_ffn, x, g_ffn, sc_f, sh_f, g_f, P['w_up'][i], P['w_down'][i])
    return (x.reshape(n_seq, seq_len, d), jnp.stack(ks), jnp.stack(vs), jnp.stack(kis),
            jnp.stack(wkvs), jnp.stack(shifts))


def kernel(x_prompt, x_sample, cache_k, cache_v, cache_idx_k, state_wkv, state_shift, page_table,
           c_prompt, c_sample, norm_g, w_ada, b_ada, w_up, w_down, rw_mix, rw_w_rkv, rw_w_o,
           rw_w0, rw_w1, rw_w2, rw_a0, rw_a1, rw_a2, rw_v0, rw_v1, rw_v2, rw_g1, rw_g2,
           rw_k_k, rw_k_a, rw_r_k, rw_lnx_w, rw_lnx_b, att_w_in, att_w_o, att_q_norm,
           att_k_norm, idx_k_norm):
    P = dict(norm_g=norm_g, w_up=w_up.astype(BF16), w_down=w_down.astype(BF16),
             rw_mix=rw_mix, rw_w_rkv=rw_w_rkv, rw_w_o=rw_w_o, rw_w0=rw_w0, rw_w1=rw_w1,
             rw_w2=rw_w2, rw_a0=rw_a0, rw_a1=rw_a1, rw_a2=rw_a2, rw_v0=rw_v0, rw_v1=rw_v1,
             rw_v2=rw_v2, rw_g1=rw_g1, rw_g2=rw_g2, rw_k_k=rw_k_k, rw_k_a=rw_k_a, rw_r_k=rw_r_k,
             rw_lnx_w=rw_lnx_w, rw_lnx_b=rw_lnx_b, att_w_in=att_w_in, att_w_o=att_w_o,
             att_q_norm=att_q_norm, att_k_norm=att_k_norm, idx_k_norm=idx_k_norm)
    n_p, seq, d = x_prompt.shape
    n_s, dec_seq, _ = x_sample.shape
    n_rwkv = state_wkv.shape[0]
    heads = d // RW_HEAD
    past_len = page_table.shape[1] * PAGE

    mods = _ada(jnp.concatenate([c_prompt, c_sample], axis=0), w_ada, b_ada)
    n_dsa, pool = cache_k.shape[:2]
    pages_k = cache_k.transpose(0, 1, 3, 4, 2).reshape(n_dsa, pool, KV_W, PAGE)
    pages_v = cache_v.transpose(0, 1, 3, 4, 2).reshape(n_dsa, pool, KV_W, PAGE)
    pages_ki = cache_idx_k.transpose(0, 1, 3, 2)

    def prompt_attn(j, grp, x, gate, q, k, v, qi, kiwi):
        return _dsa_prompt(grp.n_seq, grp.seq_len, x, gate, q, k, v, qi, kiwi, att_w_o[j])

    def sample_attn(j, grp, x, gate, q, k, v, qi, kiwi):
        return _dsa_sample(grp, x, gate, q, k, v, qi, kiwi, pages_k, pages_v, pages_ki, j,
                           page_table, att_w_o[j])

    wkv0 = jnp.zeros((n_rwkv, n_p, heads, RW_HEAD, RW_HEAD), F32)
    shift0 = jnp.zeros((n_rwkv, n_p, d), F32)
    y_p, k_p, v_p, ki_p, wkv_p, shift_p = _trunk(
        x_prompt, mods[:, :n_p], jnp.arange(seq), wkv0, shift0, prompt_attn, P)
    y_s, k_s, v_s, ki_s, wkv_s, shift_s = _trunk(
        x_sample, mods[:, n_p:], past_len + jnp.arange(dec_seq), state_wkv, state_shift, sample_attn, P)
    return (y_p, y_s, k_p, v_p, ki_p, wkv_p, shift_p, k_s, v_s, ki_s, wkv_s, shift_s)
```

```python
import functools

import jax
import jax.numpy as jnp
import numpy as np
from jax import lax
from jax.experimental import pallas as pl
from jax.experimental.pallas import tpu as pltpu

F32 = jnp.float32
BF16 = jnp.bfloat16
I32 = jnp.int32
I16 = jnp.int16

NORM_EPS = 1e-6
GN_EPS = 64e-5
RW_HEAD = 64
ATT_HEADS = 16
ATT_HEAD_DIM = 64
KV_HEADS = 4
GROUPS = ATT_HEADS // KV_HEADS
IDX_HEADS = 8
IDX_DIM = 64
TOPK_MAX = 256
ROPE_THETA = 500000.0
ROT_FRAC = 4
PAGE = 128

LANES = 128
MXU_DIM = 256
VMEM_LIMIT = 56 * 1024 * 1024

LOG2E = 1.4426950408889634
INT_MIN = -(2 ** 31)
NEG_INF_KEY = INT_MIN + 0x7FFFFF
MASK_BIAS = -1e30


def _cparams(*sem):
    return pltpu.CompilerParams(dimension_semantics=sem, vmem_limit_bytes=VMEM_LIMIT)


def _const_spec(shape):
    n = len(shape)
    return pl.BlockSpec(shape, lambda *_: (0,) * n, pipeline_mode=pl.Buffered(1))


def _bdot(a, b):
    return jnp.dot(a.astype(BF16), b.astype(BF16), preferred_element_type=F32)


def _bdot_nt(a, b):
    return lax.dot_general(a.astype(BF16), b.astype(BF16), (((1,), (1,)), ((), ())),
                           preferred_element_type=F32)


def _rms_mod(x, g, sc, sh):
    ms = jnp.mean(x * x, axis=-1, keepdims=True)
    return x * lax.rsqrt(ms + NORM_EPS) * g * (1.0 + sc) + sh


def _seg_sum64(x, ones):
    hi = x.astype(BF16)
    lo = (x - hi.astype(F32)).astype(BF16)
    w = ones.shape[0]
    outs = []
    for c in range(x.shape[-1] // w):
        sl = slice(c * w, (c + 1) * w)
        outs.append(jnp.dot(hi[:, sl], ones, preferred_element_type=F32)
                    + jnp.dot(lo[:, sl], ones, preferred_element_type=F32))
    return outs[0] if len(outs) == 1 else jnp.concatenate(outs, axis=-1)


def _block_ones(width):
    i = np.arange(width) // 64
    return jnp.asarray((i[:, None] == i[None, :]).astype(np.float32), dtype=BF16)


def _ada_kernel(c_ref, w_ref, b_ref, o_ref):
    c = c_ref[...]
    s = c * jax.nn.sigmoid(c)
    o_ref[0] = _bdot(s, w_ref[0]) + b_ref[0]


def _ada(c_all, w_ada, b_ada):
    depth, d, d6 = w_ada.shape
    nb = c_all.shape[0]
    tn = 1536
    return pl.pallas_call(
        _ada_kernel,
        grid=(depth, d6 // tn),
        in_specs=[pl.BlockSpec((nb, d), lambda i, j: (0, 0)),
                  pl.BlockSpec((1, d, tn), lambda i, j: (i, 0, j)),
                  pl.BlockSpec((1, 1, tn), lambda i, j: (i, 0, j))],
        out_specs=pl.BlockSpec((1, nb, tn), lambda i, j: (i, 0, j)),
        out_shape=jax.ShapeDtypeStruct((depth, nb, d6), F32),
        compiler_params=_cparams("parallel", "parallel"),
        name="ada",
    )(c_all, w_ada, b_ada.reshape(depth, 1, d6))


class _Group:
    def __init__(self, n_seq, seq_len, tile):
        self.n_seq, self.seq_len = n_seq, seq_len
        self.rows = n_seq * seq_len
        self.tile = min(tile, self.rows)
        assert self.rows % self.tile == 0
        assert seq_len % self.tile == 0 or self.tile % seq_len == 0
        self.tiles_per_seq = max(seq_len // self.tile, 1)
        self.per_seq = seq_len >= self.tile
        self.n_tiles = self.rows // self.tile

    def seq_rows(self, m):
        d = m.shape[-1]
        if self.per_seq:
            return m.reshape(self.n_seq, 1, d)
        return jnp.repeat(m, self.seq_len, axis=0).reshape(self.n_tiles, self.tile, d)

    def seq_spec(self, d):
        r = 1 if self.per_seq else self.tile
        tps = self.tiles_per_seq
        return pl.BlockSpec((1, r, d), lambda i, *_: (i // tps, 0, 0))

    def pos_rows(self, tab):
        if self.per_seq:
            return tab
        return jnp.tile(tab, (self.tile // self.seq_len, 1))

    def pos_spec(self, w):
        tps = self.tiles_per_seq
        return pl.BlockSpec((self.tile, w), lambda i, *_: (i % tps, 0))

    def row_spec(self, w):
        return pl.BlockSpec((self.tile, w), lambda i, *_: (i, 0))


def _ffn_kernel(x_ref, g_ref, sc_ref, sh_ref, gate_ref, wu_ref, wd_ref, o_ref, h_sc, acc_sc):
    j = pl.program_id(1)

    @pl.when(j == 0)
    def _():
        h_sc[...] = _rms_mod(x_ref[...], g_ref[...], sc_ref[0], sh_ref[0]).astype(BF16)
        acc_sc[...] = jnp.zeros_like(acc_sc)

    u = jnp.maximum(jnp.dot(h_sc[...], wu_ref[...], preferred_element_type=F32), 0.0)
    acc_sc[...] += jnp.dot((u * u).astype(BF16), wd_ref[...], preferred_element_type=F32)

    @pl.when(j == pl.num_programs(1) - 1)
    def _():
        o_ref[...] = x_ref[...] + gate_ref[0] * acc_sc[...]


def _ffn(grp, x, g, sc, sh, gate, w_up, w_down):
    d, dff = w_up.shape
    tf = 512
    tm = grp.tile
    return pl.pallas_call(
        _ffn_kernel,
        grid=(grp.n_tiles, dff // tf),
        in_specs=[grp.row_spec(d), pl.BlockSpec((1, d), lambda i, j: (0, 0)),
                  grp.seq_spec(d), grp.seq_spec(d), grp.seq_spec(d),
                  pl.BlockSpec((d, tf), lambda i, j: (0, j)),
                  pl.BlockSpec((tf, d), lambda i, j: (j, 0))],
        out_specs=grp.row_spec(d),
        out_shape=jax.ShapeDtypeStruct(x.shape, F32),
        scratch_shapes=[pltpu.VMEM((tm, d), BF16), pltpu.VMEM((tm, d), F32)],
        compiler_params=_cparams("parallel", "arbitrary"),
        name="ffn",
    )(x, g, grp.seq_rows(sc), grp.seq_rows(sh), grp.seq_rows(gate), w_up, w_down)


def _rwkv_proj_kernel(has_vfirst, t_minor, seq_len, *refs):
    (x_ref, g_ref, sc_ref, sh_ref, shift_ref, mix_ref, wr_ref, wk_ref, wv_ref,
     w1_ref, w2_ref, a1_ref, a2_ref, g1_ref, g2_ref, vec_ref, ones_ref) = refs[:17]
    refs = refs[17:]
    if has_vfirst:
        v1_ref, v2_ref, vf_ref = refs[:3]
        refs = refs[3:]
    scan_o = refs[:6] if t_minor else refs[:5]
    v_o, g_o, bonus_o, h_o, carry_sc = refs[len(scan_o):]
    i = pl.program_id(0)
    tt = x_ref.shape[0]

    @pl.when(i == 0)
    def _():
        carry_sc[...] = jnp.zeros_like(carry_sc)

    h = _rms_mod(x_ref[...], g_ref[...], sc_ref[0], sh_ref[0])
    h_o[...] = h
    row = lax.broadcasted_iota(I32, h.shape, 0)
    hp = pltpu.roll(h, 1, axis=0)
    hp = jnp.where(row == 0, carry_sc[...], hp)
    hp = jnp.where(((row + i * tt) & (seq_len - 1)) == 0, shift_ref[0], hp)
    carry_sc[...] = h[tt - 1:tt, :]
    dx = hp - h
    mix = mix_ref[...]
    xr, xw, xk, xv, xa, xg = [h + dx * mix[n:n + 1, :] for n in range(6)]
    vec = vec_ref[...]
    w0, a0, k_k, k_a, v0, r_k = [vec[n:n + 1, :] for n in range(6)]
    ones = ones_ref[...]

    r = jnp.dot(xr.astype(BF16), wr_ref[...], preferred_element_type=F32)
    k = jnp.dot(xk.astype(BF16), wk_ref[...], preferred_element_type=F32)
    v = jnp.dot(xv.astype(BF16), wv_ref[...], preferred_element_type=F32)
    w_pre = w0 + _bdot(jnp.tanh(_bdot(xw, w1_ref[...])), w2_ref[...])
    decay = jnp.exp(-float(np.exp(-0.5)) * jax.nn.sigmoid(w_pre))
    if has_vfirst:
        vgate = jax.nn.sigmoid(v0 + _bdot(_bdot(xv, v1_ref[...]), v2_ref[...]))
        v = v + (vf_ref[...] - v) * vgate
    a = jax.nn.sigmoid(a0 + _bdot(_bdot(xa, a1_ref[...]), a2_ref[...]))
    g_o[...] = _bdot(jax.nn.sigmoid(_bdot(xg, g1_ref[...])), g2_ref[...])
    kk = k * k_k
    kk = kk * lax.rsqrt(jnp.maximum(_seg_sum64(kk * kk, ones), 1e-24))
    k = k * (1.0 + (a - 1.0) * k_a)
    v_o[...] = v
    bonus_o[...] = _seg_sum64(r * k * r_k, ones) * v
    scan = (r, decay, k, -kk, kk * a)
    if t_minor:
        for o_ref, t in zip(scan_o, scan + (v,)):
            o_ref[0] = t.T
    else:
        for o_ref, t in zip(scan_o, scan):
            o_ref[...] = t


def _pad_to(w, axis, mult):
    n = w.shape[axis]
    pad = (-n) % mult
    if pad == 0:
        return w
    cfg = [(0, 0)] * w.ndim
    cfg[axis] = (0, pad)
    return jnp.pad(w, cfg)


def _lora_pair(w_in, w_out):
    return (_pad_to(w_in, 1, LANES).astype(BF16), _pad_to(w_out, 0, LANES).astype(BF16))


def _rwkv_proj(grp, x, g, sc, sh, shift, P, j, v_first, t_minor):
    d = x.shape[-1]
    assert grp.seq_len & (grp.seq_len - 1) == 0, "sequence-start test uses a bit mask"
    has_vfirst = v_first is not None
    w1, w2 = _lora_pair(P['rw_w1'][j], P['rw_w2'][j])
    a1, a2 = _lora_pair(P['rw_a1'][j], P['rw_a2'][j])
    g1, g2 = _lora_pair(P['rw_g1'][j], P['rw_g2'][j])
    v0 = P['rw_v0'][j - 1] if has_vfirst else jnp.zeros((d,), F32)
    vec = jnp.stack([P['rw_w0'][j], P['rw_a0'][j], P['rw_k_k'][j], P['rw_k_a'][j], v0,
                     P['rw_r_k'][j].reshape(d), jnp.zeros((d,), F32), jnp.zeros((d,), F32)])
    mix = _pad_to(P['rw_mix'][j], 0, 8)
    wrkv = P['rw_w_rkv'][j].astype(BF16)
    ones = _block_ones(MXU_DIM)
    args = [x, g, grp.seq_rows(sc), grp.seq_rows(sh), grp.seq_rows(shift), mix,
            wrkv[0], wrkv[1], wrkv[2], w1, w2, a1, a2, g1, g2, vec, ones]
    specs = [grp.row_spec(d), _const_spec((1, d)), grp.seq_spec(d), grp.seq_spec(d), grp.seq_spec(d),
             _const_spec(mix.shape)] + [_const_spec(a.shape) for a in args[6:]]
    if has_vfirst:
        v1, v2 = _lora_pair(P['rw_v1'][j - 1], P['rw_v2'][j - 1])
        args += [v1, v2, v_first]
        specs += [_const_spec(v1.shape), _const_spec(v2.shape), grp.row_spec(d)]
    out = jax.ShapeDtypeStruct(x.shape, F32)
    if t_minor:
        assert grp.per_seq and grp.tile % LANES == 0
        tps = grp.tiles_per_seq
        scan_specs = [pl.BlockSpec((1, d, grp.tile), lambda i: (i // tps, 0, i % tps))] * 6
        scan_shapes = [jax.ShapeDtypeStruct((grp.n_seq, d, grp.seq_len), F32)] * 6
    else:
        scan_specs = [grp.row_spec(d)] * 5
        scan_shapes = [out] * 5
    return pl.pallas_call(
        functools.partial(_rwkv_proj_kernel, has_vfirst, t_minor, grp.seq_len),
        grid=(grp.n_tiles,),
        in_specs=specs,
        out_specs=scan_specs + [grp.row_spec(d)] * 4,
        out_shape=scan_shapes + [out] * 4,
        scratch_shapes=[pltpu.VMEM((1, d), F32)],
        compiler_params=_cparams("arbitrary"),
        name="rwkv_proj",
    )(*args)


def _wkv_kernel(r_ref, w_ref, k_ref, a_ref, b_ref, v_ref, s0_ref, y_ref, s_ref):
    c = pl.program_id(1)
    tc, nk, _ = r_ref.shape

    @pl.when(c == 0)
    def _():
        s_ref[...] = s0_ref[...]

    def step(t, carry):
        vt = v_ref[t]
        parts = [None] * 4
        for kk in range(nk):
            term = s_ref[kk] * a_ref[t, pl.ds(kk, 1), :]
            parts[kk % 4] = term if parts[kk % 4] is None else parts[kk % 4] + term
        sa = (parts[0] + parts[1]) + (parts[2] + parts[3])
        parts = [None] * 4
        for kk in range(nk):
            s_new = (s_ref[kk] * w_ref[t, pl.ds(kk, 1), :] + sa * b_ref[t, pl.ds(kk, 1), :]
                     + vt * k_ref[t, pl.ds(kk, 1), :])
            s_ref[kk] = s_new
            term = s_new * r_ref[t, pl.ds(kk, 1), :]
            parts[kk % 4] = term if parts[kk % 4] is None else parts[kk % 4] + term
        y_ref[t] = (parts[0] + parts[1]) + (parts[2] + parts[3])
        return carry

    lax.fori_loop(0, tc, step, 0)


def _wkv_scan(r, w, k, a, b, v, s0, tc):
    t_len, nk, lanes = r.shape
    nv = v.shape[1]
    tc = min(tc, t_len)
    kspec = pl.BlockSpec((tc, nk, LANES), lambda l, c: (c, 0, l))
    vspec = pl.BlockSpec((tc, nv, LANES), lambda l, c: (c, 0, l))
    sspec = pl.BlockSpec((nk, nv, LANES), lambda l, c: (0, 0, l))
    return pl.pallas_call(
        _wkv_kernel,
        grid=(lanes // LANES, t_len // tc),
        in_specs=[kspec] * 5 + [vspec, sspec],
        out_specs=[vspec, sspec],
        out_shape=[jax.ShapeDtypeStruct(v.shape, F32), jax.ShapeDtypeStruct(s0.shape, F32)],
        compiler_params=_cparams("parallel", "arbitrary"),
        name="wkv_scan",
    )(r, w, k, a, b, v, s0)


def _to_klanes(x, n_seq, seq_len, dup):
    h = x.shape[-1] // RW_HEAD
    y = x.reshape(n_seq, seq_len, h, RW_HEAD).transpose(1, 3, 0, 2).reshape(seq_len, RW_HEAD, n_seq * h)
    return jnp.concatenate([y] * dup, axis=-1) if dup > 1 else y


def _to_vlanes(x, n_seq, seq_len, dup):
    h = x.shape[-1] // RW_HEAD
    y = x.reshape(n_seq, seq_len, h, dup, RW_HEAD // dup).transpose(1, 4, 3, 0, 2)
    return y.reshape(seq_len, RW_HEAD // dup, dup * n_seq * h)


def _from_vlanes(y, n_seq, seq_len, dup):
    h = y.shape[-1] // (dup * n_seq)
    y = y.reshape(seq_len, RW_HEAD // dup, dup, n_seq, h).transpose(3, 0, 4, 2, 1)
    return y.reshape(n_seq * seq_len, h * RW_HEAD)


def _state_to_lanes(s, dup):
    n_seq, h = s.shape[:2]
    y = s.reshape(n_seq, h, dup, RW_HEAD // dup, RW_HEAD).transpose(4, 3, 2, 0, 1)
    return y.reshape(RW_HEAD, RW_HEAD // dup, dup * n_seq * h)


def _state_from_lanes(s, n_seq, dup):
    h = s.shape[-1] // (dup * n_seq)
    y = s.reshape(RW_HEAD, RW_HEAD // dup, dup, n_seq, h).transpose(3, 4, 2, 1, 0)
    return y.reshape(n_seq, h, RW_HEAD, RW_HEAD)


def _wkv(grp, r, w, k, v, a, b, s0):
    n_seq, seq_len = grp.n_seq, grp.seq_len
    heads = r.shape[-1] // RW_HEAD
    dup = max(1, LANES // (n_seq * heads))
    kl = [_to_klanes(t, n_seq, seq_len, dup) for t in (r, w, k, a, b)]
    y, s = _wkv_scan(*kl, _to_vlanes(v, n_seq, seq_len, dup), _state_to_lanes(s0, dup), tc=64)
    return _from_vlanes(y, n_seq, seq_len, dup), _state_from_lanes(s, n_seq, dup)


def _wkv_t_kernel(r_ref, w_ref, k_ref, a_ref, b_ref, v_ref, s0_ref, y_ref, s_ref, ks_sc, vs_sc, ys_sc):
    c = pl.program_id(0)
    n_seq, d, tc = r_ref.shape
    nk = RW_HEAD
    heads = d // nk
    dup = LANES // (n_seq * heads)
    nv = nk // dup

    @pl.when(c == 0)
    def _():
        s_ref[...] = s0_ref[...]

    def head_rows(ref, ch):
        return [ref[b, pl.ds(ch, heads, stride=nk), :] for b in range(n_seq)]

    def load_k(kk, carry):
        for n, ref in enumerate((r_ref, w_ref, k_ref, a_ref, b_ref)):
            ks_sc[n, kk] = jnp.concatenate(head_rows(ref, kk) * dup, axis=0).T
        return carry

    lax.fori_loop(0, nk, load_k, 0, unroll=4)

    def load_v(vr, carry):
        rows = []
        for part in range(dup):
            rows += head_rows(v_ref, part * nv + vr)
        vs_sc[pl.ds(vr, tc, stride=nv), :] = jnp.concatenate(rows, axis=0).T
        return carry

    lax.fori_loop(0, nv, load_v, 0, unroll=2)

    def tree(parts):
        return (parts[0] + parts[1]) + (parts[2] + parts[3])

    parts = [None] * 4
    for kk in range(nk):
        term = s_ref[kk] * ks_sc[3, kk, 0:1, :]
        parts[kk % 4] = term if parts[kk % 4] is None else parts[kk % 4] + term

    def step(t, sa):
        trow = pl.ds(t, 1)
        nrow = pl.ds(jnp.minimum(t + 1, tc - 1), 1)
        vt = vs_sc[pl.ds(pl.multiple_of(t * nv, nv), nv), :]
        ys = [None] * 4
        sn = [None] * 4
        for kk in range(nk):
            s_new = (s_ref[kk] * ks_sc[1, kk, trow, :] + sa * ks_sc[4, kk, trow, :]
                     + vt * ks_sc[2, kk, trow, :])
            s_ref[kk] = s_new
            ty = s_new * ks_sc[0, kk, trow, :]
            ts = s_new * ks_sc[3, kk, nrow, :]
            ys[kk % 4] = ty if ys[kk % 4] is None else ys[kk % 4] + ty
            sn[kk % 4] = ts if sn[kk % 4] is None else sn[kk % 4] + ts
        ys_sc[pl.ds(pl.multiple_of(t * nv, nv), nv), :] = tree(ys)
        return tree(sn)

    lax.fori_loop(0, tc, step, tree(parts))

    def store_y(vr, carry):
        yt = ys_sc[pl.ds(vr, tc, stride=nv), :].T
        for part in range(dup):
            for b in range(n_seq):
                r0 = (part * n_seq + b) * heads
                y_ref[b, pl.ds(part * nv + vr, heads, stride=nk), :] = yt[r0:r0 + heads]
        return carry

    lax.fori_loop(0, nv, store_y, 0, unroll=2)


WKV_CHUNK = 128


def _wkv_t(r, w, k, a, b, v, s0):
    n_seq, d, t_len = r.shape
    heads = d // RW_HEAD
    assert LANES % (n_seq * heads) == 0 and t_len % WKV_CHUNK == 0
    dup = LANES // (n_seq * heads)
    nv = RW_HEAD // dup
    tc = WKV_CHUNK
    one = pl.Buffered(1)
    xspec = pl.BlockSpec((n_seq, d, tc), lambda c: (0, 0, c), pipeline_mode=one)
    sspec = pl.BlockSpec((RW_HEAD, nv, LANES), lambda c: (0, 0, 0), pipeline_mode=one)
    y, s = pl.pallas_call(
        _wkv_t_kernel,
        grid=(t_len // tc,),
        in_specs=[xspec] * 6 + [sspec],
        out_specs=[pl.BlockSpec((n_seq, d, tc), lambda c: (0, 0, c)),
                   pl.BlockSpec((RW_HEAD, nv, LANES), lambda c: (0, 0, 0))],
        out_shape=[jax.ShapeDtypeStruct(r.shape, F32), jax.ShapeDtypeStruct((RW_HEAD, nv, LANES), F32)],
        scratch_shapes=[pltpu.VMEM((5, RW_HEAD, tc, LANES), F32), pltpu.VMEM((tc * nv, LANES), F32),
                        pltpu.VMEM((tc * nv, LANES), F32)],
        compiler_params=_cparams("arbitrary"),
        name="wkv_scan_t",
    )(r, w, k, a, b, v, _state_to_lanes(s0, dup))
    return y, _state_from_lanes(s, n_seq, dup)


def _rwkv_post_kernel(t_minor, x_ref, y_ref, bonus_ref, g_ref, gate_ref, vec_ref, ones_ref, wo_ref, o_ref):
    vec = vec_ref[...]
    lnx_w, lnx_b = vec[0:1, :], vec[1:2, :]
    y = y_ref[0].T if t_minor else y_ref[...]
    ones = ones_ref[...]
    inv = 1.0 / RW_HEAD
    mu = _seg_sum64(y, ones) * inv
    yc = y - mu
    var = _seg_sum64(yc * yc, ones) * inv
    yn = yc * lax.rsqrt(var + GN_EPS) * lnx_w + lnx_b
    out = jnp.dot(((yn + bonus_ref[...]) * g_ref[...]).astype(BF16), wo_ref[...], preferred_element_type=F32)
    o_ref[...] = x_ref[...] + gate_ref[0] * out


def _rwkv_post(grp, x, y, bonus, g, gate, P, j, t_minor):
    d = x.shape[-1]
    vec = _pad_to(jnp.stack([P['rw_lnx_w'][j], P['rw_lnx_b'][j]]), 0, 8)
    ones = _block_ones(MXU_DIM)
    wo = P['rw_w_o'][j].astype(BF16)
    tps = grp.tiles_per_seq
    yspec = (pl.BlockSpec((1, d, grp.tile), lambda i: (i // tps, 0, i % tps)) if t_minor
             else grp.row_spec(d))
    return pl.pallas_call(
        functools.partial(_rwkv_post_kernel, t_minor),
        grid=(grp.n_tiles,),
        in_specs=[grp.row_spec(d), yspec, grp.row_spec(d), grp.row_spec(d), grp.seq_spec(d),
                  _const_spec(vec.shape), _const_spec(ones.shape), _const_spec(wo.shape)],
        out_specs=grp.row_spec(d),
        out_shape=jax.ShapeDtypeStruct(x.shape, F32),
        compiler_params=_cparams("parallel"),
        name="rwkv_post",
    )(x, y, bonus, g, grp.seq_rows(gate), vec, ones, wo)


Q_W = ATT_HEADS * ATT_HEAD_DIM
KV_W = KV_HEADS * ATT_HEAD_DIM
IQ_W = IDX_HEADS * IDX_DIM
OFF_K = Q_W
OFF_V = Q_W + KV_W
OFF_QI = Q_W + 2 * KV_W
OFF_KI = OFF_QI + IQ_W


def _rope_tables(pos):
    rd = ATT_HEAD_DIM // ROT_FRAC
    half = rd // 2
    inv = ROPE_THETA ** (-jnp.arange(half, dtype=F32) * 2.0 / rd)
    ang = pos.astype(F32)[:, None] * inv[None, :]
    cos, sin = jnp.cos(ang), jnp.sin(ang)
    n = pos.shape[0]
    rest = ATT_HEAD_DIM - rd
    c = jnp.concatenate([cos, cos, jnp.ones((n, rest), F32)], axis=-1)
    s_up = jnp.concatenate([-sin, jnp.zeros((n, half + rest), F32)], axis=-1)
    s_dn = jnp.concatenate([jnp.zeros((n, half), F32), sin, jnp.zeros((n, rest), F32)], axis=-1)
    return [jnp.concatenate([t, t], axis=-1) for t in (c, s_up, s_dn)]


def _rope128(x, c, s_up, s_dn):
    half = ATT_HEAD_DIM // ROT_FRAC // 2
    return x * c + pltpu.roll(x, LANES - half, axis=1) * s_up + pltpu.roll(x, half, axis=1) * s_dn


def _dsa_proj_kernel(x_ref, g_ref, sc_ref, sh_ref, w_ref, c_ref, su_ref, sd_ref, qg_ref, kg_ref, ig_ref,
                     ones_ref, q_o, k_o, v_o, qi_o, kiwi_o):
    h = _rms_mod(x_ref[...], g_ref[...], sc_ref[0], sh_ref[0])
    z = jnp.dot(h.astype(BF16), w_ref[...], preferred_element_type=F32)
    c, s_up, s_dn = c_ref[...], su_ref[...], sd_ref[...]
    ones = ones_ref[...]
    inv = 1.0 / ATT_HEAD_DIM
    lane = lax.broadcasted_iota(I32, (z.shape[0], LANES), 1)
    low = lane < ATT_HEAD_DIM

    def head_norm(t, gain, blk=ones):
        return t * lax.rsqrt(_seg_sum64(t * t, blk) * inv + NORM_EPS) * gain

    def pairs(t):
        return [t[:, n * LANES:(n + 1) * LANES] for n in range(t.shape[-1] // LANES)]

    q = head_norm(z[:, :Q_W], qg_ref[...]) * (LOG2E * ATT_HEAD_DIM ** -0.5)
    for n, t in enumerate(pairs(q)):
        t = _rope128(t, c, s_up, s_dn).astype(BF16)
        q_o[0, 2 * n] = t[:, :ATT_HEAD_DIM]
        q_o[0, 2 * n + 1] = t[:, ATT_HEAD_DIM:]
    k = head_norm(z[:, OFF_K:OFF_V], kg_ref[...])
    k_o[...] = jnp.concatenate([_rope128(t, c, s_up, s_dn) for t in pairs(k)], axis=-1)
    v_o[...] = z[:, OFF_V:OFF_QI]
    qi = z[:, OFF_QI:OFF_KI] * (IDX_DIM ** -0.5)
    for n, t in enumerate(pairs(qi)):
        t = _rope128(t, c, s_up, s_dn)
        qi_o[0, 2 * n] = jnp.where(low, t, 0.0).astype(BF16)
        qi_o[0, 2 * n + 1] = jnp.where(low, pltpu.roll(t, ATT_HEAD_DIM, axis=1), 0.0).astype(BF16)
    kw = z[:, OFF_KI:OFF_KI + LANES]
    ki = _rope128(head_norm(kw, ig_ref[...], ones[:LANES, :LANES]), c, s_up, s_dn)
    kiwi_o[...] = jnp.where(low, ki, kw * (IDX_HEADS ** -0.5))


def _dsa_proj(grp, x, g, sc, sh, pos, P, j):
    d = x.shape[-1]
    w = _pad_to(P['att_w_in'][j], 1, LANES).astype(BF16)
    tabs = [grp.pos_rows(t) for t in _rope_tables(pos)]
    qg = jnp.tile(P['att_q_norm'][j], ATT_HEADS).reshape(1, Q_W)
    kg = jnp.tile(P['att_k_norm'][j], KV_HEADS).reshape(1, KV_W)
    ig = _pad_to(P['idx_k_norm'][j], 0, LANES).reshape(1, LANES)
    ones = _block_ones(MXU_DIM)
    nsb = grp.n_seq if grp.per_seq else grp.n_tiles
    rows = grp.seq_len if grp.per_seq else grp.tile
    tps = grp.tiles_per_seq
    tt = grp.tile

    def head_spec(nh, wd):
        return pl.BlockSpec((1, nh, tt, wd), lambda i: (i // tps, 0, i % tps, 0))

    return pl.pallas_call(
        _dsa_proj_kernel,
        grid=(grp.n_tiles,),
        in_specs=[grp.row_spec(d), _const_spec((1, d)), grp.seq_spec(d), grp.seq_spec(d), _const_spec(w.shape),
                  grp.pos_spec(LANES), grp.pos_spec(LANES), grp.pos_spec(LANES),
                  _const_spec(qg.shape), _const_spec(kg.shape), _const_spec(ig.shape), _const_spec(ones.shape)],
        out_specs=[head_spec(ATT_HEADS, ATT_HEAD_DIM), grp.row_spec(KV_W), grp.row_spec(KV_W),
                   head_spec(IDX_HEADS, LANES), grp.row_spec(LANES)],
        out_shape=[jax.ShapeDtypeStruct((nsb, ATT_HEADS, rows, ATT_HEAD_DIM), BF16),
                   jax.ShapeDtypeStruct((grp.rows, KV_W), F32),
                   jax.ShapeDtypeStruct((grp.rows, KV_W), F32),
                   jax.ShapeDtypeStruct((nsb, IDX_HEADS, rows, LANES), BF16),
                   jax.ShapeDtypeStruct((grp.rows, LANES), F32)],
        compiler_params=_cparams("parallel"),
        name="dsa_proj",
    )(x, g, grp.seq_rows(sc), grp.seq_rows(sh), w, *tabs, qg, kg, ig, ones)


def _score_key(score, admissible):
    bits = pltpu.bitcast(score + 0.0, I32)
    key = jnp.where(bits >= 0, bits, bits ^ 0x7FFFFFFF)
    return jnp.where(admissible, key, NEG_INF_KEY)


def _kth_largest(count_ge, shape, kth, bits):
    lowest = -(2 ** (bits - 1))
    theta = jnp.where(count_ge(jnp.zeros(shape, I32)) >= kth, jnp.zeros(shape, I32), jnp.full(shape, lowest, I32))

    def value_bit(n, theta):
        cand = theta + jnp.left_shift(jnp.int32(1), bits - 2 - n)
        return jnp.where(count_ge(cand) >= kth, cand, theta)

    return lax.fori_loop(0, bits - 1, value_bit, theta)


def _kth_largest_halves(count_hi, set_bucket, count_lo, shape, topk):
    kf = float(topk)
    th_hi = _kth_largest(lambda c: count_hi(lambda k: k >= c.astype(I16)), shape, kf, 16)
    hi16 = th_hi.astype(I16)
    rest = kf - count_hi(lambda k: k > hi16)
    set_bucket(hi16)
    th_lo = _kth_largest(lambda c: count_lo(lambda k: k >= c.astype(I16)), shape, rest, 16)
    return th_hi * 65536 + (th_lo + 32768)


def _topk_threshold(count, shape, topk, idx_bits, theta=None):
    kf = float(topk)
    if theta is None:
        theta = _kth_largest(lambda c: count(lambda k, i: k >= c), shape, kf, 32)
    need = kf - count(lambda k, i: k > theta)

    def index_bit(n, cut):
        cand = cut + jnp.left_shift(jnp.int32(1), idx_bits - 1 - n)
        below = count(lambda k, i: jnp.where(k == theta, i, cand) < cand)
        return jnp.where(below < need, cand, cut)

    surplus = count(lambda k, i: k == theta) - need
    tied = jnp.where(theta > NEG_INF_KEY, surplus, 0.0)
    cut = lax.cond(jnp.max(tied) > 0.0,
                   lambda: lax.fori_loop(0, idx_bits, index_bit, jnp.zeros(shape, I32)),
                   lambda: jnp.full(shape, 2 ** 30, I32))
    return theta, cut


def _select_bias(key, idx, theta, cut):
    tie = jnp.where(idx <= cut, 0.0, MASK_BIAS)
    bias = jnp.where(key > theta, 0.0, jnp.where(key == theta, tie, MASK_BIAS))
    return jnp.where(key > NEG_INF_KEY, bias, MASK_BIAS)


ATT_ROW_BLOCK = 32


def _dsa_prompt_kernel(topk, q_ref, qi_ref, kiwiq_ref, x_ref, gate_ref, k_ref, v_ref, kiwi_ref, place_ref,
                       wo_ref, o_ref, key_sc, hi_sc, lo_sc, bucket_sc, bias_sc, qblk_sc, kb_sc, vb_sc, s_sc, p_sc,
                       m_sc, l_sc, alpha_sc, acc_sc):
    qb = pl.program_id(1)
    tq = x_ref.shape[0]
    assert tq == LANES
    n_chunks, kc, _ = key_sc.shape
    lt = kc // LANES
    nkc = (qb * tq + tq - 1) // kc + 1
    qpos = qb * tq + lax.broadcasted_iota(I32, (kc, LANES), 1)
    krow = lax.broadcasted_iota(I32, (kc, LANES), 0)

    qi = qi_ref[0].reshape(IDX_HEADS * tq, LANES)
    wi_t = kiwiq_ref[...].T

    def score_chunk(c, carry):
        kic = kiwi_ref[pl.ds(pl.multiple_of(c * kc, kc), kc), :]
        logits = _bdot_nt(kic, qi)
        acc = None
        for h in range(IDX_HEADS):
            t = jnp.maximum(logits[:, h * tq:(h + 1) * tq], 0.0) * wi_t[IDX_DIM + h:IDX_DIM + h + 1, :]
            acc = t if acc is None else acc + t
        key = _score_key(acc, c * kc + krow <= qpos)
        key_sc[c] = key
        hi_sc[c] = (key >> 16).astype(I16)
        lo_sc[c] = (key ^ 0x8000).astype(I16)
        return carry

    lax.fori_loop(0, nkc, score_chunk, 0)

    def count(pred):
        def body(c, acc):
            w = jnp.where(pred(key_sc[c], c * kc + krow), 1.0, 0.0)
            for j in range(lt):
                acc = acc + w[j * LANES:(j + 1) * LANES, :]
            return acc
        acc = lax.fori_loop(0, nkc, body, jnp.zeros((LANES, LANES), F32))
        return jnp.sum(acc, axis=0, keepdims=True)

    def count16(ref):
        def run(pred):
            def body(c, acc):
                w = jnp.where(pred(ref[c]), jnp.int16(1), jnp.int16(0))
                for j in range(lt):
                    acc = acc + w[j * LANES:(j + 1) * LANES, :]
                return acc
            acc = lax.fori_loop(0, nkc, body, jnp.zeros((LANES, LANES), I16))
            return jnp.sum(acc.astype(F32), axis=0, keepdims=True)
        return run

    def set_bucket(hi16):
        def body(c, carry):
            bucket_sc[c] = jnp.where(hi_sc[c] == hi16, lo_sc[c], jnp.int16(-(2 ** 15)))
            return carry
        lax.fori_loop(0, nkc, body, 0)

    theta = _kth_largest_halves(count16(hi_sc), set_bucket, count16(bucket_sc), (1, LANES), topk)
    theta, cut = _topk_threshold(count, (1, LANES), topk, (n_chunks * kc - 1).bit_length(), theta)

    def bias_chunk(c, carry):
        bias_t = _select_bias(key_sc[c], c * kc + krow, theta, cut)
        bias_sc[c] = jnp.concatenate([bias_t[j * LANES:(j + 1) * LANES, :].T for j in range(lt)], axis=-1)
        return carry

    lax.fori_loop(0, nkc, bias_chunk, 0)

    gq = GROUPS * tq
    for g in range(KV_HEADS):
        qg = q_ref[0, g * GROUPS:(g + 1) * GROUPS].reshape(gq, ATT_HEAD_DIM)
        qblk_sc[g] = jnp.dot(qg, place_ref[g], preferred_element_type=F32).astype(BF16)
    m_sc[...] = jnp.full_like(m_sc, MASK_BIAS)
    l_sc[...] = jnp.zeros_like(l_sc)
    acc_sc[...] = jnp.zeros_like(acc_sc)

    def att_chunk(c, carry):
        rows = pl.ds(pl.multiple_of(c * kc, kc), kc)
        kb_sc[...] = k_ref[rows, :].astype(BF16)
        vb_sc[...] = v_ref[rows, :].astype(BF16)

        for g in range(KV_HEADS):
            buf = g % 2
            base = g * gq
            s_sc[buf] = _bdot_nt(qblk_sc[g], kb_sc[...])
            for r0 in range(0, gq, ATT_ROW_BLOCK):
                t0 = r0 % tq
                rb = slice(r0, r0 + ATT_ROW_BLOCK)
                ms = slice(base + r0, base + r0 + ATT_ROW_BLOCK)
                mx = None
                for j in range(lt):
                    cols = slice(j * LANES, (j + 1) * LANES)
                    t = s_sc[buf, rb, cols] + bias_sc[c, t0:t0 + ATT_ROW_BLOCK, cols]
                    s_sc[buf, rb, cols] = t
                    mx = t if mx is None else jnp.maximum(mx, t)
                m_old = m_sc[ms, :]
                m_new = jnp.maximum(m_old, jnp.max(mx, axis=1, keepdims=True))
                alpha_sc[buf, rb, :] = jnp.exp2(m_old - m_new)
                m_sc[ms, :] = m_new
            for r0 in range(0, gq, ATT_ROW_BLOCK):
                rb = slice(r0, r0 + ATT_ROW_BLOCK)
                ms = slice(base + r0, base + r0 + ATT_ROW_BLOCK)
                m_new = m_sc[ms, :]
                tot = None
                for j in range(lt):
                    cols = slice(j * LANES, (j + 1) * LANES)
                    p = jnp.exp2(s_sc[buf, rb, cols] - m_new)
                    p_sc[buf, rb, cols] = p.astype(BF16)
                    tot = p if tot is None else tot + p
                l_sc[ms, :] = alpha_sc[buf, rb, :] * l_sc[ms, :] + jnp.sum(tot, axis=1, keepdims=True)
            a = alpha_sc[buf]
            rs = slice(base, base + gq)
            acc_sc[rs, :] = (acc_sc[rs, :] * jnp.concatenate([a] * (KV_W // LANES), axis=-1)
                             + jnp.dot(p_sc[buf], vb_sc[...], preferred_element_type=F32))
        return carry

    lax.fori_loop(0, nkc, att_chunk, 0)
    out = None
    for head in range(ATT_HEADS):
        rs = slice(head * tq, (head + 1) * tq)
        inv_l = 1.0 / l_sc[rs, :]
        o = acc_sc[rs, :] * jnp.concatenate([inv_l] * (KV_W // LANES), axis=-1)
        t = jnp.dot(o.astype(BF16), wo_ref[head], preferred_element_type=F32)
        out = t if out is None else out + t
    o_ref[...] = x_ref[...] + gate_ref[0] * out


def _head_placement():
    p = np.zeros((KV_HEADS, ATT_HEAD_DIM, KV_W), np.float32)
    for g in range(KV_HEADS):
        p[g, np.arange(ATT_HEAD_DIM), g * ATT_HEAD_DIM + np.arange(ATT_HEAD_DIM)] = 1.0
    return jnp.asarray(p, dtype=BF16)


def _dsa_prompt(n_seq, seq_len, x, gate, q, k, v, qi, kiwi, w_o):
    d = x.shape[-1]
    tq, kc = 128, 512
    assert seq_len % kc == 0
    topk = min(TOPK_MAX, seq_len // 4)
    nq = seq_len // tq
    gq = GROUPS * tq
    wo_h = w_o.astype(BF16).reshape(KV_HEADS, GROUPS, 1, ATT_HEAD_DIM, d)
    sel = jnp.eye(KV_HEADS, dtype=BF16).reshape(KV_HEADS, 1, KV_HEADS, 1, 1)
    wo = (wo_h * sel).reshape(ATT_HEADS, KV_W, d)
    place = _head_placement()
    return pl.pallas_call(
        functools.partial(_dsa_prompt_kernel, topk),
        grid=(n_seq, nq),
        in_specs=[pl.BlockSpec((1, ATT_HEADS, tq, ATT_HEAD_DIM), lambda b, i: (b, 0, i, 0)),
                  pl.BlockSpec((1, IDX_HEADS, tq, LANES), lambda b, i: (b, 0, i, 0)),
                  pl.BlockSpec((tq, LANES), lambda b, i: (b * nq + i, 0)),
                  pl.BlockSpec((tq, d), lambda b, i: (b * nq + i, 0)),
                  pl.BlockSpec((1, 1, d), lambda b, i: (b, 0, 0)),
                  pl.BlockSpec((seq_len, KV_W), lambda b, i: (b, 0)),
                  pl.BlockSpec((seq_len, KV_W), lambda b, i: (b, 0)),
                  pl.BlockSpec((seq_len, LANES), lambda b, i: (b, 0)),
                  _const_spec(place.shape), _const_spec(wo.shape)],
        out_specs=pl.BlockSpec((tq, d), lambda b, i: (b * nq + i, 0)),
        out_shape=jax.ShapeDtypeStruct(x.shape, F32),
        scratch_shapes=[pltpu.VMEM((seq_len // kc, kc, tq), I32)] + [pltpu.VMEM((seq_len // kc, kc, tq), I16)] * 3
        + [pltpu.VMEM((seq_len // kc, tq, kc), F32),
                        pltpu.VMEM((KV_HEADS, gq, KV_W), BF16), pltpu.VMEM((kc, KV_W), BF16),
                        pltpu.VMEM((kc, KV_W), BF16), pltpu.VMEM((2, gq, kc), F32),
                        pltpu.VMEM((2, gq, kc), BF16), pltpu.VMEM((ATT_HEADS * tq, LANES), F32),
                        pltpu.VMEM((ATT_HEADS * tq, LANES), F32), pltpu.VMEM((2, gq, LANES), F32),
                        pltpu.VMEM((ATT_HEADS * tq, KV_W), F32)],
        compiler_params=_cparams("parallel", "arbitrary"),
        name="dsa_prompt",
    )(q, qi, kiwi, x, gate.reshape(n_seq, 1, d), k, v, kiwi, place, wo)


PAGES_PER_STEP = 8


def _dsa_sel_kernel(topk, pt_ref, qi_ref, wib_ref, *refs):
    page_refs = refs[:PAGES_PER_STEP]
    kinew_ref, bias_ref, key_sc = refs[PAGES_PER_STEP:]
    p = pl.program_id(1)
    nsteps = pl.num_programs(1)
    npg = key_sc.shape[0] - 1
    t_new = bias_ref.shape[2]
    qi = qi_ref[0]
    wib = wib_ref[0]

    def score(keys):
        t = jnp.maximum(_bdot(qi, keys), 0.0) * wib
        acc = t[0:t_new]
        for h in range(1, IDX_HEADS):
            acc = acc + t[h * t_new:(h + 1) * t_new]
        return acc

    for n, page_ref in enumerate(page_refs):
        key_sc[p * PAGES_PER_STEP + n] = _score_key(score(page_ref[0, 0]), True)

    @pl.when(p == nsteps - 1)
    def _():
        lane = lax.broadcasted_iota(I32, (t_new, LANES), 1)
        row = lax.broadcasted_iota(I32, (t_new, LANES), 0)
        key_sc[npg] = _score_key(score(kinew_ref[0]), lane <= row)
        keys = key_sc[...]
        idx = (lax.broadcasted_iota(I32, keys.shape, 0) * LANES + lax.broadcasted_iota(I32, keys.shape, 2))

        def count(pred):
            c = jnp.sum(jnp.where(pred(keys, idx), 1.0, 0.0), axis=0)
            return jnp.sum(c, axis=1, keepdims=True)

        theta, cut = _topk_threshold(count, (t_new, LANES), topk, (keys.shape[0] * LANES - 1).bit_length())
        bias_ref[0] = _select_bias(keys, idx, theta, cut)


def _page_spec(width, n, layer):
    return pl.BlockSpec((1, 1, width, PAGE),
                        lambda b, p, pt: (layer, pt[b, p * PAGES_PER_STEP + n], 0, 0))


def _dsa_sel(page_table, qi, wib, cache_ki, layer, kinew, t_new):
    n_seq, npg = page_table.shape
    assert npg % PAGES_PER_STEP == 0
    topk = min(TOPK_MAX, (npg * PAGE + t_new) // 4)
    rows = qi.shape[1]
    grid_spec = pltpu.PrefetchScalarGridSpec(
        num_scalar_prefetch=1,
        grid=(n_seq, npg // PAGES_PER_STEP),
        in_specs=[pl.BlockSpec((1, rows, IDX_DIM), lambda b, p, pt: (b, 0, 0)),
                  pl.BlockSpec((1, rows, LANES), lambda b, p, pt: (b, 0, 0))]
        + [_page_spec(IDX_DIM, n, layer) for n in range(PAGES_PER_STEP)]
        + [pl.BlockSpec((1, IDX_DIM, PAGE), lambda b, p, pt: (b, 0, 0))],
        out_specs=pl.BlockSpec((1, npg + 1, t_new, LANES), lambda b, p, pt: (b, 0, 0, 0)),
        scratch_shapes=[pltpu.VMEM((npg + 1, t_new, LANES), I32)])
    return pl.pallas_call(
        functools.partial(_dsa_sel_kernel, topk),
        grid_spec=grid_spec,
        out_shape=jax.ShapeDtypeStruct((n_seq, npg + 1, t_new, LANES), F32),
        compiler_params=_cparams("parallel", "arbitrary"),
        name="dsa_sel",
    )(page_table, qi, wib, *([cache_ki] * PAGES_PER_STEP), kinew)


def _dsa_att_kernel(pt_ref, q_ref, *refs):
    kpage_refs = refs[:PAGES_PER_STEP]
    vpage_refs = refs[PAGES_PER_STEP:2 * PAGES_PER_STEP]
    knew_ref, vnew_ref, bias_ref, biasnew_ref, o_ref, m_sc, l_sc, acc_sc = refs[2 * PAGES_PER_STEP:]
    p = pl.program_id(1)
    last = pl.num_programs(1) - 1
    rows = q_ref.shape[1]
    t_new = bias_ref.shape[2]

    @pl.when(p == 0)
    def _():
        m_sc[...] = jnp.full_like(m_sc, MASK_BIAS)
        l_sc[...] = jnp.zeros_like(l_sc)
        acc_sc[...] = jnp.zeros_like(acc_sc)

    def attend(kv_bias):
        q = q_ref[0]
        ss = []
        for kk, _, bias in kv_bias:
            s = _bdot(q, kk)
            ss.append((s.reshape(rows // t_new, t_new, LANES) + bias[None]).reshape(rows, LANES))
        mx = ss[0]
        for s in ss[1:]:
            mx = jnp.maximum(mx, s)
        m_old = m_sc[...]
        m_new = jnp.maximum(m_old, jnp.max(mx, axis=1, keepdims=True))
        alpha = jnp.exp2(m_old - m_new)
        tot = None
        pv = None
        for s, (_, vv, _) in zip(ss, kv_bias):
            pr = jnp.exp2(s - m_new)
            tot = pr if tot is None else tot + pr
            t = _bdot_nt(pr, vv)
            pv = t if pv is None else pv + t
        l_sc[...] = alpha * l_sc[...] + jnp.sum(tot, axis=1, keepdims=True)
        acc_sc[...] = alpha * acc_sc[...] + pv
        m_sc[...] = m_new

    attend([(kpage_refs[n][0, 0], vpage_refs[n][0, 0], bias_ref[0, n]) for n in range(PAGES_PER_STEP)])

    @pl.when(p == last)
    def _():
        attend([(knew_ref[0], vnew_ref[0], biasnew_ref[0, 0])])
        o_ref[0] = acc_sc[...] / l_sc[...]


def _dsa_att(page_table, qblk, cache_k, cache_v, layer, knew, vnew, bias):
    n_seq, npg = page_table.shape
    rows = qblk.shape[1]
    t_new = bias.shape[2]
    grid_spec = pltpu.PrefetchScalarGridSpec(
        num_scalar_prefetch=1,
        grid=(n_seq, npg // PAGES_PER_STEP),
        in_specs=[pl.BlockSpec((1, rows, KV_W), lambda b, p, pt: (b, 0, 0))]
        + [_page_spec(KV_W, n, layer) for n in range(PAGES_PER_STEP)] * 2
        + [pl.BlockSpec((1, KV_W, PAGE), lambda b, p, pt: (b, 0, 0)),
           pl.BlockSpec((1, KV_W, PAGE), lambda b, p, pt: (b, 0, 0)),
           pl.BlockSpec((1, PAGES_PER_STEP, t_new, LANES), lambda b, p, pt: (b, p, 0, 0)),
           pl.BlockSpec((1, 1, t_new, LANES), lambda b, p, pt: (b, npg, 0, 0))],
        out_specs=pl.BlockSpec((1, rows, KV_W), lambda b, p, pt: (b, 0, 0)),
        scratch_shapes=[pltpu.VMEM((rows, 1), F32), pltpu.VMEM((rows, 1), F32), pltpu.VMEM((rows, KV_W), F32)])
    return pl.pallas_call(
        _dsa_att_kernel,
        grid_spec=grid_spec,
        out_shape=jax.ShapeDtypeStruct((n_seq, rows, KV_W), F32),
        compiler_params=_cparams("parallel", "arbitrary"),
        name="dsa_att",
    )(page_table, qblk, *([cache_k] * PAGES_PER_STEP), *([cache_v] * PAGES_PER_STEP), knew, vnew, bias, bias)


def _linear_res_kernel(x_ref, a_ref, gate_ref, w_ref, o_ref):
    o_ref[...] = x_ref[...] + gate_ref[0] * _bdot(a_ref[...], w_ref[...])


def _linear_res(grp, x, a, gate, w):
    d = x.shape[-1]
    wb = w.astype(BF16)
    return pl.pallas_call(
        _linear_res_kernel,
        grid=(grp.n_tiles,),
        in_specs=[grp.row_spec(d), grp.row_spec(a.shape[-1]), grp.seq_spec(d), _const_spec(wb.shape)],
        out_specs=grp.row_spec(d),
        out_shape=jax.ShapeDtypeStruct(x.shape, F32),
        compiler_params=_cparams("parallel"),
        name="linear_res",
    )(x, a, grp.seq_rows(gate), wb)


def _dsa_sample(grp, x, gate, q, k, v, qi, kiwi, cache_k, cache_v, cache_ki, layer, page_table, w_o):
    n_seq, t_new = grp.n_seq, grp.seq_len
    assert t_new <= PAGE and grp.n_tiles == 1

    def new_page(t):
        return _pad_to(t.reshape(n_seq, t_new, t.shape[-1]), 1, PAGE).transpose(0, 2, 1)

    qi_b = qi[0, :, :, :IDX_DIM].reshape(IDX_HEADS, n_seq, t_new, IDX_DIM).transpose(1, 0, 2, 3)
    qi_b = qi_b.reshape(n_seq, IDX_HEADS * t_new, IDX_DIM)
    wi = kiwi[:, IDX_DIM:IDX_DIM + IDX_HEADS].reshape(n_seq, t_new, IDX_HEADS).transpose(0, 2, 1)
    wib = jnp.broadcast_to(wi.reshape(n_seq, IDX_HEADS * t_new, 1), (n_seq, IDX_HEADS * t_new, LANES))
    bias = _dsa_sel(page_table, qi_b, wib, cache_ki, layer, new_page(kiwi[:, :IDX_DIM]), t_new)
    q_b = q[0].reshape(KV_HEADS, GROUPS, n_seq, t_new, ATT_HEAD_DIM).transpose(2, 0, 1, 3, 4)
    eye = jnp.eye(KV_HEADS, dtype=q_b.dtype)
    qblk = (q_b[:, :, :, :, None, :] * eye[None, :, None, None, :, None])
    qblk = qblk.reshape(n_seq, ATT_HEADS * t_new, KV_W)
    o = _dsa_att(page_table, qblk, cache_k, cache_v, layer, new_page(k), new_page(v), bias)
    o = o.reshape(n_seq, KV_HEADS, GROUPS, t_new, KV_HEADS, ATT_HEAD_DIM)
    o = jnp.stack([o[:, g, :, :, g, :] for g in range(KV_HEADS)], axis=1)
    o = o.transpose(0, 3, 1, 2, 4).reshape(n_seq * t_new, Q_W)
    return _linear_res(grp, x, o, gate, w_o)


TOKEN_TILE = 256
FFN_TILE = 1024


def _trunk(x, mods, pos, wkv0, shift0, attn_fn, P):
    n_seq, seq_len, d = x.shape
    depth = mods.shape[0]
    grp = _Group(n_seq, seq_len, TOKEN_TILE)
    grp_ffn = _Group(n_seq, seq_len, FFN_TILE)
    x = x.reshape(n_seq * seq_len, d)
    v_first = None
    ks, vs, kis, wkvs, shifts = [], [], [], [], []
    for i in range(depth):
        j = i // 2
        sh_a, sc_a, g_a, sh_f, sc_f, g_f = jnp.split(mods[i], 6, axis=-1)
        g_att = P['norm_g'][i, 0].reshape(1, d)
        g_ffn = P['norm_g'][i, 1].reshape(1, d)
        if i % 2 == 0:
            t_minor = grp.per_seq and seq_len % WKV_CHUNK == 0 and LANES % (n_seq * (d // RW_HEAD)) == 0
            outs = _rwkv_proj(grp, x, g_att, sc_a, sh_a, shift0[j], P, j, v_first, t_minor)
            v, gate, bonus, h = outs[-4:]
            if j == 0:
                v_first = v
            if t_minor:
                y, state = _wkv_t(*outs[:6], wkv0[j])
            else:
                r, w, k, a, b = outs[:5]
                y, state = _wkv(grp, r, w, k, v, a, b, wkv0[j])
            x = _rwkv_post(grp, x, y, bonus, gate, g_a, P, j, t_minor)
            wkvs.append(state)
            shifts.append(h.reshape(n_seq, seq_len, d)[:, -1])
        else:
            q, k, v, qi, kiwi = _dsa_proj(grp, x, g_att, sc_a, sh_a, pos, P, j)
            x = attn_fn(j, grp, x, g_a, q, k, v, qi, kiwi)
            ks.append(k.reshape(n_seq, seq_len, KV_HEADS, ATT_HEAD_DIM))
            vs.append(v.reshape(n_seq, seq_len, KV_HEADS, ATT_HEAD_DIM))
            kis.append(kiwi[:, :IDX_DIM].reshape(n_seq, seq_len, IDX_DIM))
        x = _ffn(grp_ffn, x, g_ffn, sc_f, sh_f, g_f, P['w_up'][i], P['w_down'][i])
    return (x.reshape(n_seq, seq_len, d), jnp.stack(ks), jnp.stack(vs), jnp.stack(kis),
            jnp.stack(wkvs), jnp.stack(shifts))


def kernel(x_prompt, x_sample, cache_k, cache_v, cache_idx_k, state_wkv, state_shift, page_table,
           c_prompt, c_sample, norm_g, w_ada, b_ada, w_up, w_down, rw_mix, rw_w_rkv, rw_w_o,
           rw_w0, rw_w1, rw_w2, rw_a0, rw_a1, rw_a2, rw_v0, rw_v1, rw_v2, rw_g1, rw_g2,
           rw_k_k, rw_k_a, rw_r_k, rw_lnx_w, rw_lnx_b, att_w_in, att_w_o, att_q_norm,
           att_k_norm, idx_k_norm):
    P = dict(norm_g=norm_g, w_up=w_up.astype(BF16), w_down=w_down.astype(BF16),
             rw_mix=rw_mix, rw_w_rkv=rw_w_rkv, rw_w_o=rw_w_o, rw_w0=rw_w0, rw_w1=rw_w1,
             rw_w2=rw_w2, rw_a0=rw_a0, rw_a1=rw_a1, rw_a2=rw_a2, rw_v0=rw_v0, rw_v1=rw_v1,
             rw_v2=rw_v2, rw_g1=rw_g1, rw_g2=rw_g2, rw_k_k=rw_k_k, rw_k_a=rw_k_a, rw_r_k=rw_r_k,
             rw_lnx_w=rw_lnx_w, rw_lnx_b=rw_lnx_b, att_w_in=att_w_in, att_w_o=att_w_o,
             att_q_norm=att_q_norm, att_k_norm=att_k_norm, idx_k_norm=idx_k_norm)
    n_p, seq, d = x_prompt.shape
    n_s, dec_seq, _ = x_sample.shape
    n_rwkv = state_wkv.shape[0]
    heads = d // RW_HEAD
    past_len = page_table.shape[1] * PAGE

    mods = _ada(jnp.concatenate([c_prompt, c_sample], axis=0), w_ada, b_ada)
    n_dsa, pool = cache_k.shape[:2]
    pages_k = cache_k.transpose(0, 1, 3, 4, 2).reshape(n_dsa, pool, KV_W, PAGE)
    pages_v = cache_v.transpose(0, 1, 3, 4, 2).reshape(n_dsa, pool, KV_W, PAGE)
    pages_ki = cache_idx_k.transpose(0, 1, 3, 2)

    def prompt_attn(j, grp, x, gate, q, k, v, qi, kiwi):
        return _dsa_prompt(grp.n_seq, grp.seq_len, x, gate, q, k, v, qi, kiwi, att_w_o[j])

    def sample_attn(j, grp, x, gate, q, k, v, qi, kiwi):
        return _dsa_sample(grp, x, gate, q, k, v, qi, kiwi, pages_k, pages_v, pages_ki, j,
                           page_table, att_w_o[j])

    wkv0 = jnp.zeros((n_rwkv, n_p, heads, RW_HEAD, RW_HEAD), F32)
    shift0 = jnp.zeros((n_rwkv, n_p, d), F32)
    y_p, k_p, v_p, ki_p, wkv_p, shift_p = _trunk(
        x_prompt, mods[:, :n_p], jnp.arange(seq), wkv0, shift0, prompt_attn, P)
    y_s, k_s, v_s, ki_s, wkv_s, shift_s = _trunk(
        x_sample, mods[:, n_p:], past_len + jnp.arange(dec_seq), state_wkv, state_shift, sample_attn, P)
    return (y_p, y_s, k_p, v_p, ki_p, wkv_p, shift_p, k_s, v_s, ki_s, wkv_s, shift_s)
```

```python
import functools

import jax
import jax.numpy as jnp
import numpy as np
from jax import lax
from jax.experimental import pallas as pl
from jax.experimental.pallas import tpu as pltpu

F32 = jnp.float32
BF16 = jnp.bfloat16
I32 = jnp.int32

NORM_EPS = 1e-6
GN_EPS = 64e-5
RW_HEAD = 64
ATT_HEADS = 16
ATT_HEAD_DIM = 64
KV_HEADS = 4
GROUPS = ATT_HEADS // KV_HEADS
IDX_HEADS = 8
IDX_DIM = 64
TOPK_MAX = 256
ROPE_THETA = 500000.0
ROT_FRAC = 4
PAGE = 128

LANES = 128
MXU_DIM = 256
VMEM_LIMIT = 56 * 1024 * 1024

LOG2E = 1.4426950408889634
INT_MIN = -(2 ** 31)
NEG_INF_KEY = INT_MIN + 0x7FFFFF
MASK_BIAS = -1e30


def _cparams(*sem):
    return pltpu.CompilerParams(dimension_semantics=sem, vmem_limit_bytes=VMEM_LIMIT)


def _const_spec(shape):
    n = len(shape)
    return pl.BlockSpec(shape, lambda *_: (0,) * n, pipeline_mode=pl.Buffered(1))


def _bdot(a, b):
    return jnp.dot(a.astype(BF16), b.astype(BF16), preferred_element_type=F32)


def _bdot_nt(a, b):
    return lax.dot_general(a.astype(BF16), b.astype(BF16), (((1,), (1,)), ((), ())),
                           preferred_element_type=F32)


def _rms_mod(x, g, sc, sh):
    ms = jnp.mean(x * x, axis=-1, keepdims=True)
    return x * lax.rsqrt(ms + NORM_EPS) * g * (1.0 + sc) + sh


def _seg_sum64(x, ones):
    hi = x.astype(BF16)
    lo = (x - hi.astype(F32)).astype(BF16)
    w = ones.shape[0]
    outs = []
    for c in range(x.shape[-1] // w):
        sl = slice(c * w, (c + 1) * w)
        outs.append(jnp.dot(hi[:, sl], ones, preferred_element_type=F32)
                    + jnp.dot(lo[:, sl], ones, preferred_element_type=F32))
    return outs[0] if len(outs) == 1 else jnp.concatenate(outs, axis=-1)


def _block_ones(width):
    i = np.arange(width) // 64
    return jnp.asarray((i[:, None] == i[None, :]).astype(np.float32), dtype=BF16)


def _ada_kernel(c_ref, w_ref, b_ref, o_ref):
    c = c_ref[...]
    s = c * jax.nn.sigmoid(c)
    o_ref[0] = _bdot(s, w_ref[0]) + b_ref[0]


def _ada(c_all, w_ada, b_ada):
    depth, d, d6 = w_ada.shape
    nb = c_all.shape[0]
    tn = 1536
    return pl.pallas_call(
        _ada_kernel,
        grid=(depth, d6 // tn),
        in_specs=[pl.BlockSpec((nb, d), lambda i, j: (0, 0)),
                  pl.BlockSpec((1, d, tn), lambda i, j: (i, 0, j)),
                  pl.BlockSpec((1, 1, tn), lambda i, j: (i, 0, j))],
        out_specs=pl.BlockSpec((1, nb, tn), lambda i, j: (i, 0, j)),
        out_shape=jax.ShapeDtypeStruct((depth, nb, d6), F32),
        compiler_params=_cparams("parallel", "parallel"),
        name="ada",
    )(c_all, w_ada, b_ada.reshape(depth, 1, d6))


class _Group:
    def __init__(self, n_seq, seq_len, tile):
        self.n_seq, self.seq_len = n_seq, seq_len
        self.rows = n_seq * seq_len
        self.tile = min(tile, self.rows)
        assert self.rows % self.tile == 0
        assert seq_len % self.tile == 0 or self.tile % seq_len == 0
        self.tiles_per_seq = max(seq_len // self.tile, 1)
        self.per_seq = seq_len >= self.tile
        self.n_tiles = self.rows // self.tile

    def seq_rows(self, m):
        d = m.shape[-1]
        if self.per_seq:
            return m.reshape(self.n_seq, 1, d)
        return jnp.repeat(m, self.seq_len, axis=0).reshape(self.n_tiles, self.tile, d)

    def seq_spec(self, d):
        r = 1 if self.per_seq else self.tile
        tps = self.tiles_per_seq
        return pl.BlockSpec((1, r, d), lambda i, *_: (i // tps, 0, 0))

    def pos_rows(self, tab):
        if self.per_seq:
            return tab
        return jnp.tile(tab, (self.tile // self.seq_len, 1))

    def pos_spec(self, w):
        tps = self.tiles_per_seq
        return pl.BlockSpec((self.tile, w), lambda i, *_: (i % tps, 0))

    def row_spec(self, w):
        return pl.BlockSpec((self.tile, w), lambda i, *_: (i, 0))


def _ffn_kernel(x_ref, g_ref, sc_ref, sh_ref, gate_ref, wu_ref, wd_ref, o_ref, h_sc, acc_sc):
    j = pl.program_id(1)

    @pl.when(j == 0)
    def _():
        h_sc[...] = _rms_mod(x_ref[...], g_ref[...], sc_ref[0], sh_ref[0]).astype(BF16)
        acc_sc[...] = jnp.zeros_like(acc_sc)

    u = jnp.maximum(jnp.dot(h_sc[...], wu_ref[...], preferred_element_type=F32), 0.0)
    acc_sc[...] += jnp.dot((u * u).astype(BF16), wd_ref[...], preferred_element_type=F32)

    @pl.when(j == pl.num_programs(1) - 1)
    def _():
        o_ref[...] = x_ref[...] + gate_ref[0] * acc_sc[...]


def _ffn(grp, x, g, sc, sh, gate, w_up, w_down):
    d, dff = w_up.shape
    tf = 512
    tm = grp.tile
    return pl.pallas_call(
        _ffn_kernel,
        grid=(grp.n_tiles, dff // tf),
        in_specs=[grp.row_spec(d), pl.BlockSpec((1, d), lambda i, j: (0, 0)),
                  grp.seq_spec(d), grp.seq_spec(d), grp.seq_spec(d),
                  pl.BlockSpec((d, tf), lambda i, j: (0, j)),
                  pl.BlockSpec((tf, d), lambda i, j: (j, 0))],
        out_specs=grp.row_spec(d),
        out_shape=jax.ShapeDtypeStruct(x.shape, F32),
        scratch_shapes=[pltpu.VMEM((tm, d), BF16), pltpu.VMEM((tm, d), F32)],
        compiler_params=_cparams("parallel", "arbitrary"),
        name="ffn",
    )(x, g, grp.seq_rows(sc), grp.seq_rows(sh), grp.seq_rows(gate), w_up, w_down)


def _rwkv_proj_kernel(has_vfirst, t_minor, seq_len, *refs):
    (x_ref, g_ref, sc_ref, sh_ref, shift_ref, mix_ref, wr_ref, wk_ref, wv_ref,
     w1_ref, w2_ref, a1_ref, a2_ref, g1_ref, g2_ref, vec_ref, ones_ref) = refs[:17]
    refs = refs[17:]
    if has_vfirst:
        v1_ref, v2_ref, vf_ref = refs[:3]
        refs = refs[3:]
    scan_o = refs[:6] if t_minor else refs[:5]
    v_o, g_o, bonus_o, h_o, carry_sc = refs[len(scan_o):]
    i = pl.program_id(0)
    tt = x_ref.shape[0]

    @pl.when(i == 0)
    def _():
        carry_sc[...] = jnp.zeros_like(carry_sc)

    h = _rms_mod(x_ref[...], g_ref[...], sc_ref[0], sh_ref[0])
    h_o[...] = h
    row = lax.broadcasted_iota(I32, h.shape, 0)
    hp = pltpu.roll(h, 1, axis=0)
    hp = jnp.where(row == 0, carry_sc[...], hp)
    hp = jnp.where(((row + i * tt) & (seq_len - 1)) == 0, shift_ref[0], hp)
    carry_sc[...] = h[tt - 1:tt, :]
    dx = hp - h
    mix = mix_ref[...]
    xr, xw, xk, xv, xa, xg = [h + dx * mix[n:n + 1, :] for n in range(6)]
    vec = vec_ref[...]
    w0, a0, k_k, k_a, v0, r_k = [vec[n:n + 1, :] for n in range(6)]
    ones = ones_ref[...]

    r = jnp.dot(xr.astype(BF16), wr_ref[...], preferred_element_type=F32)
    k = jnp.dot(xk.astype(BF16), wk_ref[...], preferred_element_type=F32)
    v = jnp.dot(xv.astype(BF16), wv_ref[...], preferred_element_type=F32)
    w_pre = w0 + _bdot(jnp.tanh(_bdot(xw, w1_ref[...])), w2_ref[...])
    decay = jnp.exp(-float(np.exp(-0.5)) * jax.nn.sigmoid(w_pre))
    if has_vfirst:
        vgate = jax.nn.sigmoid(v0 + _bdot(_bdot(xv, v1_ref[...]), v2_ref[...]))
        v = v + (vf_ref[...] - v) * vgate
    a = jax.nn.sigmoid(a0 + _bdot(_bdot(xa, a1_ref[...]), a2_ref[...]))
    g_o[...] = _bdot(jax.nn.sigmoid(_bdot(xg, g1_ref[...])), g2_ref[...])
    kk = k * k_k
    kk = kk * lax.rsqrt(jnp.maximum(_seg_sum64(kk * kk, ones), 1e-24))
    k = k * (1.0 + (a - 1.0) * k_a)
    v_o[...] = v
    bonus_o[...] = _seg_sum64(r * k * r_k, ones) * v
    scan = (r, decay, k, -kk, kk * a)
    if t_minor:
        for o_ref, t in zip(scan_o, scan + (v,)):
            o_ref[0] = t.T
    else:
        for o_ref, t in zip(scan_o, scan):
            o_ref[...] = t


def _pad_to(w, axis, mult):
    n = w.shape[axis]
    pad = (-n) % mult
    if pad == 0:
        return w
    cfg = [(0, 0)] * w.ndim
    cfg[axis] = (0, pad)
    return jnp.pad(w, cfg)


def _lora_pair(w_in, w_out):
    return (_pad_to(w_in, 1, LANES).astype(BF16), _pad_to(w_out, 0, LANES).astype(BF16))


def _rwkv_proj(grp, x, g, sc, sh, shift, P, j, v_first, t_minor):
    d = x.shape[-1]
    assert grp.seq_len & (grp.seq_len - 1) == 0, "sequence-start test uses a bit mask"
    has_vfirst = v_first is not None
    w1, w2 = _lora_pair(P['rw_w1'][j], P['rw_w2'][j])
    a1, a2 = _lora_pair(P['rw_a1'][j], P['rw_a2'][j])
    g1, g2 = _lora_pair(P['rw_g1'][j], P['rw_g2'][j])
    v0 = P['rw_v0'][j - 1] if has_vfirst else jnp.zeros((d,), F32)
    vec = jnp.stack([P['rw_w0'][j], P['rw_a0'][j], P['rw_k_k'][j], P['rw_k_a'][j], v0,
                     P['rw_r_k'][j].reshape(d), jnp.zeros((d,), F32), jnp.zeros((d,), F32)])
    mix = _pad_to(P['rw_mix'][j], 0, 8)
    wrkv = P['rw_w_rkv'][j].astype(BF16)
    ones = _block_ones(MXU_DIM)
    args = [x, g, grp.seq_rows(sc), grp.seq_rows(sh), grp.seq_rows(shift), mix,
            wrkv[0], wrkv[1], wrkv[2], w1, w2, a1, a2, g1, g2, vec, ones]
    specs = [grp.row_spec(d), _const_spec((1, d)), grp.seq_spec(d), grp.seq_spec(d), grp.seq_spec(d),
             _const_spec(mix.shape)] + [_const_spec(a.shape) for a in args[6:]]
    if has_vfirst:
        v1, v2 = _lora_pair(P['rw_v1'][j - 1], P['rw_v2'][j - 1])
        args += [v1, v2, v_first]
        specs += [_const_spec(v1.shape), _const_spec(v2.shape), grp.row_spec(d)]
    out = jax.ShapeDtypeStruct(x.shape, F32)
    if t_minor:
        assert grp.per_seq and grp.tile % LANES == 0
        tps = grp.tiles_per_seq
        scan_specs = [pl.BlockSpec((1, d, grp.tile), lambda i: (i // tps, 0, i % tps))] * 6
        scan_shapes = [jax.ShapeDtypeStruct((grp.n_seq, d, grp.seq_len), F32)] * 6
    else:
        scan_specs = [grp.row_spec(d)] * 5
        scan_shapes = [out] * 5
    return pl.pallas_call(
        functools.partial(_rwkv_proj_kernel, has_vfirst, t_minor, grp.seq_len),
        grid=(grp.n_tiles,),
        in_specs=specs,
        out_specs=scan_specs + [grp.row_spec(d)] * 4,
        out_shape=scan_shapes + [out] * 4,
        scratch_shapes=[pltpu.VMEM((1, d), F32)],
        compiler_params=_cparams("arbitrary"),
        name="rwkv_proj",
    )(*args)


def _wkv_kernel(r_ref, w_ref, k_ref, a_ref, b_ref, v_ref, s0_ref, y_ref, s_ref):
    c = pl.program_id(1)
    tc, nk, _ = r_ref.shape

    @pl.when(c == 0)
    def _():
        s_ref[...] = s0_ref[...]

    def step(t, carry):
        vt = v_ref[t]
        parts = [None] * 4
        for kk in range(nk):
            term = s_ref[kk] * a_ref[t, pl.ds(kk, 1), :]
            parts[kk % 4] = term if parts[kk % 4] is None else parts[kk % 4] + term
        sa = (parts[0] + parts[1]) + (parts[2] + parts[3])
        parts = [None] * 4
        for kk in range(nk):
            s_new = (s_ref[kk] * w_ref[t, pl.ds(kk, 1), :] + sa * b_ref[t, pl.ds(kk, 1), :]
                     + vt * k_ref[t, pl.ds(kk, 1), :])
            s_ref[kk] = s_new
            term = s_new * r_ref[t, pl.ds(kk, 1), :]
            parts[kk % 4] = term if parts[kk % 4] is None else parts[kk % 4] + term
        y_ref[t] = (parts[0] + parts[1]) + (parts[2] + parts[3])
        return carry

    lax.fori_loop(0, tc, step, 0)


def _wkv_scan(r, w, k, a, b, v, s0, tc):
    t_len, nk, lanes = r.shape
    nv = v.shape[1]
    tc = min(tc, t_len)
    kspec = pl.BlockSpec((tc, nk, LANES), lambda l, c: (c, 0, l))
    vspec = pl.BlockSpec((tc, nv, LANES), lambda l, c: (c, 0, l))
    sspec = pl.BlockSpec((nk, nv, LANES), lambda l, c: (0, 0, l))
    return pl.pallas_call(
        _wkv_kernel,
        grid=(lanes // LANES, t_len // tc),
        in_specs=[kspec] * 5 + [vspec, sspec],
        out_specs=[vspec, sspec],
        out_shape=[jax.ShapeDtypeStruct(v.shape, F32), jax.ShapeDtypeStruct(s0.shape, F32)],
        compiler_params=_cparams("parallel", "arbitrary"),
        name="wkv_scan",
    )(r, w, k, a, b, v, s0)


def _to_klanes(x, n_seq, seq_len, dup):
    h = x.shape[-1] // RW_HEAD
    y = x.reshape(n_seq, seq_len, h, RW_HEAD).transpose(1, 3, 0, 2).reshape(seq_len, RW_HEAD, n_seq * h)
    return jnp.concatenate([y] * dup, axis=-1) if dup > 1 else y


def _to_vlanes(x, n_seq, seq_len, dup):
    h = x.shape[-1] // RW_HEAD
    y = x.reshape(n_seq, seq_len, h, dup, RW_HEAD // dup).transpose(1, 4, 3, 0, 2)
    return y.reshape(seq_len, RW_HEAD // dup, dup * n_seq * h)


def _from_vlanes(y, n_seq, seq_len, dup):
    h = y.shape[-1] // (dup * n_seq)
    y = y.reshape(seq_len, RW_HEAD // dup, dup, n_seq, h).transpose(3, 0, 4, 2, 1)
    return y.reshape(n_seq * seq_len, h * RW_HEAD)


def _state_to_lanes(s, dup):
    n_seq, h = s.shape[:2]
    y = s.reshape(n_seq, h, dup, RW_HEAD // dup, RW_HEAD).transpose(4, 3, 2, 0, 1)
    return y.reshape(RW_HEAD, RW_HEAD // dup, dup * n_seq * h)


def _state_from_lanes(s, n_seq, dup):
    h = s.shape[-1] // (dup * n_seq)
    y = s.reshape(RW_HEAD, RW_HEAD // dup, dup, n_seq, h).transpose(3, 4, 2, 1, 0)
    return y.reshape(n_seq, h, RW_HEAD, RW_HEAD)


def _wkv(grp, r, w, k, v, a, b, s0):
    n_seq, seq_len = grp.n_seq, grp.seq_len
    heads = r.shape[-1] // RW_HEAD
    dup = max(1, LANES // (n_seq * heads))
    kl = [_to_klanes(t, n_seq, seq_len, dup) for t in (r, w, k, a, b)]
    y, s = _wkv_scan(*kl, _to_vlanes(v, n_seq, seq_len, dup), _state_to_lanes(s0, dup), tc=64)
    return _from_vlanes(y, n_seq, seq_len, dup), _state_from_lanes(s, n_seq, dup)


def _wkv_t_kernel(r_ref, w_ref, k_ref, a_ref, b_ref, v_ref, s0_ref, y_ref, s_ref, ks_sc, vs_sc, ys_sc):
    c = pl.program_id(0)
    n_seq, d, tc = r_ref.shape
    nk = RW_HEAD
    heads = d // nk
    dup = LANES // (n_seq * heads)
    nv = nk // dup

    @pl.when(c == 0)
    def _():
        s_ref[...] = s0_ref[...]

    def head_rows(ref, ch):
        return [ref[b, pl.ds(ch, heads, stride=nk), :] for b in range(n_seq)]

    def load_k(kk, carry):
        for n, ref in enumerate((r_ref, w_ref, k_ref, a_ref, b_ref)):
            ks_sc[n, kk] = jnp.concatenate(head_rows(ref, kk) * dup, axis=0).T
        return carry

    lax.fori_loop(0, nk, load_k, 0, unroll=4)

    def load_v(vr, carry):
        rows = []
        for part in range(dup):
            rows += head_rows(v_ref, part * nv + vr)
        vs_sc[pl.ds(vr, tc, stride=nv), :] = jnp.concatenate(rows, axis=0).T
        return carry

    lax.fori_loop(0, nv, load_v, 0, unroll=2)

    def tree(parts):
        return (parts[0] + parts[1]) + (parts[2] + parts[3])

    parts = [None] * 4
    for kk in range(nk):
        term = s_ref[kk] * ks_sc[3, kk, 0:1, :]
        parts[kk % 4] = term if parts[kk % 4] is None else parts[kk % 4] + term

    def step(t, sa):
        trow = pl.ds(t, 1)
        nrow = pl.ds(jnp.minimum(t + 1, tc - 1), 1)
        vt = vs_sc[pl.ds(pl.multiple_of(t * nv, nv), nv), :]
        ys = [None] * 4
        sn = [None] * 4
        for kk in range(nk):
            s_new = (s_ref[kk] * ks_sc[1, kk, trow, :] + sa * ks_sc[4, kk, trow, :]
                     + vt * ks_sc[2, kk, trow, :])
            s_ref[kk] = s_new
            ty = s_new * ks_sc[0, kk, trow, :]
            ts = s_new * ks_sc[3, kk, nrow, :]
            ys[kk % 4] = ty if ys[kk % 4] is None else ys[kk % 4] + ty
            sn[kk % 4] = ts if sn[kk % 4] is None else sn[kk % 4] + ts
        ys_sc[pl.ds(pl.multiple_of(t * nv, nv), nv), :] = tree(ys)
        return tree(sn)

    lax.fori_loop(0, tc, step, tree(parts))

    def store_y(vr, carry):
        yt = ys_sc[pl.ds(vr, tc, stride=nv), :].T
        for part in range(dup):
            for b in range(n_seq):
                r0 = (part * n_seq + b) * heads
                y_ref[b, pl.ds(part * nv + vr, heads, stride=nk), :] = yt[r0:r0 + heads]
        return carry

    lax.fori_loop(0, nv, store_y, 0, unroll=2)


WKV_CHUNK = 128


def _wkv_t(r, w, k, a, b, v, s0):
    n_seq, d, t_len = r.shape
    heads = d // RW_HEAD
    assert LANES % (n_seq * heads) == 0 and t_len % WKV_CHUNK == 0
    dup = LANES // (n_seq * heads)
    nv = RW_HEAD // dup
    tc = WKV_CHUNK
    one = pl.Buffered(1)
    xspec = pl.BlockSpec((n_seq, d, tc), lambda c: (0, 0, c), pipeline_mode=one)
    sspec = pl.BlockSpec((RW_HEAD, nv, LANES), lambda c: (0, 0, 0), pipeline_mode=one)
    y, s = pl.pallas_call(
        _wkv_t_kernel,
        grid=(t_len // tc,),
        in_specs=[xspec] * 6 + [sspec],
        out_specs=[pl.BlockSpec((n_seq, d, tc), lambda c: (0, 0, c)),
                   pl.BlockSpec((RW_HEAD, nv, LANES), lambda c: (0, 0, 0))],
        out_shape=[jax.ShapeDtypeStruct(r.shape, F32), jax.ShapeDtypeStruct((RW_HEAD, nv, LANES), F32)],
        scratch_shapes=[pltpu.VMEM((5, RW_HEAD, tc, LANES), F32), pltpu.VMEM((tc * nv, LANES), F32),
                        pltpu.VMEM((tc * nv, LANES), F32)],
        compiler_params=_cparams("arbitrary"),
        name="wkv_scan_t",
    )(r, w, k, a, b, v, _state_to_lanes(s0, dup))
    return y, _state_from_lanes(s, n_seq, dup)


def _rwkv_post_kernel(t_minor, x_ref, y_ref, bonus_ref, g_ref, gate_ref, vec_ref, ones_ref, wo_ref, o_ref):
    vec = vec_ref[...]
    lnx_w, lnx_b = vec[0:1, :], vec[1:2, :]
    y = y_ref[0].T if t_minor else y_ref[...]
    ones = ones_ref[...]
    inv = 1.0 / RW_HEAD
    mu = _seg_sum64(y, ones) * inv
    yc = y - mu
    var = _seg_sum64(yc * yc, ones) * inv
    yn = yc * lax.rsqrt(var + GN_EPS) * lnx_w + lnx_b
    out = jnp.dot(((yn + bonus_ref[...]) * g_ref[...]).astype(BF16), wo_ref[...], preferred_element_type=F32)
    o_ref[...] = x_ref[...] + gate_ref[0] * out


def _rwkv_post(grp, x, y, bonus, g, gate, P, j, t_minor):
    d = x.shape[-1]
    vec = _pad_to(jnp.stack([P['rw_lnx_w'][j], P['rw_lnx_b'][j]]), 0, 8)
    ones = _block_ones(MXU_DIM)
    wo = P['rw_w_o'][j].astype(BF16)
    tps = grp.tiles_per_seq
    yspec = (pl.BlockSpec((1, d, grp.tile), lambda i: (i // tps, 0, i % tps)) if t_minor
             else grp.row_spec(d))
    return pl.pallas_call(
        functools.partial(_rwkv_post_kernel, t_minor),
        grid=(grp.n_tiles,),
        in_specs=[grp.row_spec(d), yspec, grp.row_spec(d), grp.row_spec(d), grp.seq_spec(d),
                  _const_spec(vec.shape), _const_spec(ones.shape), _const_spec(wo.shape)],
        out_specs=grp.row_spec(d),
        out_shape=jax.ShapeDtypeStruct(x.shape, F32),
        compiler_params=_cparams("parallel"),
        name="rwkv_post",
    )(x, y, bonus, g, grp.seq_rows(gate), vec, ones, wo)


Q_W = ATT_HEADS * ATT_HEAD_DIM
KV_W = KV_HEADS * ATT_HEAD_DIM
IQ_W = IDX_HEADS * IDX_DIM
OFF_K = Q_W
OFF_V = Q_W + KV_W
OFF_QI = Q_W + 2 * KV_W
OFF_KI = OFF_QI + IQ_W


def _rope_tables(pos):
    rd = ATT_HEAD_DIM // ROT_FRAC
    half = rd // 2
    inv = ROPE_THETA ** (-jnp.arange(half, dtype=F32) * 2.0 / rd)
    ang = pos.astype(F32)[:, None] * inv[None, :]
    cos, sin = jnp.cos(ang), jnp.sin(ang)
    n = pos.shape[0]
    rest = ATT_HEAD_DIM - rd
    c = jnp.concatenate([cos, cos, jnp.ones((n, rest), F32)], axis=-1)
    s_up = jnp.concatenate([-sin, jnp.zeros((n, half + rest), F32)], axis=-1)
    s_dn = jnp.concatenate([jnp.zeros((n, half), F32), sin, jnp.zeros((n, rest), F32)], axis=-1)
    return [jnp.concatenate([t, t], axis=-1) for t in (c, s_up, s_dn)]


def _rope128(x, c, s_up, s_dn):
    half = ATT_HEAD_DIM // ROT_FRAC // 2
    return x * c + pltpu.roll(x, LANES - half, axis=1) * s_up + pltpu.roll(x, half, axis=1) * s_dn


def _dsa_proj_kernel(x_ref, g_ref, sc_ref, sh_ref, w_ref, c_ref, su_ref, sd_ref, qg_ref, kg_ref, ig_ref,
                     ones_ref, q_o, k_o, v_o, qi_o, kiwi_o):
    h = _rms_mod(x_ref[...], g_ref[...], sc_ref[0], sh_ref[0])
    z = jnp.dot(h.astype(BF16), w_ref[...], preferred_element_type=F32)
    c, s_up, s_dn = c_ref[...], su_ref[...], sd_ref[...]
    ones = ones_ref[...]
    inv = 1.0 / ATT_HEAD_DIM
    lane = lax.broadcasted_iota(I32, (z.shape[0], LANES), 1)
    low = lane < ATT_HEAD_DIM

    def head_norm(t, gain, blk=ones):
        return t * lax.rsqrt(_seg_sum64(t * t, blk) * inv + NORM_EPS) * gain

    def pairs(t):
        return [t[:, n * LANES:(n + 1) * LANES] for n in range(t.shape[-1] // LANES)]

    q = head_norm(z[:, :Q_W], qg_ref[...]) * (LOG2E * ATT_HEAD_DIM ** -0.5)
    for n, t in enumerate(pairs(q)):
        t = _rope128(t, c, s_up, s_dn).astype(BF16)
        q_o[0, 2 * n] = t[:, :ATT_HEAD_DIM]
        q_o[0, 2 * n + 1] = t[:, ATT_HEAD_DIM:]
    k = head_norm(z[:, OFF_K:OFF_V], kg_ref[...])
    k_o[...] = jnp.concatenate([_rope128(t, c, s_up, s_dn) for t in pairs(k)], axis=-1)
    v_o[...] = z[:, OFF_V:OFF_QI]
    qi = z[:, OFF_QI:OFF_KI] * (IDX_DIM ** -0.5)
    for n, t in enumerate(pairs(qi)):
        t = _rope128(t, c, s_up, s_dn)
        qi_o[0, 2 * n] = jnp.where(low, t, 0.0).astype(BF16)
        qi_o[0, 2 * n + 1] = jnp.where(low, pltpu.roll(t, ATT_HEAD_DIM, axis=1), 0.0).astype(BF16)
    kw = z[:, OFF_KI:OFF_KI + LANES]
    ki = _rope128(head_norm(kw, ig_ref[...], ones[:LANES, :LANES]), c, s_up, s_dn)
    kiwi_o[...] = jnp.where(low, ki, kw * (IDX_HEADS ** -0.5))


def _dsa_proj(grp, x, g, sc, sh, pos, P, j):
    d = x.shape[-1]
    w = _pad_to(P['att_w_in'][j], 1, LANES).astype(BF16)
    tabs = [grp.pos_rows(t) for t in _rope_tables(pos)]
    qg = jnp.tile(P['att_q_norm'][j], ATT_HEADS).reshape(1, Q_W)
    kg = jnp.tile(P['att_k_norm'][j], KV_HEADS).reshape(1, KV_W)
    ig = _pad_to(P['idx_k_norm'][j], 0, LANES).reshape(1, LANES)
    ones = _block_ones(MXU_DIM)
    nsb = grp.n_seq if grp.per_seq else grp.n_tiles
    rows = grp.seq_len if grp.per_seq else grp.tile
    tps = grp.tiles_per_seq
    tt = grp.tile

    def head_spec(nh, wd):
        return pl.BlockSpec((1, nh, tt, wd), lambda i: (i // tps, 0, i % tps, 0))

    return pl.pallas_call(
        _dsa_proj_kernel,
        grid=(grp.n_tiles,),
        in_specs=[grp.row_spec(d), _const_spec((1, d)), grp.seq_spec(d), grp.seq_spec(d), _const_spec(w.shape),
                  grp.pos_spec(LANES), grp.pos_spec(LANES), grp.pos_spec(LANES),
                  _const_spec(qg.shape), _const_spec(kg.shape), _const_spec(ig.shape), _const_spec(ones.shape)],
        out_specs=[head_spec(ATT_HEADS, ATT_HEAD_DIM), grp.row_spec(KV_W), grp.row_spec(KV_W),
                   head_spec(IDX_HEADS, LANES), grp.row_spec(LANES)],
        out_shape=[jax.ShapeDtypeStruct((nsb, ATT_HEADS, rows, ATT_HEAD_DIM), BF16),
                   jax.ShapeDtypeStruct((grp.rows, KV_W), F32),
                   jax.ShapeDtypeStruct((grp.rows, KV_W), F32),
                   jax.ShapeDtypeStruct((nsb, IDX_HEADS, rows, LANES), BF16),
                   jax.ShapeDtypeStruct((grp.rows, LANES), F32)],
        compiler_params=_cparams("parallel"),
        name="dsa_proj",
    )(x, g, grp.seq_rows(sc), grp.seq_rows(sh), w, *tabs, qg, kg, ig, ones)


def _score_key(score, admissible):
    bits = pltpu.bitcast(score + 0.0, I32)
    key = jnp.where(bits >= 0, bits, bits ^ 0x7FFFFFFF)
    return jnp.where(admissible, key, NEG_INF_KEY)


def _topk_threshold(count, shape, topk, idx_bits):
    kf = float(topk)
    theta = jnp.where(count(lambda k, i: k >= 0) >= kf, jnp.full(shape, 0, I32), jnp.full(shape, INT_MIN, I32))

    def value_bit(n, theta):
        cand = theta + jnp.left_shift(jnp.int32(1), 30 - n)
        return jnp.where(count(lambda k, i: k >= cand) >= kf, cand, theta)

    theta = lax.fori_loop(0, 31, value_bit, theta)
    need = kf - count(lambda k, i: k > theta)

    def index_bit(n, cut):
        cand = cut + jnp.left_shift(jnp.int32(1), idx_bits - 1 - n)
        below = count(lambda k, i: jnp.where(k == theta, i, cand) < cand)
        return jnp.where(below < need, cand, cut)

    surplus = count(lambda k, i: k == theta) - need
    tied = jnp.where(theta > NEG_INF_KEY, surplus, 0.0)
    cut = lax.cond(jnp.max(tied) > 0.0,
                   lambda: lax.fori_loop(0, idx_bits, index_bit, jnp.zeros(shape, I32)),
                   lambda: jnp.full(shape, 2 ** 30, I32))
    return theta, cut


def _select_bias(key, idx, theta, cut):
    tie = jnp.where(idx <= cut, 0.0, MASK_BIAS)
    bias = jnp.where(key > theta, 0.0, jnp.where(key == theta, tie, MASK_BIAS))
    return jnp.where(key > NEG_INF_KEY, bias, MASK_BIAS)


ATT_ROW_BLOCK = 32


def _dsa_prompt_kernel(topk, q_ref, qi_ref, kiwiq_ref, x_ref, gate_ref, k_ref, v_ref, kiwi_ref, place_ref,
                       wo_ref, o_ref, key_sc, bias_sc, qblk_sc, kb_sc, vb_sc, s_sc, p_sc, m_sc, l_sc, alpha_sc,
                       acc_sc):
    qb = pl.program_id(1)
    tq = x_ref.shape[0]
    assert tq == LANES
    n_chunks, kc, _ = key_sc.shape
    lt = kc // LANES
    nkc = (qb * tq + tq - 1) // kc + 1
    qpos = qb * tq + lax.broadcasted_iota(I32, (kc, LANES), 1)
    krow = lax.broadcasted_iota(I32, (kc, LANES), 0)

    qi = qi_ref[0].reshape(IDX_HEADS * tq, LANES)
    wi_t = kiwiq_ref[...].T

    def score_chunk(c, carry):
        kic = kiwi_ref[pl.ds(pl.multiple_of(c * kc, kc), kc), :]
        logits = _bdot_nt(kic, qi)
        acc = None
        for h in range(IDX_HEADS):
            t = jnp.maximum(logits[:, h * tq:(h + 1) * tq], 0.0) * wi_t[IDX_DIM + h:IDX_DIM + h + 1, :]
            acc = t if acc is None else acc + t
        key_sc[c] = _score_key(acc, c * kc + krow <= qpos)
        return carry

    lax.fori_loop(0, nkc, score_chunk, 0)

    def count(pred):
        def body(c, acc):
            w = jnp.where(pred(key_sc[c], c * kc + krow), 1.0, 0.0)
            for j in range(lt):
                acc = acc + w[j * LANES:(j + 1) * LANES, :]
            return acc
        acc = lax.fori_loop(0, nkc, body, jnp.zeros((LANES, LANES), F32))
        return jnp.sum(acc, axis=0, keepdims=True)

    theta, cut = _topk_threshold(count, (1, LANES), topk, (n_chunks * kc - 1).bit_length())

    def bias_chunk(c, carry):
        bias_t = _select_bias(key_sc[c], c * kc + krow, theta, cut)
        bias_sc[c] = jnp.concatenate([bias_t[j * LANES:(j + 1) * LANES, :].T for j in range(lt)], axis=-1)
        return carry

    lax.fori_loop(0, nkc, bias_chunk, 0)

    gq = GROUPS * tq
    for g in range(KV_HEADS):
        qg = q_ref[0, g * GROUPS:(g + 1) * GROUPS].reshape(gq, ATT_HEAD_DIM)
        qblk_sc[g] = jnp.dot(qg, place_ref[g], preferred_element_type=F32).astype(BF16)
    m_sc[...] = jnp.full_like(m_sc, MASK_BIAS)
    l_sc[...] = jnp.zeros_like(l_sc)
    acc_sc[...] = jnp.zeros_like(acc_sc)

    def att_chunk(c, carry):
        rows = pl.ds(pl.multiple_of(c * kc, kc), kc)
        kb_sc[...] = k_ref[rows, :].astype(BF16)
        vb_sc[...] = v_ref[rows, :].astype(BF16)

        for g in range(KV_HEADS):
            buf = g % 2
            base = g * gq
            s_sc[buf] = _bdot_nt(qblk_sc[g], kb_sc[...])
            for r0 in range(0, gq, ATT_ROW_BLOCK):
                t0 = r0 % tq
                rb = slice(r0, r0 + ATT_ROW_BLOCK)
                ms = slice(base + r0, base + r0 + ATT_ROW_BLOCK)
                mx = None
                for j in range(lt):
                    cols = slice(j * LANES, (j + 1) * LANES)
                    t = s_sc[buf, rb, cols] + bias_sc[c, t0:t0 + ATT_ROW_BLOCK, cols]
                    s_sc[buf, rb, cols] = t
                    mx = t if mx is None else jnp.maximum(mx, t)
                m_old = m_sc[ms, :]
                m_new = jnp.maximum(m_old, jnp.max(mx, axis=1, keepdims=True))
                alpha_sc[buf, rb, :] = jnp.exp2(m_old - m_new)
                m_sc[ms, :] = m_new
            for r0 in range(0, gq, ATT_ROW_BLOCK):
                rb = slice(r0, r0 + ATT_ROW_BLOCK)
                ms = slice(base + r0, base + r0 + ATT_ROW_BLOCK)
                m_new = m_sc[ms, :]
                tot = None
                for j in range(lt):
                    cols = slice(j * LANES, (j + 1) * LANES)
                    p = jnp.exp2(s_sc[buf, rb, cols] - m_new)
                    p_sc[buf, rb, cols] = p.astype(BF16)
                    tot = p if tot is None else tot + p
                l_sc[ms, :] = alpha_sc[buf, rb, :] * l_sc[ms, :] + jnp.sum(tot, axis=1, keepdims=True)
            a = alpha_sc[buf]
            rs = slice(base, base + gq)
            acc_sc[rs, :] = (acc_sc[rs, :] * jnp.concatenate([a] * (KV_W // LANES), axis=-1)
                             + jnp.dot(p_sc[buf], vb_sc[...], preferred_element_type=F32))
        return carry

    lax.fori_loop(0, nkc, att_chunk, 0)
    out = None
    for head in range(ATT_HEADS):
        rs = slice(head * tq, (head + 1) * tq)
        inv_l = 1.0 / l_sc[rs, :]
        o = acc_sc[rs, :] * jnp.concatenate([inv_l] * (KV_W // LANES), axis=-1)
        t = jnp.dot(o.astype(BF16), wo_ref[head], preferred_element_type=F32)
        out = t if out is None else out + t
    o_ref[...] = x_ref[...] + gate_ref[0] * out


def _head_placement():
    p = np.zeros((KV_HEADS, ATT_HEAD_DIM, KV_W), np.float32)
    for g in range(KV_HEADS):
        p[g, np.arange(ATT_HEAD_DIM), g * ATT_HEAD_DIM + np.arange(ATT_HEAD_DIM)] = 1.0
    return jnp.asarray(p, dtype=BF16)


def _dsa_prompt(n_seq, seq_len, x, gate, q, k, v, qi, kiwi, w_o):
    d = x.shape[-1]
    tq, kc = 128, 512
    assert seq_len % kc == 0
    topk = min(TOPK_MAX, seq_len // 4)
    nq = seq_len // tq
    gq = GROUPS * tq
    wo_h = w_o.astype(BF16).reshape(KV_HEADS, GROUPS, 1, ATT_HEAD_DIM, d)
    sel = jnp.eye(KV_HEADS, dtype=BF16).reshape(KV_HEADS, 1, KV_HEADS, 1, 1)
    wo = (wo_h * sel).reshape(ATT_HEADS, KV_W, d)
    place = _head_placement()
    return pl.pallas_call(
        functools.partial(_dsa_prompt_kernel, topk),
        grid=(n_seq, nq),
        in_specs=[pl.BlockSpec((1, ATT_HEADS, tq, ATT_HEAD_DIM), lambda b, i: (b, 0, i, 0)),
                  pl.BlockSpec((1, IDX_HEADS, tq, LANES), lambda b, i: (b, 0, i, 0)),
                  pl.BlockSpec((tq, LANES), lambda b, i: (b * nq + i, 0)),
                  pl.BlockSpec((tq, d), lambda b, i: (b * nq + i, 0)),
                  pl.BlockSpec((1, 1, d), lambda b, i: (b, 0, 0)),
                  pl.BlockSpec((seq_len, KV_W), lambda b, i: (b, 0)),
                  pl.BlockSpec((seq_len, KV_W), lambda b, i: (b, 0)),
                  pl.BlockSpec((seq_len, LANES), lambda b, i: (b, 0)),
                  _const_spec(place.shape), _const_spec(wo.shape)],
        out_specs=pl.BlockSpec((tq, d), lambda b, i: (b * nq + i, 0)),
        out_shape=jax.ShapeDtypeStruct(x.shape, F32),
        scratch_shapes=[pltpu.VMEM((seq_len // kc, kc, tq), I32), pltpu.VMEM((seq_len // kc, tq, kc), F32),
                        pltpu.VMEM((KV_HEADS, gq, KV_W), BF16), pltpu.VMEM((kc, KV_W), BF16),
                        pltpu.VMEM((kc, KV_W), BF16), pltpu.VMEM((2, gq, kc), F32),
                        pltpu.VMEM((2, gq, kc), BF16), pltpu.VMEM((ATT_HEADS * tq, LANES), F32),
                        pltpu.VMEM((ATT_HEADS * tq, LANES), F32), pltpu.VMEM((2, gq, LANES), F32),
                        pltpu.VMEM((ATT_HEADS * tq, KV_W), F32)],
        compiler_params=_cparams("parallel", "arbitrary"),
        name="dsa_prompt",
    )(q, qi, kiwi, x, gate.reshape(n_seq, 1, d), k, v, kiwi, place, wo)


PAGES_PER_STEP = 16


def _dsa_sel_kernel(pt_ref, qi_ref, wib_ref, *refs):
    page_refs = refs[:PAGES_PER_STEP]
    kinew_ref, key_ref = refs[PAGES_PER_STEP:]
    p = pl.program_id(1)
    nsteps = pl.num_programs(1)
    npg = key_ref.shape[1] - 1
    t_new = key_ref.shape[2]
    qi = qi_ref[0]
    wib = wib_ref[0]

    def score(keys):
        t = jnp.maximum(_bdot(qi, keys), 0.0) * wib
        acc = t[0:t_new]
        for h in range(1, IDX_HEADS):
            acc = acc + t[h * t_new:(h + 1) * t_new]
        return acc

    for n, page_ref in enumerate(page_refs):
        key_ref[0, p * PAGES_PER_STEP + n] = _score_key(score(page_ref[0, 0]), True)

    @pl.when(p == nsteps - 1)
    def _():
        lane = lax.broadcasted_iota(I32, (t_new, LANES), 1)
        row = lax.broadcasted_iota(I32, (t_new, LANES), 0)
        key_ref[0, npg] = _score_key(score(kinew_ref[0]), lane <= row)


def _dsa_search_kernel(topk, key_ref, bias_ref):
    n_seq, n_slots, t_new, _ = key_ref.shape
    lane = lax.broadcasted_iota(I32, (n_seq, t_new, LANES), 2)

    def count(pred):
        def body(s, acc):
            return acc + jnp.where(pred(key_ref[:, s], s * LANES + lane), 1.0, 0.0)
        acc = lax.fori_loop(0, n_slots, body, jnp.zeros((n_seq, t_new, LANES), F32))
        return jnp.sum(acc, axis=2, keepdims=True)

    theta, cut = _topk_threshold(count, (n_seq, t_new, LANES), topk, (n_slots * LANES - 1).bit_length())

    def write(s, carry):
        bias_ref[:, s] = _select_bias(key_ref[:, s], s * LANES + lane, theta, cut)
        return carry

    lax.fori_loop(0, n_slots, write, 0)


def _page_spec(width, n, layer):
    return pl.BlockSpec((1, 1, width, PAGE),
                        lambda b, p, pt: (layer, pt[b, p * PAGES_PER_STEP + n], 0, 0))


def _dsa_sel(page_table, qi, wib, cache_ki, layer, kinew, t_new):
    n_seq, npg = page_table.shape
    assert npg % PAGES_PER_STEP == 0
    topk = min(TOPK_MAX, (npg * PAGE + t_new) // 4)
    rows = qi.shape[1]
    grid_spec = pltpu.PrefetchScalarGridSpec(
        num_scalar_prefetch=1,
        grid=(n_seq, npg // PAGES_PER_STEP),
        in_specs=[pl.BlockSpec((1, rows, IDX_DIM), lambda b, p, pt: (b, 0, 0)),
                  pl.BlockSpec((1, rows, LANES), lambda b, p, pt: (b, 0, 0))]
        + [_page_spec(IDX_DIM, n, layer) for n in range(PAGES_PER_STEP)]
        + [pl.BlockSpec((1, IDX_DIM, PAGE), lambda b, p, pt: (b, 0, 0))],
        out_specs=pl.BlockSpec((1, npg + 1, t_new, LANES), lambda b, p, pt: (b, 0, 0, 0)))
    keys = pl.pallas_call(
        _dsa_sel_kernel,
        grid_spec=grid_spec,
        out_shape=jax.ShapeDtypeStruct((n_seq, npg + 1, t_new, LANES), I32),
        compiler_params=_cparams("parallel", "arbitrary"),
        name="dsa_sel",
    )(page_table, qi, wib, *([cache_ki] * PAGES_PER_STEP), kinew)
    return pl.pallas_call(
        functools.partial(_dsa_search_kernel, topk),
        out_shape=jax.ShapeDtypeStruct(keys.shape, F32),
        compiler_params=pltpu.CompilerParams(vmem_limit_bytes=VMEM_LIMIT),
        name="dsa_search",
    )(keys)


def _dsa_att_kernel(pt_ref, q_ref, *refs):
    kpage_refs = refs[:PAGES_PER_STEP]
    vpage_refs = refs[PAGES_PER_STEP:2 * PAGES_PER_STEP]
    knew_ref, vnew_ref, bias_ref, biasnew_ref, o_ref, m_sc, l_sc, acc_sc = refs[2 * PAGES_PER_STEP:]
    p = pl.program_id(1)
    last = pl.num_programs(1) - 1
    rows = q_ref.shape[1]
    t_new = bias_ref.shape[2]

    @pl.when(p == 0)
    def _():
        m_sc[...] = jnp.full_like(m_sc, MASK_BIAS)
        l_sc[...] = jnp.zeros_like(l_sc)
        acc_sc[...] = jnp.zeros_like(acc_sc)

    def attend(kv_bias):
        q = q_ref[0]
        ss = []
        for kk, _, bias in kv_bias:
            s = _bdot(q, kk)
            ss.append((s.reshape(rows // t_new, t_new, LANES) + bias[None]).reshape(rows, LANES))
        mx = ss[0]
        for s in ss[1:]:
            mx = jnp.maximum(mx, s)
        m_old = m_sc[...]
        m_new = jnp.maximum(m_old, jnp.max(mx, axis=1, keepdims=True))
        alpha = jnp.exp2(m_old - m_new)
        tot = None
        pv = None
        for s, (_, vv, _) in zip(ss, kv_bias):
            pr = jnp.exp2(s - m_new)
            tot = pr if tot is None else tot + pr
            t = _bdot_nt(pr, vv)
            pv = t if pv is None else pv + t
        l_sc[...] = alpha * l_sc[...] + jnp.sum(tot, axis=1, keepdims=True)
        acc_sc[...] = alpha * acc_sc[...] + pv
        m_sc[...] = m_new

    attend([(kpage_refs[n][0, 0], vpage_refs[n][0, 0], bias_ref[0, n]) for n in range(PAGES_PER_STEP)])

    @pl.when(p == last)
    def _():
        attend([(knew_ref[0], vnew_ref[0], biasnew_ref[0, 0])])
        o_ref[0] = acc_sc[...] / l_sc[...]


def _dsa_att(page_table, qblk, cache_k, cache_v, layer, knew, vnew, bias):
    n_seq, npg = page_table.shape
    rows = qblk.shape[1]
    t_new = bias.shape[2]
    grid_spec = pltpu.PrefetchScalarGridSpec(
        num_scalar_prefetch=1,
        grid=(n_seq, npg // PAGES_PER_STEP),
        in_specs=[pl.BlockSpec((1, rows, KV_W), lambda b, p, pt: (b, 0, 0))]
        + [_page_spec(KV_W, n, layer) for n in range(PAGES_PER_STEP)] * 2
        + [pl.BlockSpec((1, KV_W, PAGE), lambda b, p, pt: (b, 0, 0)),
           pl.BlockSpec((1, KV_W, PAGE), lambda b, p, pt: (b, 0, 0)),
           pl.BlockSpec((1, PAGES_PER_STEP, t_new, LANES), lambda b, p, pt: (b, p, 0, 0)),
           pl.BlockSpec((1, 1, t_new, LANES), lambda b, p, pt: (b, npg, 0, 0))],
        out_specs=pl.BlockSpec((1, rows, KV_W), lambda b, p, pt: (b, 0, 0)),
        scratch_shapes=[pltpu.VMEM((rows, 1), F32), pltpu.VMEM((rows, 1), F32), pltpu.VMEM((rows, KV_W), F32)])
    return pl.pallas_call(
        _dsa_att_kernel,
        grid_spec=grid_spec,
        out_shape=jax.ShapeDtypeStruct((n_seq, rows, KV_W), F32),
        compiler_params=_cparams("parallel", "arbitrary"),
        name="dsa_att",
    )(page_table, qblk, *([cache_k] * PAGES_PER_STEP), *([cache_v] * PAGES_PER_STEP), knew, vnew, bias, bias)


def _linear_res_kernel(x_ref, a_ref, gate_ref, w_ref, o_ref):
    o_ref[...] = x_ref[...] + gate_ref[0] * _bdot(a_ref[...], w_ref[...])


def _linear_res(grp, x, a, gate, w):
    d = x.shape[-1]
    wb = w.astype(BF16)
    return pl.pallas_call(
        _linear_res_kernel,
        grid=(grp.n_tiles,),
        in_specs=[grp.row_spec(d), grp.row_spec(a.shape[-1]), grp.seq_spec(d), _const_spec(wb.shape)],
        out_specs=grp.row_spec(d),
        out_shape=jax.ShapeDtypeStruct(x.shape, F32),
        compiler_params=_cparams("parallel"),
        name="linear_res",
    )(x, a, grp.seq_rows(gate), wb)


def _dsa_sample(grp, x, gate, q, k, v, qi, kiwi, cache_k, cache_v, cache_ki, layer, page_table, w_o):
    n_seq, t_new = grp.n_seq, grp.seq_len
    assert t_new <= PAGE and grp.n_tiles == 1

    def new_page(t):
        return _pad_to(t.reshape(n_seq, t_new, t.shape[-1]), 1, PAGE).transpose(0, 2, 1)

    qi_b = qi[0, :, :, :IDX_DIM].reshape(IDX_HEADS, n_seq, t_new, IDX_DIM).transpose(1, 0, 2, 3)
    qi_b = qi_b.reshape(n_seq, IDX_HEADS * t_new, IDX_DIM)
    wi = kiwi[:, IDX_DIM:IDX_DIM + IDX_HEADS].reshape(n_seq, t_new, IDX_HEADS).transpose(0, 2, 1)
    wib = jnp.broadcast_to(wi.reshape(n_seq, IDX_HEADS * t_new, 1), (n_seq, IDX_HEADS * t_new, LANES))
    bias = _dsa_sel(page_table, qi_b, wib, cache_ki, layer, new_page(kiwi[:, :IDX_DIM]), t_new)
    q_b = q[0].reshape(KV_HEADS, GROUPS, n_seq, t_new, ATT_HEAD_DIM).transpose(2, 0, 1, 3, 4)
    eye = jnp.eye(KV_HEADS, dtype=q_b.dtype)
    qblk = (q_b[:, :, :, :, None, :] * eye[None, :, None, None, :, None])
    qblk = qblk.reshape(n_seq, ATT_HEADS * t_new, KV_W)
    o = _dsa_att(page_table, qblk, cache_k, cache_v, layer, new_page(k), new_page(v), bias)
    o = o.reshape(n_seq, KV_HEADS, GROUPS, t_new, KV_HEADS, ATT_HEAD_DIM)
    o = jnp.stack([o[:, g, :, :, g, :] for g in range(KV_HEADS)], axis=1)
    o = o.transpose(0, 3, 1, 2, 4).reshape(n_seq * t_new, Q_W)
    return _linear_res(grp, x, o, gate, w_o)


TOKEN_TILE = 256
FFN_TILE = 1024


def _trunk(x, mods, pos, wkv0, shift0, attn_fn, P):
    n_seq, seq_len, d = x.shape
    depth = mods.shape[0]
    grp = _Group(n_seq, seq_len, TOKEN_TILE)
    grp_ffn = _Group(n_seq, seq_len, FFN_TILE)
    x = x.reshape(n_seq * seq_len, d)
    v_first = None
    ks, vs, kis, wkvs, shifts = [], [], [], [], []
    for i in range(depth):
        j = i // 2
        sh_a, sc_a, g_a, sh_f, sc_f, g_f = jnp.split(mods[i], 6, axis=-1)
        g_att = P['norm_g'][i, 0].reshape(1, d)
        g_ffn = P['norm_g'][i, 1].reshape(1, d)
        if i % 2 == 0:
            t_minor = grp.per_seq and seq_len % WKV_CHUNK == 0 and LANES % (n_seq * (d // RW_HEAD)) == 0
            outs = _rwkv_proj(grp, x, g_att, sc_a, sh_a, shift0[j], P, j, v_first, t_minor)
            v, gate, bonus, h = outs[-4:]
            if j == 0:
                v_first = v
            if t_minor:
                y, state = _wkv_t(*outs[:6], wkv0[j])
            else:
                r, w, k, a, b = outs[:5]
                y, state = _wkv(grp, r, w, k, v, a, b, wkv0[j])
            x = _rwkv_post(grp, x, y, bonus, gate, g_a, P, j, t_minor)
            wkvs.append(state)
            shifts.append(h.reshape(n_seq, seq_len, d)[:, -1])
        else:
            q, k, v, qi, kiwi = _dsa_proj(grp, x, g_att, sc_a, sh_a, pos, P, j)
            x = attn_fn(j, grp, x, g_a, q, k, v, qi, kiwi)
            ks.append(k.reshape(n_seq, seq_len, KV_HEADS, ATT_HEAD_DIM))
            vs.append(v.reshape(n_seq, seq_len, KV_HEADS, ATT_HEAD_DIM))
            kis.append(kiwi[:, :IDX_DIM].reshape(n_seq, seq_len, IDX_DIM))
        x = _ffn(grp_ffn, x, g_ffn, sc_f, sh_f, g_f, P['w_up'][i], P['w_down'][i])
    return (x.reshape(n_seq, seq_len, d), jnp.stack(ks), jnp.stack(vs), jnp.stack(kis),
            jnp.stack(wkvs), jnp.stack(shifts))


def kernel(x_prompt, x_sample, cache_k, cache_v, cache_idx_k, state_wkv, state_shift, page_table,
           c_prompt, c_sample, norm_g, w_ada, b_ada, w_up, w_down, rw_mix, rw_w_rkv, rw_w_o,
           rw_w0, rw_w1, rw_w2, rw_a0, rw_a1, rw_a2, rw_v0, rw_v1, rw_v2, rw_g1, rw_g2,
           rw_k_k, rw_k_a, rw_r_k, rw_lnx_w, rw_lnx_b, att_w_in, att_w_o, att_q_norm,
           att_k_norm, idx_k_norm):
    P = dict(norm_g=norm_g, w_up=w_up.astype(BF16), w_down=w_down.astype(BF16),
             rw_mix=rw_mix, rw_w_rkv=rw_w_rkv, rw_w_o=rw_w_o, rw_w0=rw_w0, rw_w1=rw_w1,
             rw_w2=rw_w2, rw_a0=rw_a0, rw_a1=rw_a1, rw_a2=rw_a2, rw_v0=rw_v0, rw_v1=rw_v1,
             rw_v2=rw_v2, rw_g1=rw_g1, rw_g2=rw_g2, rw_k_k=rw_k_k, rw_k_a=rw_k_a, rw_r_k=rw_r_k,
             rw_lnx_w=rw_lnx_w, rw_lnx_b=rw_lnx_b, att_w_in=att_w_in, att_w_o=att_w_o,
             att_q_norm=att_q_norm, att_k_norm=att_k_norm, idx_k_norm=idx_k_norm)
    n_p, seq, d = x_prompt.shape
    n_s, dec_seq, _ = x_sample.shape
    n_rwkv = state_wkv.shape[0]
    heads = d // RW_HEAD
    past_len = page_table.shape[1] * PAGE

    mods = _ada(jnp.concatenate([c_prompt, c_sample], axis=0), w_ada, b_ada)
    n_dsa, pool = cache_k.shape[:2]
    pages_k = cache_k.transpose(0, 1, 3, 4, 2).reshape(n_dsa, pool, KV_W, PAGE)
    pages_v = cache_v.transpose(0, 1, 3, 4, 2).reshape(n_dsa, pool, KV_W, PAGE)
    pages_ki = cache_idx_k.transpose(0, 1, 3, 2)

    def prompt_attn(j, grp, x, gate, q, k, v, qi, kiwi):
        return _dsa_prompt(grp.n_seq, grp.seq_len, x, gate, q, k, v, qi, kiwi, att_w_o[j])

    def sample_attn(j, grp, x, gate, q, k, v, qi, kiwi):
        return _dsa_sample(grp, x, gate, q, k, v, qi, kiwi, pages_k, pages_v, pages_ki, j,
                           page_table, att_w_o[j])

    wkv0 = jnp.zeros((n_rwkv, n_p, heads, RW_HEAD, RW_HEAD), F32)
    shift0 = jnp.zeros((n_rwkv, n_p, d), F32)
    y_p, k_p, v_p, ki_p, wkv_p, shift_p = _trunk(
        x_prompt, mods[:, :n_p], jnp.arange(seq), wkv0, shift0, prompt_attn, P)
    y_s, k_s, v_s, ki_s, wkv_s, shift_s = _trunk(
        x_sample, mods[:, n_p:], past_len + jnp.arange(dec_seq), state_wkv, state_shift, sample_attn, P)
    return (y_p, y_s, k_p, v_p, ki_p, wkv_p, shift_p, k_s, v_s, ki_s, wkv_s, shift_s)
```

```python
import functools

import jax
import jax.numpy as jnp
import numpy as np
from jax import lax
from jax.experimental import pallas as pl
from jax.experimental.pallas import tpu as pltpu

F32 = jnp.float32
BF16 = jnp.bfloat16
I32 = jnp.int32

NORM_EPS = 1e-6
GN_EPS = 64e-5
RW_HEAD = 64
ATT_HEADS = 16
ATT_HEAD_DIM = 64
KV_HEADS = 4
GROUPS = ATT_HEADS // KV_HEADS
IDX_HEADS = 8
IDX_DIM = 64
TOPK_MAX = 256
ROPE_THETA = 500000.0
ROT_FRAC = 4
PAGE = 128

LANES = 128
MXU_DIM = 256
VMEM_LIMIT = 56 * 1024 * 1024

LOG2E = 1.4426950408889634
INT_MIN = -(2 ** 31)
NEG_INF_KEY = INT_MIN + 0x7FFFFF
MASK_BIAS = -1e30


def _cparams(*sem):
    return pltpu.CompilerParams(dimension_semantics=sem, vmem_limit_bytes=VMEM_LIMIT)


def _const_spec(shape):
    n = len(shape)
    return pl.BlockSpec(shape, lambda *_: (0,) * n, pipeline_mode=pl.Buffered(1))


def _bdot(a, b):
    return jnp.dot(a.astype(BF16), b.astype(BF16), preferred_element_type=F32)


def _bdot_nt(a, b):
    return lax.dot_general(a.astype(BF16), b.astype(BF16), (((1,), (1,)), ((), ())),
                           preferred_element_type=F32)


def _rms_mod(x, g, sc, sh):
    ms = jnp.mean(x * x, axis=-1, keepdims=True)
    return x * lax.rsqrt(ms + NORM_EPS) * g * (1.0 + sc) + sh


def _seg_sum64(x, ones):
    hi = x.astype(BF16)
    lo = (x - hi.astype(F32)).astype(BF16)
    w = ones.shape[0]
    outs = []
    for c in range(x.shape[-1] // w):
        sl = slice(c * w, (c + 1) * w)
        outs.append(jnp.dot(hi[:, sl], ones, preferred_element_type=F32)
                    + jnp.dot(lo[:, sl], ones, preferred_element_type=F32))
    return outs[0] if len(outs) == 1 else jnp.concatenate(outs, axis=-1)


def _block_ones(width):
    i = np.arange(width) // 64
    return jnp.asarray((i[:, None] == i[None, :]).astype(np.float32), dtype=BF16)


def _ada_kernel(c_ref, w_ref, b_ref, o_ref):
    c = c_ref[...]
    s = c * jax.nn.sigmoid(c)
    o_ref[0] = _bdot(s, w_ref[0]) + b_ref[0]


def _ada(c_all, w_ada, b_ada):
    depth, d, d6 = w_ada.shape
    nb = c_all.shape[0]
    tn = 1536
    return pl.pallas_call(
        _ada_kernel,
        grid=(depth, d6 // tn),
        in_specs=[pl.BlockSpec((nb, d), lambda i, j: (0, 0)),
                  pl.BlockSpec((1, d, tn), lambda i, j: (i, 0, j)),
                  pl.BlockSpec((1, 1, tn), lambda i, j: (i, 0, j))],
        out_specs=pl.BlockSpec((1, nb, tn), lambda i, j: (i, 0, j)),
        out_shape=jax.ShapeDtypeStruct((depth, nb, d6), F32),
        compiler_params=_cparams("parallel", "parallel"),
        name="ada",
    )(c_all, w_ada, b_ada.reshape(depth, 1, d6))


class _Group:
    def __init__(self, n_seq, seq_len, tile):
        self.n_seq, self.seq_len = n_seq, seq_len
        self.rows = n_seq * seq_len
        self.tile = min(tile, self.rows)
        assert self.rows % self.tile == 0
        assert seq_len % self.tile == 0 or self.tile % seq_len == 0
        self.tiles_per_seq = max(seq_len // self.tile, 1)
        self.per_seq = seq_len >= self.tile
        self.n_tiles = self.rows // self.tile

    def seq_rows(self, m):
        d = m.shape[-1]
        if self.per_seq:
            return m.reshape(self.n_seq, 1, d)
        return jnp.repeat(m, self.seq_len, axis=0).reshape(self.n_tiles, self.tile, d)

    def seq_spec(self, d):
        r = 1 if self.per_seq else self.tile
        tps = self.tiles_per_seq
        return pl.BlockSpec((1, r, d), lambda i, *_: (i // tps, 0, 0))

    def pos_rows(self, tab):
        if self.per_seq:
            return tab
        return jnp.tile(tab, (self.tile // self.seq_len, 1))

    def pos_spec(self, w):
        tps = self.tiles_per_seq
        return pl.BlockSpec((self.tile, w), lambda i, *_: (i % tps, 0))

    def row_spec(self, w):
        return pl.BlockSpec((self.tile, w), lambda i, *_: (i, 0))


def _ffn_kernel(x_ref, g_ref, sc_ref, sh_ref, gate_ref, wu_ref, wd_ref, o_ref, h_sc, acc_sc):
    j = pl.program_id(1)

    @pl.when(j == 0)
    def _():
        h_sc[...] = _rms_mod(x_ref[...], g_ref[...], sc_ref[0], sh_ref[0]).astype(BF16)
        acc_sc[...] = jnp.zeros_like(acc_sc)

    u = jnp.maximum(jnp.dot(h_sc[...], wu_ref[...], preferred_element_type=F32), 0.0)
    acc_sc[...] += jnp.dot((u * u).astype(BF16), wd_ref[...], preferred_element_type=F32)

    @pl.when(j == pl.num_programs(1) - 1)
    def _():
        o_ref[...] = x_ref[...] + gate_ref[0] * acc_sc[...]


def _ffn(grp, x, g, sc, sh, gate, w_up, w_down):
    d, dff = w_up.shape
    tf = 512
    tm = grp.tile
    return pl.pallas_call(
        _ffn_kernel,
        grid=(grp.n_tiles, dff // tf),
        in_specs=[grp.row_spec(d), pl.BlockSpec((1, d), lambda i, j: (0, 0)),
                  grp.seq_spec(d), grp.seq_spec(d), grp.seq_spec(d),
                  pl.BlockSpec((d, tf), lambda i, j: (0, j)),
                  pl.BlockSpec((tf, d), lambda i, j: (j, 0))],
        out_specs=grp.row_spec(d),
        out_shape=jax.ShapeDtypeStruct(x.shape, F32),
        scratch_shapes=[pltpu.VMEM((tm, d), BF16), pltpu.VMEM((tm, d), F32)],
        compiler_params=_cparams("parallel", "arbitrary"),
        name="ffn",
    )(x, g, grp.seq_rows(sc), grp.seq_rows(sh), grp.seq_rows(gate), w_up, w_down)


def _rwkv_proj_kernel(has_vfirst, t_minor, seq_len, *refs):
    (x_ref, g_ref, sc_ref, sh_ref, shift_ref, mix_ref, wr_ref, wk_ref, wv_ref,
     w1_ref, w2_ref, a1_ref, a2_ref, g1_ref, g2_ref, vec_ref, ones_ref) = refs[:17]
    refs = refs[17:]
    if has_vfirst:
        v1_ref, v2_ref, vf_ref = refs[:3]
        refs = refs[3:]
    scan_o = refs[:6] if t_minor else refs[:5]
    v_o, g_o, bonus_o, h_o, carry_sc = refs[len(scan_o):]
    i = pl.program_id(0)
    tt = x_ref.shape[0]

    @pl.when(i == 0)
    def _():
        carry_sc[...] = jnp.zeros_like(carry_sc)

    h = _rms_mod(x_ref[...], g_ref[...], sc_ref[0], sh_ref[0])
    h_o[...] = h
    row = lax.broadcasted_iota(I32, h.shape, 0)
    hp = pltpu.roll(h, 1, axis=0)
    hp = jnp.where(row == 0, carry_sc[...], hp)
    hp = jnp.where(((row + i * tt) & (seq_len - 1)) == 0, shift_ref[0], hp)
    carry_sc[...] = h[tt - 1:tt, :]
    dx = hp - h
    mix = mix_ref[...]
    xr, xw, xk, xv, xa, xg = [h + dx * mix[n:n + 1, :] for n in range(6)]
    vec = vec_ref[...]
    w0, a0, k_k, k_a, v0, r_k = [vec[n:n + 1, :] for n in range(6)]
    ones = ones_ref[...]

    r = jnp.dot(xr.astype(BF16), wr_ref[...], preferred_element_type=F32)
    k = jnp.dot(xk.astype(BF16), wk_ref[...], preferred_element_type=F32)
    v = jnp.dot(xv.astype(BF16), wv_ref[...], preferred_element_type=F32)
    w_pre = w0 + _bdot(jnp.tanh(_bdot(xw, w1_ref[...])), w2_ref[...])
    decay = jnp.exp(-float(np.exp(-0.5)) * jax.nn.sigmoid(w_pre))
    if has_vfirst:
        vgate = jax.nn.sigmoid(v0 + _bdot(_bdot(xv, v1_ref[...]), v2_ref[...]))
        v = v + (vf_ref[...] - v) * vgate
    a = jax.nn.sigmoid(a0 + _bdot(_bdot(xa, a1_ref[...]), a2_ref[...]))
    g_o[...] = _bdot(jax.nn.sigmoid(_bdot(xg, g1_ref[...])), g2_ref[...])
    kk = k * k_k
    kk = kk * lax.rsqrt(jnp.maximum(_seg_sum64(kk * kk, ones), 1e-24))
    k = k * (1.0 + (a - 1.0) * k_a)
    v_o[...] = v
    bonus_o[...] = _seg_sum64(r * k * r_k, ones) * v
    scan = (r, decay, k, -kk, kk * a)
    if t_minor:
        for o_ref, t in zip(scan_o, scan + (v,)):
            o_ref[0] = t.T
    else:
        for o_ref, t in zip(scan_o, scan):
            o_ref[...] = t


def _pad_to(w, axis, mult):
    n = w.shape[axis]
    pad = (-n) % mult
    if pad == 0:
        return w
    cfg = [(0, 0)] * w.ndim
    cfg[axis] = (0, pad)
    return jnp.pad(w, cfg)


def _lora_pair(w_in, w_out):
    return (_pad_to(w_in, 1, LANES).astype(BF16), _pad_to(w_out, 0, LANES).astype(BF16))


def _rwkv_proj(grp, x, g, sc, sh, shift, P, j, v_first, t_minor):
    d = x.shape[-1]
    assert grp.seq_len & (grp.seq_len - 1) == 0, "sequence-start test uses a bit mask"
    has_vfirst = v_first is not None
    w1, w2 = _lora_pair(P['rw_w1'][j], P['rw_w2'][j])
    a1, a2 = _lora_pair(P['rw_a1'][j], P['rw_a2'][j])
    g1, g2 = _lora_pair(P['rw_g1'][j], P['rw_g2'][j])
    v0 = P['rw_v0'][j - 1] if has_vfirst else jnp.zeros((d,), F32)
    vec = jnp.stack([P['rw_w0'][j], P['rw_a0'][j], P['rw_k_k'][j], P['rw_k_a'][j], v0,
                     P['rw_r_k'][j].reshape(d), jnp.zeros((d,), F32), jnp.zeros((d,), F32)])
    mix = _pad_to(P['rw_mix'][j], 0, 8)
    wrkv = P['rw_w_rkv'][j].astype(BF16)
    ones = _block_ones(MXU_DIM)
    args = [x, g, grp.seq_rows(sc), grp.seq_rows(sh), grp.seq_rows(shift), mix,
            wrkv[0], wrkv[1], wrkv[2], w1, w2, a1, a2, g1, g2, vec, ones]
    specs = [grp.row_spec(d), _const_spec((1, d)), grp.seq_spec(d), grp.seq_spec(d), grp.seq_spec(d),
             _const_spec(mix.shape)] + [_const_spec(a.shape) for a in args[6:]]
    if has_vfirst:
        v1, v2 = _lora_pair(P['rw_v1'][j - 1], P['rw_v2'][j - 1])
        args += [v1, v2, v_first]
        specs += [_const_spec(v1.shape), _const_spec(v2.shape), grp.row_spec(d)]
    out = jax.ShapeDtypeStruct(x.shape, F32)
    if t_minor:
        assert grp.per_seq and grp.tile % LANES == 0
        tps = grp.tiles_per_seq
        scan_specs = [pl.BlockSpec((1, d, grp.tile), lambda i: (i // tps, 0, i % tps))] * 6
        scan_shapes = [jax.ShapeDtypeStruct((grp.n_seq, d, grp.seq_len), F32)] * 6
    else:
        scan_specs = [grp.row_spec(d)] * 5
        scan_shapes = [out] * 5
    return pl.pallas_call(
        functools.partial(_rwkv_proj_kernel, has_vfirst, t_minor, grp.seq_len),
        grid=(grp.n_tiles,),
        in_specs=specs,
        out_specs=scan_specs + [grp.row_spec(d)] * 4,
        out_shape=scan_shapes + [out] * 4,
        scratch_shapes=[pltpu.VMEM((1, d), F32)],
        compiler_params=_cparams("arbitrary"),
        name="rwkv_proj",
    )(*args)


def _wkv_kernel(r_ref, w_ref, k_ref, a_ref, b_ref, v_ref, s0_ref, y_ref, s_ref):
    c = pl.program_id(1)
    tc, nk, _ = r_ref.shape

    @pl.when(c == 0)
    def _():
        s_ref[...] = s0_ref[...]

    def step(t, carry):
        vt = v_ref[t]
        parts = [None] * 4
        for kk in range(nk):
            term = s_ref[kk] * a_ref[t, pl.ds(kk, 1), :]
            parts[kk % 4] = term if parts[kk % 4] is None else parts[kk % 4] + term
        sa = (parts[0] + parts[1]) + (parts[2] + parts[3])
        parts = [None] * 4
        for kk in range(nk):
            s_new = (s_ref[kk] * w_ref[t, pl.ds(kk, 1), :] + sa * b_ref[t, pl.ds(kk, 1), :]
                     + vt * k_ref[t, pl.ds(kk, 1), :])
            s_ref[kk] = s_new
            term = s_new * r_ref[t, pl.ds(kk, 1), :]
            parts[kk % 4] = term if parts[kk % 4] is None else parts[kk % 4] + term
        y_ref[t] = (parts[0] + parts[1]) + (parts[2] + parts[3])
        return carry

    lax.fori_loop(0, tc, step, 0)


def _wkv_scan(r, w, k, a, b, v, s0, tc):
    t_len, nk, lanes = r.shape
    nv = v.shape[1]
    tc = min(tc, t_len)
    kspec = pl.BlockSpec((tc, nk, LANES), lambda l, c: (c, 0, l))
    vspec = pl.BlockSpec((tc, nv, LANES), lambda l, c: (c, 0, l))
    sspec = pl.BlockSpec((nk, nv, LANES), lambda l, c: (0, 0, l))
    return pl.pallas_call(
        _wkv_kernel,
        grid=(lanes // LANES, t_len // tc),
        in_specs=[kspec] * 5 + [vspec, sspec],
        out_specs=[vspec, sspec],
        out_shape=[jax.ShapeDtypeStruct(v.shape, F32), jax.ShapeDtypeStruct(s0.shape, F32)],
        compiler_params=_cparams("parallel", "arbitrary"),
        name="wkv_scan",
    )(r, w, k, a, b, v, s0)


def _to_klanes(x, n_seq, seq_len, dup):
    h = x.shape[-1] // RW_HEAD
    y = x.reshape(n_seq, seq_len, h, RW_HEAD).transpose(1, 3, 0, 2).reshape(seq_len, RW_HEAD, n_seq * h)
    return jnp.concatenate([y] * dup, axis=-1) if dup > 1 else y


def _to_vlanes(x, n_seq, seq_len, dup):
    h = x.shape[-1] // RW_HEAD
    y = x.reshape(n_seq, seq_len, h, dup, RW_HEAD // dup).transpose(1, 4, 3, 0, 2)
    return y.reshape(seq_len, RW_HEAD // dup, dup * n_seq * h)


def _from_vlanes(y, n_seq, seq_len, dup):
    h = y.shape[-1] // (dup * n_seq)
    y = y.reshape(seq_len, RW_HEAD // dup, dup, n_seq, h).transpose(3, 0, 4, 2, 1)
    return y.reshape(n_seq * seq_len, h * RW_HEAD)


def _state_to_lanes(s, dup):
    n_seq, h = s.shape[:2]
    y = s.reshape(n_seq, h, dup, RW_HEAD // dup, RW_HEAD).transpose(4, 3, 2, 0, 1)
    return y.reshape(RW_HEAD, RW_HEAD // dup, dup * n_seq * h)


def _state_from_lanes(s, n_seq, dup):
    h = s.shape[-1] // (dup * n_seq)
    y = s.reshape(RW_HEAD, RW_HEAD // dup, dup, n_seq, h).transpose(3, 4, 2, 1, 0)
    return y.reshape(n_seq, h, RW_HEAD, RW_HEAD)


def _wkv(grp, r, w, k, v, a, b, s0):
    n_seq, seq_len = grp.n_seq, grp.seq_len
    heads = r.shape[-1] // RW_HEAD
    dup = max(1, LANES // (n_seq * heads))
    kl = [_to_klanes(t, n_seq, seq_len, dup) for t in (r, w, k, a, b)]
    y, s = _wkv_scan(*kl, _to_vlanes(v, n_seq, seq_len, dup), _state_to_lanes(s0, dup), tc=64)
    return _from_vlanes(y, n_seq, seq_len, dup), _state_from_lanes(s, n_seq, dup)


def _wkv_t_kernel(r_ref, w_ref, k_ref, a_ref, b_ref, v_ref, s0_ref, y_ref, s_ref, ks_sc, vs_sc, ys_sc):
    c = pl.program_id(0)
    n_seq, d, tc = r_ref.shape
    nk = RW_HEAD
    heads = d // nk
    dup = LANES // (n_seq * heads)
    nv = nk // dup

    @pl.when(c == 0)
    def _():
        s_ref[...] = s0_ref[...]

    def head_rows(ref, ch):
        return [ref[b, pl.ds(ch, heads, stride=nk), :] for b in range(n_seq)]

    def load_k(kk, carry):
        for n, ref in enumerate((r_ref, w_ref, k_ref, a_ref, b_ref)):
            ks_sc[n, kk] = jnp.concatenate(head_rows(ref, kk) * dup, axis=0).T
        return carry

    lax.fori_loop(0, nk, load_k, 0, unroll=4)

    def load_v(vr, carry):
        rows = []
        for part in range(dup):
            rows += head_rows(v_ref, part * nv + vr)
        vs_sc[pl.ds(vr, tc, stride=nv), :] = jnp.concatenate(rows, axis=0).T
        return carry

    lax.fori_loop(0, nv, load_v, 0, unroll=2)

    def tree(parts):
        return (parts[0] + parts[1]) + (parts[2] + parts[3])

    parts = [None] * 4
    for kk in range(nk):
        term = s_ref[kk] * ks_sc[3, kk, 0:1, :]
        parts[kk % 4] = term if parts[kk % 4] is None else parts[kk % 4] + term

    def step(t, sa):
        trow = pl.ds(t, 1)
        nrow = pl.ds(jnp.minimum(t + 1, tc - 1), 1)
        vt = vs_sc[pl.ds(pl.multiple_of(t * nv, nv), nv), :]
        ys = [None] * 4
        sn = [None] * 4
        for kk in range(nk):
            s_new = (s_ref[kk] * ks_sc[1, kk, trow, :] + sa * ks_sc[4, kk, trow, :]
                     + vt * ks_sc[2, kk, trow, :])
            s_ref[kk] = s_new
            ty = s_new * ks_sc[0, kk, trow, :]
            ts = s_new * ks_sc[3, kk, nrow, :]
            ys[kk % 4] = ty if ys[kk % 4] is None else ys[kk % 4] + ty
            sn[kk % 4] = ts if sn[kk % 4] is None else sn[kk % 4] + ts
        ys_sc[pl.ds(pl.multiple_of(t * nv, nv), nv), :] = tree(ys)
        return tree(sn)

    lax.fori_loop(0, tc, step, tree(parts))

    def store_y(vr, carry):
        yt = ys_sc[pl.ds(vr, tc, stride=nv), :].T
        for part in range(dup):
            for b in range(n_seq):
                r0 = (part * n_seq + b) * heads
                y_ref[b, pl.ds(part * nv + vr, heads, stride=nk), :] = yt[r0:r0 + heads]
        return carry

    lax.fori_loop(0, nv, store_y, 0, unroll=2)


WKV_CHUNK = 128


def _wkv_t(r, w, k, a, b, v, s0):
    n_seq, d, t_len = r.shape
    heads = d // RW_HEAD
    assert LANES % (n_seq * heads) == 0 and t_len % WKV_CHUNK == 0
    dup = LANES // (n_seq * heads)
    nv = RW_HEAD // dup
    tc = WKV_CHUNK
    one = pl.Buffered(1)
    xspec = pl.BlockSpec((n_seq, d, tc), lambda c: (0, 0, c), pipeline_mode=one)
    sspec = pl.BlockSpec((RW_HEAD, nv, LANES), lambda c: (0, 0, 0), pipeline_mode=one)
    y, s = pl.pallas_call(
        _wkv_t_kernel,
        grid=(t_len // tc,),
        in_specs=[xspec] * 6 + [sspec],
        out_specs=[pl.BlockSpec((n_seq, d, tc), lambda c: (0, 0, c)),
                   pl.BlockSpec((RW_HEAD, nv, LANES), lambda c: (0, 0, 0))],
        out_shape=[jax.ShapeDtypeStruct(r.shape, F32), jax.ShapeDtypeStruct((RW_HEAD, nv, LANES), F32)],
        scratch_shapes=[pltpu.VMEM((5, RW_HEAD, tc, LANES), F32), pltpu.VMEM((tc * nv, LANES), F32),
                        pltpu.VMEM((tc * nv, LANES), F32)],
        compiler_params=_cparams("arbitrary"),
        name="wkv_scan_t",
    )(r, w, k, a, b, v, _state_to_lanes(s0, dup))
    return y, _state_from_lanes(s, n_seq, dup)


def _rwkv_post_kernel(t_minor, x_ref, y_ref, bonus_ref, g_ref, gate_ref, vec_ref, ones_ref, wo_ref, o_ref):
    vec = vec_ref[...]
    lnx_w, lnx_b = vec[0:1, :], vec[1:2, :]
    y = y_ref[0].T if t_minor else y_ref[...]
    ones = ones_ref[...]
    inv = 1.0 / RW_HEAD
    mu = _seg_sum64(y, ones) * inv
    yc = y - mu
    var = _seg_sum64(yc * yc, ones) * inv
    yn = yc * lax.rsqrt(var + GN_EPS) * lnx_w + lnx_b
    out = jnp.dot(((yn + bonus_ref[...]) * g_ref[...]).astype(BF16), wo_ref[...], preferred_element_type=F32)
    o_ref[...] = x_ref[...] + gate_ref[0] * out


def _rwkv_post(grp, x, y, bonus, g, gate, P, j, t_minor):
    d = x.shape[-1]
    vec = _pad_to(jnp.stack([P['rw_lnx_w'][j], P['rw_lnx_b'][j]]), 0, 8)
    ones = _block_ones(MXU_DIM)
    wo = P['rw_w_o'][j].astype(BF16)
    tps = grp.tiles_per_seq
    yspec = (pl.BlockSpec((1, d, grp.tile), lambda i: (i // tps, 0, i % tps)) if t_minor
             else grp.row_spec(d))
    return pl.pallas_call(
        functools.partial(_rwkv_post_kernel, t_minor),
        grid=(grp.n_tiles,),
        in_specs=[grp.row_spec(d), yspec, grp.row_spec(d), grp.row_spec(d), grp.seq_spec(d),
                  _const_spec(vec.shape), _const_spec(ones.shape), _const_spec(wo.shape)],
        out_specs=grp.row_spec(d),
        out_shape=jax.ShapeDtypeStruct(x.shape, F32),
        compiler_params=_cparams("parallel"),
        name="rwkv_post",
    )(x, y, bonus, g, grp.seq_rows(gate), vec, ones, wo)


Q_W = ATT_HEADS * ATT_HEAD_DIM
KV_W = KV_HEADS * ATT_HEAD_DIM
IQ_W = IDX_HEADS * IDX_DIM
OFF_K = Q_W
OFF_V = Q_W + KV_W
OFF_QI = Q_W + 2 * KV_W
OFF_KI = OFF_QI + IQ_W


def _rope_tables(pos):
    rd = ATT_HEAD_DIM // ROT_FRAC
    half = rd // 2
    inv = ROPE_THETA ** (-jnp.arange(half, dtype=F32) * 2.0 / rd)
    ang = pos.astype(F32)[:, None] * inv[None, :]
    cos, sin = jnp.cos(ang), jnp.sin(ang)
    n = pos.shape[0]
    rest = ATT_HEAD_DIM - rd
    c = jnp.concatenate([cos, cos, jnp.ones((n, rest), F32)], axis=-1)
    s_up = jnp.concatenate([-sin, jnp.zeros((n, half + rest), F32)], axis=-1)
    s_dn = jnp.concatenate([jnp.zeros((n, half), F32), sin, jnp.zeros((n, rest), F32)], axis=-1)
    return [jnp.concatenate([t, t], axis=-1) for t in (c, s_up, s_dn)]


def _rope128(x, c, s_up, s_dn):
    half = ATT_HEAD_DIM // ROT_FRAC // 2
    return x * c + pltpu.roll(x, LANES - half, axis=1) * s_up + pltpu.roll(x, half, axis=1) * s_dn


def _dsa_proj_kernel(x_ref, g_ref, sc_ref, sh_ref, w_ref, c_ref, su_ref, sd_ref, qg_ref, kg_ref, ig_ref,
                     ones_ref, q_o, k_o, v_o, qi_o, kiwi_o):
    h = _rms_mod(x_ref[...], g_ref[...], sc_ref[0], sh_ref[0])
    z = jnp.dot(h.astype(BF16), w_ref[...], preferred_element_type=F32)
    c, s_up, s_dn = c_ref[...], su_ref[...], sd_ref[...]
    ones = ones_ref[...]
    inv = 1.0 / ATT_HEAD_DIM
    lane = lax.broadcasted_iota(I32, (z.shape[0], LANES), 1)
    low = lane < ATT_HEAD_DIM

    def head_norm(t, gain, blk=ones):
        return t * lax.rsqrt(_seg_sum64(t * t, blk) * inv + NORM_EPS) * gain

    def pairs(t):
        return [t[:, n * LANES:(n + 1) * LANES] for n in range(t.shape[-1] // LANES)]

    q = head_norm(z[:, :Q_W], qg_ref[...]) * (LOG2E * ATT_HEAD_DIM ** -0.5)
    for n, t in enumerate(pairs(q)):
        t = _rope128(t, c, s_up, s_dn).astype(BF16)
        q_o[0, 2 * n] = t[:, :ATT_HEAD_DIM]
        q_o[0, 2 * n + 1] = t[:, ATT_HEAD_DIM:]
    k = head_norm(z[:, OFF_K:OFF_V], kg_ref[...])
    k_o[...] = jnp.concatenate([_rope128(t, c, s_up, s_dn) for t in pairs(k)], axis=-1)
    v_o[...] = z[:, OFF_V:OFF_QI]
    qi = z[:, OFF_QI:OFF_KI] * (IDX_DIM ** -0.5)
    for n, t in enumerate(pairs(qi)):
        t = _rope128(t, c, s_up, s_dn)
        qi_o[0, 2 * n] = jnp.where(low, t, 0.0).astype(BF16)
        qi_o[0, 2 * n + 1] = jnp.where(low, pltpu.roll(t, ATT_HEAD_DIM, axis=1), 0.0).astype(BF16)
    kw = z[:, OFF_KI:OFF_KI + LANES]
    ki = _rope128(head_norm(kw, ig_ref[...], ones[:LANES, :LANES]), c, s_up, s_dn)
    kiwi_o[...] = jnp.where(low, ki, kw * (IDX_HEADS ** -0.5))


def _dsa_proj(grp, x, g, sc, sh, pos, P, j):
    d = x.shape[-1]
    w = _pad_to(P['att_w_in'][j], 1, LANES).astype(BF16)
    tabs = [grp.pos_rows(t) for t in _rope_tables(pos)]
    qg = jnp.tile(P['att_q_norm'][j], ATT_HEADS).reshape(1, Q_W)
    kg = jnp.tile(P['att_k_norm'][j], KV_HEADS).reshape(1, KV_W)
    ig = _pad_to(P['idx_k_norm'][j], 0, LANES).reshape(1, LANES)
    ones = _block_ones(MXU_DIM)
    nsb = grp.n_seq if grp.per_seq else grp.n_tiles
    rows = grp.seq_len if grp.per_seq else grp.tile
    tps = grp.tiles_per_seq
    tt = grp.tile

    def head_spec(nh, wd):
        return pl.BlockSpec((1, nh, tt, wd), lambda i: (i // tps, 0, i % tps, 0))

    return pl.pallas_call(
        _dsa_proj_kernel,
        grid=(grp.n_tiles,),
        in_specs=[grp.row_spec(d), _const_spec((1, d)), grp.seq_spec(d), grp.seq_spec(d), _const_spec(w.shape),
                  grp.pos_spec(LANES), grp.pos_spec(LANES), grp.pos_spec(LANES),
                  _const_spec(qg.shape), _const_spec(kg.shape), _const_spec(ig.shape), _const_spec(ones.shape)],
        out_specs=[head_spec(ATT_HEADS, ATT_HEAD_DIM), grp.row_spec(KV_W), grp.row_spec(KV_W),
                   head_spec(IDX_HEADS, LANES), grp.row_spec(LANES)],
        out_shape=[jax.ShapeDtypeStruct((nsb, ATT_HEADS, rows, ATT_HEAD_DIM), BF16),
                   jax.ShapeDtypeStruct((grp.rows, KV_W), F32),
                   jax.ShapeDtypeStruct((grp.rows, KV_W), F32),
                   jax.ShapeDtypeStruct((nsb, IDX_HEADS, rows, LANES), BF16),
                   jax.ShapeDtypeStruct((grp.rows, LANES), F32)],
        compiler_params=_cparams("parallel"),
        name="dsa_proj",
    )(x, g, grp.seq_rows(sc), grp.seq_rows(sh), w, *tabs, qg, kg, ig, ones)


def _score_key(score, admissible):
    bits = pltpu.bitcast(score + 0.0, I32)
    key = jnp.where(bits >= 0, bits, bits ^ 0x7FFFFFFF)
    return jnp.where(admissible, key, NEG_INF_KEY)


def _topk_threshold(count, shape, topk, idx_bits):
    kf = float(topk)
    theta = jnp.where(count(lambda k, i: k >= 0) >= kf, jnp.full(shape, 0, I32), jnp.full(shape, INT_MIN, I32))

    def value_bit(n, theta):
        cand = theta + jnp.left_shift(jnp.int32(1), 30 - n)
        return jnp.where(count(lambda k, i: k >= cand) >= kf, cand, theta)

    theta = lax.fori_loop(0, 31, value_bit, theta)
    need = kf - count(lambda k, i: k > theta)

    def index_bit(n, cut):
        cand = cut + jnp.left_shift(jnp.int32(1), idx_bits - 1 - n)
        below = count(lambda k, i: jnp.where(k == theta, i, cand) < cand)
        return jnp.where(below < need, cand, cut)

    surplus = count(lambda k, i: k == theta) - need
    tied = jnp.where(theta > NEG_INF_KEY, surplus, 0.0)
    cut = lax.cond(jnp.max(tied) > 0.0,
                   lambda: lax.fori_loop(0, idx_bits, index_bit, jnp.zeros(shape, I32)),
                   lambda: jnp.full(shape, 2 ** 30, I32))
    return theta, cut


def _select_bias(key, idx, theta, cut):
    tie = jnp.where(idx <= cut, 0.0, MASK_BIAS)
    bias = jnp.where(key > theta, 0.0, jnp.where(key == theta, tie, MASK_BIAS))
    return jnp.where(key > NEG_INF_KEY, bias, MASK_BIAS)


ATT_ROW_BLOCK = 32


def _dsa_prompt_kernel(topk, q_ref, qi_ref, kiwiq_ref, x_ref, gate_ref, k_ref, v_ref, kiwi_ref, place_ref,
                       wo_ref, o_ref, key_sc, bias_sc, qblk_sc, kb_sc, vb_sc, s_sc, p_sc, m_sc, l_sc, alpha_sc,
                       acc_sc):
    qb = pl.program_id(1)
    tq = x_ref.shape[0]
    assert tq == LANES
    n_chunks, kc, _ = key_sc.shape
    lt = kc // LANES
    nkc = (qb * tq + tq - 1) // kc + 1
    qpos = qb * tq + lax.broadcasted_iota(I32, (kc, LANES), 1)
    krow = lax.broadcasted_iota(I32, (kc, LANES), 0)

    qi = qi_ref[0].reshape(IDX_HEADS * tq, LANES)
    wi_t = kiwiq_ref[...].T

    def score_chunk(c, carry):
        kic = kiwi_ref[pl.ds(pl.multiple_of(c * kc, kc), kc), :]
        logits = _bdot_nt(kic, qi)
        acc = None
        for h in range(IDX_HEADS):
            t = jnp.maximum(logits[:, h * tq:(h + 1) * tq], 0.0) * wi_t[IDX_DIM + h:IDX_DIM + h + 1, :]
            acc = t if acc is None else acc + t
        key_sc[c] = _score_key(acc, c * kc + krow <= qpos)
        return carry

    lax.fori_loop(0, nkc, score_chunk, 0)

    def count(pred):
        def body(c, acc):
            w = jnp.where(pred(key_sc[c], c * kc + krow), 1.0, 0.0)
            for j in range(lt):
                acc = acc + w[j * LANES:(j + 1) * LANES, :]
            return acc
        acc = lax.fori_loop(0, nkc, body, jnp.zeros((LANES, LANES), F32))
        return jnp.sum(acc, axis=0, keepdims=True)

    theta, cut = _topk_threshold(count, (1, LANES), topk, (n_chunks * kc - 1).bit_length())

    def bias_chunk(c, carry):
        bias_t = _select_bias(key_sc[c], c * kc + krow, theta, cut)
        bias_sc[c] = jnp.concatenate([bias_t[j * LANES:(j + 1) * LANES, :].T for j in range(lt)], axis=-1)
        return carry

    lax.fori_loop(0, nkc, bias_chunk, 0)

    gq = GROUPS * tq
    for g in range(KV_HEADS):
        qg = q_ref[0, g * GROUPS:(g + 1) * GROUPS].reshape(gq, ATT_HEAD_DIM)
        qblk_sc[g] = jnp.dot(qg, place_ref[g], preferred_element_type=F32).astype(BF16)
    m_sc[...] = jnp.full_like(m_sc, MASK_BIAS)
    l_sc[...] = jnp.zeros_like(l_sc)
    acc_sc[...] = jnp.zeros_like(acc_sc)

    def att_chunk(c, carry):
        rows = pl.ds(pl.multiple_of(c * kc, kc), kc)
        kb_sc[...] = k_ref[rows, :].astype(BF16)
        vb_sc[...] = v_ref[rows, :].astype(BF16)

        for g in range(KV_HEADS):
            buf = g % 2
            base = g * gq
            s_sc[buf] = _bdot_nt(qblk_sc[g], kb_sc[...])
            for r0 in range(0, gq, ATT_ROW_BLOCK):
                t0 = r0 % tq
                rb = slice(r0, r0 + ATT_ROW_BLOCK)
                ms = slice(base + r0, base + r0 + ATT_ROW_BLOCK)
                mx = None
                for j in range(lt):
                    cols = slice(j * LANES, (j + 1) * LANES)
                    t = s_sc[buf, rb, cols] + bias_sc[c, t0:t0 + ATT_ROW_BLOCK, cols]
                    s_sc[buf, rb, cols] = t
                    mx = t if mx is None else jnp.maximum(mx, t)
                m_old = m_sc[ms, :]
                m_new = jnp.maximum(m_old, jnp.max(mx, axis=1, keepdims=True))
                alpha_sc[buf, rb, :] = jnp.exp2(m_old - m_new)
                m_sc[ms, :] = m_new
            for r0 in range(0, gq, ATT_ROW_BLOCK):
                rb = slice(r0, r0 + ATT_ROW_BLOCK)
                ms = slice(base + r0, base + r0 + ATT_ROW_BLOCK)
                m_new = m_sc[ms, :]
                tot = None
                for j in range(lt):
                    cols = slice(j * LANES, (j + 1) * LANES)
                    p = jnp.exp2(s_sc[buf, rb, cols] - m_new)
                    p_sc[buf, rb, cols] = p.astype(BF16)
                    tot = p if tot is None else tot + p
                l_sc[ms, :] = alpha_sc[buf, rb, :] * l_sc[ms, :] + jnp.sum(tot, axis=1, keepdims=True)
            a = alpha_sc[buf]
            rs = slice(base, base + gq)
            acc_sc[rs, :] = (acc_sc[rs, :] * jnp.concatenate([a] * (KV_W // LANES), axis=-1)
                             + jnp.dot(p_sc[buf], vb_sc[...], preferred_element_type=F32))
        return carry

    lax.fori_loop(0, nkc, att_chunk, 0)
    pieces = []
    for head in range(ATT_HEADS):
        g = head // GROUPS
        rs = slice(head * tq, (head + 1) * tq)
        inv_l = 1.0 / l_sc[rs, :ATT_HEAD_DIM]
        pieces.append(acc_sc[rs, g * ATT_HEAD_DIM:(g + 1) * ATT_HEAD_DIM] * inv_l)
    o = jnp.concatenate(pieces, axis=-1).astype(BF16)
    out = jnp.dot(o, wo_ref[...], preferred_element_type=F32)
    o_ref[...] = x_ref[...] + gate_ref[0] * out


def _head_placement():
    p = np.zeros((KV_HEADS, ATT_HEAD_DIM, KV_W), np.float32)
    for g in range(KV_HEADS):
        p[g, np.arange(ATT_HEAD_DIM), g * ATT_HEAD_DIM + np.arange(ATT_HEAD_DIM)] = 1.0
    return jnp.asarray(p, dtype=BF16)


def _dsa_prompt(n_seq, seq_len, x, gate, q, k, v, qi, kiwi, w_o):
    d = x.shape[-1]
    tq, kc = 128, 512
    assert seq_len % kc == 0
    topk = min(TOPK_MAX, seq_len // 4)
    nq = seq_len // tq
    gq = GROUPS * tq
    wo = w_o.astype(BF16)
    place = _head_placement()
    return pl.pallas_call(
        functools.partial(_dsa_prompt_kernel, topk),
        grid=(n_seq, nq),
        in_specs=[pl.BlockSpec((1, ATT_HEADS, tq, ATT_HEAD_DIM), lambda b, i: (b, 0, i, 0)),
                  pl.BlockSpec((1, IDX_HEADS, tq, LANES), lambda b, i: (b, 0, i, 0)),
                  pl.BlockSpec((tq, LANES), lambda b, i: (b * nq + i, 0)),
                  pl.BlockSpec((tq, d), lambda b, i: (b * nq + i, 0)),
                  pl.BlockSpec((1, 1, d), lambda b, i: (b, 0, 0)),
                  pl.BlockSpec((seq_len, KV_W), lambda b, i: (b, 0)),
                  pl.BlockSpec((seq_len, KV_W), lambda b, i: (b, 0)),
                  pl.BlockSpec((seq_len, LANES), lambda b, i: (b, 0)),
                  _const_spec(place.shape), _const_spec(wo.shape)],
        out_specs=pl.BlockSpec((tq, d), lambda b, i: (b * nq + i, 0)),
        out_shape=jax.ShapeDtypeStruct(x.shape, F32),
        scratch_shapes=[pltpu.VMEM((seq_len // kc, kc, tq), I32), pltpu.VMEM((seq_len // kc, tq, kc), F32),
                        pltpu.VMEM((KV_HEADS, gq, KV_W), BF16), pltpu.VMEM((kc, KV_W), BF16),
                        pltpu.VMEM((kc, KV_W), BF16), pltpu.VMEM((2, gq, kc), F32),
                        pltpu.VMEM((2, gq, kc), BF16), pltpu.VMEM((ATT_HEADS * tq, LANES), F32),
                        pltpu.VMEM((ATT_HEADS * tq, LANES), F32), pltpu.VMEM((2, gq, LANES), F32),
                        pltpu.VMEM((ATT_HEADS * tq, KV_W), F32)],
        compiler_params=_cparams("parallel", "arbitrary"),
        name="dsa_prompt",
    )(q, qi, kiwi, x, gate.reshape(n_seq, 1, d), k, v, kiwi, place, wo)


PAGES_PER_STEP = 16


def _dsa_sel_kernel(pt_ref, qi_ref, wib_ref, *refs):
    page_refs = refs[:PAGES_PER_STEP]
    kinew_ref, key_ref = refs[PAGES_PER_STEP:]
    p = pl.program_id(1)
    nsteps = pl.num_programs(1)
    npg = key_ref.shape[1] - 1
    t_new = key_ref.shape[2]
    qi = qi_ref[0]
    wib = wib_ref[0]

    def score(keys):
        t = jnp.maximum(_bdot(qi, keys), 0.0) * wib
        acc = t[0:t_new]
        for h in range(1, IDX_HEADS):
            acc = acc + t[h * t_new:(h + 1) * t_new]
        return acc

    for n, page_ref in enumerate(page_refs):
        key_ref[0, p * PAGES_PER_STEP + n] = _score_key(score(page_ref[0, 0]), True)

    @pl.when(p == nsteps - 1)
    def _():
        lane = lax.broadcasted_iota(I32, (t_new, LANES), 1)
        row = lax.broadcasted_iota(I32, (t_new, LANES), 0)
        key_ref[0, npg] = _score_key(score(kinew_ref[0]), lane <= row)


def _dsa_search_kernel(topk, key_ref, bias_ref):
    n_seq, n_slots, t_new, _ = key_ref.shape
    lane = lax.broadcasted_iota(I32, (n_seq, t_new, LANES), 2)

    def count(pred):
        def body(s, acc):
            return acc + jnp.where(pred(key_ref[:, s], s * LANES + lane), 1.0, 0.0)
        acc = lax.fori_loop(0, n_slots, body, jnp.zeros((n_seq, t_new, LANES), F32))
        return jnp.sum(acc, axis=2, keepdims=True)

    theta, cut = _topk_threshold(count, (n_seq, t_new, LANES), topk, (n_slots * LANES - 1).bit_length())

    def write(s, carry):
        bias_ref[:, s] = _select_bias(key_ref[:, s], s * LANES + lane, theta, cut)
        return carry

    lax.fori_loop(0, n_slots, write, 0)


def _page_spec(width, n, layer):
    return pl.BlockSpec((1, 1, width, PAGE),
                        lambda b, p, pt: (layer, pt[b, p * PAGES_PER_STEP + n], 0, 0))


def _dsa_sel(page_table, qi, wib, cache_ki, layer, kinew, t_new):
    n_seq, npg = page_table.shape
    assert npg % PAGES_PER_STEP == 0
    topk = min(TOPK_MAX, (npg * PAGE + t_new) // 4)
    rows = qi.shape[1]
    grid_spec = pltpu.PrefetchScalarGridSpec(
        num_scalar_prefetch=1,
        grid=(n_seq, npg // PAGES_PER_STEP),
        in_specs=[pl.BlockSpec((1, rows, IDX_DIM), lambda b, p, pt: (b, 0, 0)),
                  pl.BlockSpec((1, rows, LANES), lambda b, p, pt: (b, 0, 0))]
        + [_page_spec(IDX_DIM, n, layer) for n in range(PAGES_PER_STEP)]
        + [pl.BlockSpec((1, IDX_DIM, PAGE), lambda b, p, pt: (b, 0, 0))],
        out_specs=pl.BlockSpec((1, npg + 1, t_new, LANES), lambda b, p, pt: (b, 0, 0, 0)))
    keys = pl.pallas_call(
        _dsa_sel_kernel,
        grid_spec=grid_spec,
        out_shape=jax.ShapeDtypeStruct((n_seq, npg + 1, t_new, LANES), I32),
        compiler_params=_cparams("parallel", "arbitrary"),
        name="dsa_sel",
    )(page_table, qi, wib, *([cache_ki] * PAGES_PER_STEP), kinew)
    return pl.pallas_call(
        functools.partial(_dsa_search_kernel, topk),
        out_shape=jax.ShapeDtypeStruct(keys.shape, F32),
        compiler_params=pltpu.CompilerParams(vmem_limit_bytes=VMEM_LIMIT),
        name="dsa_search",
    )(keys)


def _dsa_att_kernel(pt_ref, q_ref, *refs):
    kpage_refs = refs[:PAGES_PER_STEP]
    vpage_refs = refs[PAGES_PER_STEP:2 * PAGES_PER_STEP]
    knew_ref, vnew_ref, bias_ref, biasnew_ref, o_ref, m_sc, l_sc, acc_sc = refs[2 * PAGES_PER_STEP:]
    p = pl.program_id(1)
    last = pl.num_programs(1) - 1
    rows = q_ref.shape[1]
    t_new = bias_ref.shape[2]

    @pl.when(p == 0)
    def _():
        m_sc[...] = jnp.full_like(m_sc, MASK_BIAS)
        l_sc[...] = jnp.zeros_like(l_sc)
        acc_sc[...] = jnp.zeros_like(acc_sc)

    def attend(kv_bias):
        q = q_ref[0]
        ss = []
        for kk, _, bias in kv_bias:
            s = _bdot(q, kk)
            ss.append((s.reshape(rows // t_new, t_new, LANES) + bias[None]).reshape(rows, LANES))
        mx = ss[0]
        for s in ss[1:]:
            mx = jnp.maximum(mx, s)
        m_old = m_sc[...]
        m_new = jnp.maximum(m_old, jnp.max(mx, axis=1, keepdims=True))
        alpha = jnp.exp2(m_old - m_new)
        tot = None
        pv = None
        for s, (_, vv, _) in zip(ss, kv_bias):
            pr = jnp.exp2(s - m_new)
            tot = pr if tot is None else tot + pr
            t = _bdot_nt(pr, vv)
            pv = t if pv is None else pv + t
        l_sc[...] = alpha * l_sc[...] + jnp.sum(tot, axis=1, keepdims=True)
        acc_sc[...] = alpha * acc_sc[...] + pv
        m_sc[...] = m_new

    attend([(kpage_refs[n][0, 0], vpage_refs[n][0, 0], bias_ref[0, n]) for n in range(PAGES_PER_STEP)])

    @pl.when(p == last)
    def _():
        attend([(knew_ref[0], vnew_ref[0], biasnew_ref[0, 0])])
        o_ref[0] = acc_sc[...] / l_sc[...]


def _dsa_att(page_table, qblk, cache_k, cache_v, layer, knew, vnew, bias):
    n_seq, npg = page_table.shape
    rows = qblk.shape[1]
    t_new = bias.shape[2]
    grid_spec = pltpu.PrefetchScalarGridSpec(
        num_scalar_prefetch=1,
        grid=(n_seq, npg // PAGES_PER_STEP),
        in_specs=[pl.BlockSpec((1, rows, KV_W), lambda b, p, pt: (b, 0, 0))]
        + [_page_spec(KV_W, n, layer) for n in range(PAGES_PER_STEP)] * 2
        + [pl.BlockSpec((1, KV_W, PAGE), lambda b, p, pt: (b, 0, 0)),
           pl.BlockSpec((1, KV_W, PAGE), lambda b, p, pt: (b, 0, 0)),
           pl.BlockSpec((1, PAGES_PER_STEP, t_new, LANES), lambda b, p, pt: (b, p, 0, 0)),
           pl.BlockSpec((1, 1, t_new, LANES), lambda b, p, pt: (b, npg, 0, 0))],
        out_specs=pl.BlockSpec((1, rows, KV_W), lambda b, p, pt: (b, 0, 0)),
        scratch_shapes=[pltpu.VMEM((rows, 1), F32), pltpu.VMEM((rows, 1), F32), pltpu.VMEM((rows, KV_W), F32)])
    return pl.pallas_call(
        _dsa_att_kernel,
        grid_spec=grid_spec,
        out_shape=jax.ShapeDtypeStruct((n_seq, rows, KV_W), F32),
        compiler_params=_cparams("parallel", "arbitrary"),
        name="dsa_att",
    )(page_table, qblk, *([cache_k] * PAGES_PER_STEP), *([cache_v] * PAGES_PER_STEP), knew, vnew, bias, bias)


def _linear_res_kernel(x_ref, a_ref, gate_ref, w_ref, o_ref):
    o_ref[...] = x_ref[...] + gate_ref[0] * _bdot(a_ref[...], w_ref[...])


def _linear_res(grp, x, a, gate, w):
    d = x.shape[-1]
    wb = w.astype(BF16)
    return pl.pallas_call(
        _linear_res_kernel,
        grid=(grp.n_tiles,),
        in_specs=[grp.row_spec(d), grp.row_spec(a.shape[-1]), grp.seq_spec(d), _const_spec(wb.shape)],
        out_specs=grp.row_spec(d),
        out_shape=jax.ShapeDtypeStruct(x.shape, F32),
        compiler_params=_cparams("parallel"),
        name="linear_res",
    )(x, a, grp.seq_rows(gate), wb)


def _dsa_sample(grp, x, gate, q, k, v, qi, kiwi, cache_k, cache_v, cache_ki, layer, page_table, w_o):
    n_seq, t_new = grp.n_seq, grp.seq_len
    assert t_new <= PAGE and grp.n_tiles == 1

    def new_page(t):
        return _pad_to(t.reshape(n_seq, t_new, t.shape[-1]), 1, PAGE).transpose(0, 2, 1)

    qi_b = qi[0, :, :, :IDX_DIM].reshape(IDX_HEADS, n_seq, t_new, IDX_DIM).transpose(1, 0, 2, 3)
    qi_b = qi_b.reshape(n_seq, IDX_HEADS * t_new, IDX_DIM)
    wi = kiwi[:, IDX_DIM:IDX_DIM + IDX_HEADS].reshape(n_seq, t_new, IDX_HEADS).transpose(0, 2, 1)
    wib = jnp.broadcast_to(wi.reshape(n_seq, IDX_HEADS * t_new, 1), (n_seq, IDX_HEADS * t_new, LANES))
    bias = _dsa_sel(page_table, qi_b, wib, cache_ki, layer, new_page(kiwi[:, :IDX_DIM]), t_new)
    q_b = q[0].reshape(KV_HEADS, GROUPS, n_seq, t_new, ATT_HEAD_DIM).transpose(2, 0, 1, 3, 4)
    eye = jnp.eye(KV_HEADS, dtype=q_b.dtype)
    qblk = (q_b[:, :, :, :, None, :] * eye[None, :, None, None, :, None])
    qblk = qblk.reshape(n_seq, ATT_HEADS * t_new, KV_W)
    o = _dsa_att(page_table, qblk, cache_k, cache_v, layer, new_page(k), new_page(v), bias)
    o = o.reshape(n_seq, KV_HEADS, GROUPS, t_new, KV_HEADS, ATT_HEAD_DIM)
    o = jnp.stack([o[:, g, :, :, g, :] for g in range(KV_HEADS)], axis=1)
    o = o.transpose(0, 3, 1, 2, 4).reshape(n_seq * t_new, Q_W)
    return _linear_res(grp, x, o, gate, w_o)


TOKEN_TILE = 256
FFN_TILE = 1024


def _trunk(x, mods, pos, wkv0, shift0, attn_fn, P):
    n_seq, seq_len, d = x.shape
    depth = mods.shape[0]
    grp = _Group(n_seq, seq_len, TOKEN_TILE)
    grp_ffn = _Group(n_seq, seq_len, FFN_TILE)
    x = x.reshape(n_seq * seq_len, d)
    v_first = None
    ks, vs, kis, wkvs, shifts = [], [], [], [], []
    for i in range(depth):
        j = i // 2
        sh_a, sc_a, g_a, sh_f, sc_f, g_f = jnp.split(mods[i], 6, axis=-1)
        g_att = P['norm_g'][i, 0].reshape(1, d)
        g_ffn = P['norm_g'][i, 1].reshape(1, d)
        if i % 2 == 0:
            t_minor = grp.per_seq and seq_len % WKV_CHUNK == 0 and LANES % (n_seq * (d // RW_HEAD)) == 0
            outs = _rwkv_proj(grp, x, g_att, sc_a, sh_a, shift0[j], P, j, v_first, t_minor)
            v, gate, bonus, h = outs[-4:]
            if j == 0:
                v_first = v
            if t_minor:
                y, state = _wkv_t(*outs[:6], wkv0[j])
            else:
                r, w, k, a, b = outs[:5]
                y, state = _wkv(grp, r, w, k, v, a, b, wkv0[j])
            x = _rwkv_post(grp, x, y, bonus, gate, g_a, P, j, t_minor)
            wkvs.append(state)
            shifts.append(h.reshape(n_seq, seq_len, d)[:, -1])
        else:
            q, k, v, qi, kiwi = _dsa_proj(grp, x, g_att, sc_a, sh_a, pos, P, j)
            x = attn_fn(j, grp, x, g_a, q, k, v, qi, kiwi)
            ks.append(k.reshape(n_seq, seq_len, KV_HEADS, ATT_HEAD_DIM))
            vs.append(v.reshape(n_seq, seq_len, KV_HEADS, ATT_HEAD_DIM))
            kis.append(kiwi[:, :IDX_DIM].reshape(n_seq, seq_len, IDX_DIM))
        x = _ffn(grp_ffn, x, g_ffn, sc_f, sh_f, g_f, P['w_up'][i], P['w_down'][i])
    return (x.reshape(n_seq, seq_len, d), jnp.stack(ks), jnp.stack(vs), jnp.stack(kis),
            jnp.stack(wkvs), jnp.stack(shifts))


def kernel(x_prompt, x_sample, cache_k, cache_v, cache_idx_k, state_wkv, state_shift, page_table,
           c_prompt, c_sample, norm_g, w_ada, b_ada, w_up, w_down, rw_mix, rw_w_rkv, rw_w_o,
           rw_w0, rw_w1, rw_w2, rw_a0, rw_a1, rw_a2, rw_v0, rw_v1, rw_v2, rw_g1, rw_g2,
           rw_k_k, rw_k_a, rw_r_k, rw_lnx_w, rw_lnx_b, att_w_in, att_w_o, att_q_norm,
           att_k_norm, idx_k_norm):
    P = dict(norm_g=norm_g, w_up=w_up.astype(BF16), w_down=w_down.astype(BF16),
             rw_mix=rw_mix, rw_w_rkv=rw_w_rkv, rw_w_o=rw_w_o, rw_w0=rw_w0, rw_w1=rw_w1,
             rw_w2=rw_w2, rw_a0=rw_a0, rw_a1=rw_a1, rw_a2=rw_a2, rw_v0=rw_v0, rw_v1=rw_v1,
             rw_v2=rw_v2, rw_g1=rw_g1, rw_g2=rw_g2, rw_k_k=rw_k_k, rw_k_a=rw_k_a, rw_r_k=rw_r_k,
             rw_lnx_w=rw_lnx_w, rw_lnx_b=rw_lnx_b, att_w_in=att_w_in, att_w_o=att_w_o,
             att_q_norm=att_q_norm, att_k_norm=att_k_norm, idx_k_norm=idx_k_norm)
    n_p, seq, d = x_prompt.shape
    n_s, dec_seq, _ = x_sample.shape
    n_rwkv = state_wkv.shape[0]
    heads = d // RW_HEAD
    past_len = page_table.shape[1] * PAGE

    mods = _ada(jnp.concatenate([c_prompt, c_sample], axis=0), w_ada, b_ada)
    n_dsa, pool = cache_k.shape[:2]
    pages_k = cache_k.transpose(0, 1, 3, 4, 2).reshape(n_dsa, pool, KV_W, PAGE)
    pages_v = cache_v.transpose(0, 1, 3, 4, 2).reshape(n_dsa, pool, KV_W, PAGE)
    pages_ki = cache_idx_k.transpose(0, 1, 3, 2)

    def prompt_attn(j, grp, x, gate, q, k, v, qi, kiwi):
        return _dsa_prompt(grp.n_seq, grp.seq_len, x, gate, q, k, v, qi, kiwi, att_w_o[j])

    def sample_attn(j, grp, x, gate, q, k, v, qi, kiwi):
        return _dsa_sample(grp, x, gate, q, k, v, qi, kiwi, pages_k, pages_v, pages_ki, j,
                           page_table, att_w_o[j])

    wkv0 = jnp.zeros((n_rwkv, n_p, heads, RW_HEAD, RW_HEAD), F32)
    shift0 = jnp.zeros((n_rwkv, n_p, d), F32)
    y_p, k_p, v_p, ki_p, wkv_p, shift_p = _trunk(
        x_prompt, mods[:, :n_p], jnp.arange(seq), wkv0, shift0, prompt_attn, P)
    y_s, k_s, v_s, ki_s, wkv_s, shift_s = _trunk(
        x_sample, mods[:, n_p:], past_len + jnp.arange(dec_seq), state_wkv, state_shift, sample_attn, P)
    return (y_p, y_s, k_p, v_p, ki_p, wkv_p, shift_p, k_s, v_s, ki_s, wkv_s, shift_s)
```

```python
import functools

import jax
import jax.numpy as jnp
import numpy as np
from jax import lax
from jax.experimental import pallas as pl
from jax.experimental.pallas import tpu as pltpu

F32 = jnp.float32
BF16 = jnp.bfloat16
I32 = jnp.int32

NORM_EPS = 1e-6
GN_EPS = 64e-5
RW_HEAD = 64
ATT_HEADS = 16
ATT_HEAD_DIM = 64
KV_HEADS = 4
GROUPS = ATT_HEADS // KV_HEADS
IDX_HEADS = 8
IDX_DIM = 64
TOPK_MAX = 256
ROPE_THETA = 500000.0
ROT_FRAC = 4
PAGE = 128

LANES = 128
MXU_DIM = 256
VMEM_LIMIT = 56 * 1024 * 1024

LOG2E = 1.4426950408889634
INT_MIN = -(2 ** 31)
NEG_INF_KEY = INT_MIN + 0x7FFFFF
MASK_BIAS = -1e30


def _cparams(*sem):
    return pltpu.CompilerParams(dimension_semantics=sem, vmem_limit_bytes=VMEM_LIMIT)


def _const_spec(shape):
    n = len(shape)
    return pl.BlockSpec(shape, lambda *_: (0,) * n, pipeline_mode=pl.Buffered(1))


def _bdot(a, b):
    return jnp.dot(a.astype(BF16), b.astype(BF16), preferred_element_type=F32)


def _bdot_nt(a, b):
    return lax.dot_general(a.astype(BF16), b.astype(BF16), (((1,), (1,)), ((), ())),
                           preferred_element_type=F32)


def _rms_mod(x, g, sc, sh):
    ms = jnp.mean(x * x, axis=-1, keepdims=True)
    return x * lax.rsqrt(ms + NORM_EPS) * g * (1.0 + sc) + sh


def _seg_sum64(x, ones):
    hi = x.astype(BF16)
    lo = (x - hi.astype(F32)).astype(BF16)
    w = ones.shape[0]
    outs = []
    for c in range(x.shape[-1] // w):
        sl = slice(c * w, (c + 1) * w)
        outs.append(jnp.dot(hi[:, sl], ones, preferred_element_type=F32)
                    + jnp.dot(lo[:, sl], ones, preferred_element_type=F32))
    return outs[0] if len(outs) == 1 else jnp.concatenate(outs, axis=-1)


def _block_ones(width):
    i = np.arange(width) // 64
    return jnp.asarray((i[:, None] == i[None, :]).astype(np.float32), dtype=BF16)


def _ada_kernel(c_ref, w_ref, b_ref, o_ref):
    c = c_ref[...]
    s = c * jax.nn.sigmoid(c)
    o_ref[0] = _bdot(s, w_ref[0]) + b_ref[0]


def _ada(c_all, w_ada, b_ada):
    depth, d, d6 = w_ada.shape
    nb = c_all.shape[0]
    tn = 1536
    return pl.pallas_call(
        _ada_kernel,
        grid=(depth, d6 // tn),
        in_specs=[pl.BlockSpec((nb, d), lambda i, j: (0, 0)),
                  pl.BlockSpec((1, d, tn), lambda i, j: (i, 0, j)),
                  pl.BlockSpec((1, 1, tn), lambda i, j: (i, 0, j))],
        out_specs=pl.BlockSpec((1, nb, tn), lambda i, j: (i, 0, j)),
        out_shape=jax.ShapeDtypeStruct((depth, nb, d6), F32),
        compiler_params=_cparams("parallel", "parallel"),
        name="ada",
    )(c_all, w_ada, b_ada.reshape(depth, 1, d6))


class _Group:
    def __init__(self, n_seq, seq_len, tile):
        self.n_seq, self.seq_len = n_seq, seq_len
        self.rows = n_seq * seq_len
        self.tile = min(tile, self.rows)
        assert self.rows % self.tile == 0
        assert seq_len % self.tile == 0 or self.tile % seq_len == 0
        self.tiles_per_seq = max(seq_len // self.tile, 1)
        self.per_seq = seq_len >= self.tile
        self.n_tiles = self.rows // self.tile

    def seq_rows(self, m):
        d = m.shape[-1]
        if self.per_seq:
            return m.reshape(self.n_seq, 1, d)
        return jnp.repeat(m, self.seq_len, axis=0).reshape(self.n_tiles, self.tile, d)

    def seq_spec(self, d):
        r = 1 if self.per_seq else self.tile
        tps = self.tiles_per_seq
        return pl.BlockSpec((1, r, d), lambda i, *_: (i // tps, 0, 0))

    def pos_rows(self, tab):
        if self.per_seq:
            return tab
        return jnp.tile(tab, (self.tile // self.seq_len, 1))

    def pos_spec(self, w):
        tps = self.tiles_per_seq
        return pl.BlockSpec((self.tile, w), lambda i, *_: (i % tps, 0))

    def row_spec(self, w):
        return pl.BlockSpec((self.tile, w), lambda i, *_: (i, 0))


def _ffn_kernel(x_ref, g_ref, sc_ref, sh_ref, gate_ref, wu_ref, wd_ref, o_ref, h_sc, acc_sc):
    j = pl.program_id(1)

    @pl.when(j == 0)
    def _():
        h_sc[...] = _rms_mod(x_ref[...], g_ref[...], sc_ref[0], sh_ref[0]).astype(BF16)
        acc_sc[...] = jnp.zeros_like(acc_sc)

    u = jnp.maximum(jnp.dot(h_sc[...], wu_ref[...], preferred_element_type=F32), 0.0)
    acc_sc[...] += jnp.dot((u * u).astype(BF16), wd_ref[...], preferred_element_type=F32)

    @pl.when(j == pl.num_programs(1) - 1)
    def _():
        o_ref[...] = x_ref[...] + gate_ref[0] * acc_sc[...]


def _ffn(grp, x, g, sc, sh, gate, w_up, w_down):
    d, dff = w_up.shape
    tf = 512
    tm = grp.tile
    return pl.pallas_call(
        _ffn_kernel,
        grid=(grp.n_tiles, dff // tf),
        in_specs=[grp.row_spec(d), pl.BlockSpec((1, d), lambda i, j: (0, 0)),
                  grp.seq_spec(d), grp.seq_spec(d), grp.seq_spec(d),
                  pl.BlockSpec((d, tf), lambda i, j: (0, j)),
                  pl.BlockSpec((tf, d), lambda i, j: (j, 0))],
        out_specs=grp.row_spec(d),
        out_shape=jax.ShapeDtypeStruct(x.shape, F32),
        scratch_shapes=[pltpu.VMEM((tm, d), BF16), pltpu.VMEM((tm, d), F32)],
        compiler_params=_cparams("parallel", "arbitrary"),
        name="ffn",
    )(x, g, grp.seq_rows(sc), grp.seq_rows(sh), grp.seq_rows(gate), w_up, w_down)


def _rwkv_proj_kernel(has_vfirst, t_minor, seq_len, *refs):
    (x_ref, g_ref, sc_ref, sh_ref, shift_ref, mix_ref, wr_ref, wk_ref, wv_ref,
     w1_ref, w2_ref, a1_ref, a2_ref, g1_ref, g2_ref, vec_ref, ones_ref) = refs[:17]
    refs = refs[17:]
    if has_vfirst:
        v1_ref, v2_ref, vf_ref = refs[:3]
        refs = refs[3:]
    scan_o = refs[:6] if t_minor else refs[:5]
    v_o, g_o, bonus_o, h_o, carry_sc = refs[len(scan_o):]
    i = pl.program_id(0)
    tt = x_ref.shape[0]

    @pl.when(i == 0)
    def _():
        carry_sc[...] = jnp.zeros_like(carry_sc)

    h = _rms_mod(x_ref[...], g_ref[...], sc_ref[0], sh_ref[0])
    h_o[...] = h
    row = lax.broadcasted_iota(I32, h.shape, 0)
    hp = pltpu.roll(h, 1, axis=0)
    hp = jnp.where(row == 0, carry_sc[...], hp)
    hp = jnp.where(((row + i * tt) & (seq_len - 1)) == 0, shift_ref[0], hp)
    carry_sc[...] = h[tt - 1:tt, :]
    dx = hp - h
    mix = mix_ref[...]
    xr, xw, xk, xv, xa, xg = [h + dx * mix[n:n + 1, :] for n in range(6)]
    vec = vec_ref[...]
    w0, a0, k_k, k_a, v0, r_k = [vec[n:n + 1, :] for n in range(6)]
    ones = ones_ref[...]

    r = jnp.dot(xr.astype(BF16), wr_ref[...], preferred_element_type=F32)
    k = jnp.dot(xk.astype(BF16), wk_ref[...], preferred_element_type=F32)
    v = jnp.dot(xv.astype(BF16), wv_ref[...], preferred_element_type=F32)
    w_pre = w0 + _bdot(jnp.tanh(_bdot(xw, w1_ref[...])), w2_ref[...])
    decay = jnp.exp(-float(np.exp(-0.5)) * jax.nn.sigmoid(w_pre))
    if has_vfirst:
        vgate = jax.nn.sigmoid(v0 + _bdot(_bdot(xv, v1_ref[...]), v2_ref[...]))
        v = v + (vf_ref[...] - v) * vgate
    a = jax.nn.sigmoid(a0 + _bdot(_bdot(xa, a1_ref[...]), a2_ref[...]))
    g_o[...] = _bdot(jax.nn.sigmoid(_bdot(xg, g1_ref[...])), g2_ref[...])
    kk = k * k_k
    kk = kk * lax.rsqrt(jnp.maximum(_seg_sum64(kk * kk, ones), 1e-24))
    k = k * (1.0 + (a - 1.0) * k_a)
    v_o[...] = v
    bonus_o[...] = _seg_sum64(r * k * r_k, ones) * v
    scan = (r, decay, k, -kk, kk * a)
    if t_minor:
        for o_ref, t in zip(scan_o, scan + (v,)):
            o_ref[0] = t.T
    else:
        for o_ref, t in zip(scan_o, scan):
            o_ref[...] = t


def _pad_to(w, axis, mult):
    n = w.shape[axis]
    pad = (-n) % mult
    if pad == 0:
        return w
    cfg = [(0, 0)] * w.ndim
    cfg[axis] = (0, pad)
    return jnp.pad(w, cfg)


def _lora_pair(w_in, w_out):
    return (_pad_to(w_in, 1, LANES).astype(BF16), _pad_to(w_out, 0, LANES).astype(BF16))


def _rwkv_proj(grp, x, g, sc, sh, shift, P, j, v_first, t_minor):
    d = x.shape[-1]
    assert grp.seq_len & (grp.seq_len - 1) == 0, "sequence-start test uses a bit mask"
    has_vfirst = v_first is not None
    w1, w2 = _lora_pair(P['rw_w1'][j], P['rw_w2'][j])
    a1, a2 = _lora_pair(P['rw_a1'][j], P['rw_a2'][j])
    g1, g2 = _lora_pair(P['rw_g1'][j], P['rw_g2'][j])
    v0 = P['rw_v0'][j - 1] if has_vfirst else jnp.zeros((d,), F32)
    vec = jnp.stack([P['rw_w0'][j], P['rw_a0'][j], P['rw_k_k'][j], P['rw_k_a'][j], v0,
                     P['rw_r_k'][j].reshape(d), jnp.zeros((d,), F32), jnp.zeros((d,), F32)])
    mix = _pad_to(P['rw_mix'][j], 0, 8)
    wrkv = P['rw_w_rkv'][j].astype(BF16)
    ones = _block_ones(MXU_DIM)
    args = [x, g, grp.seq_rows(sc), grp.seq_rows(sh), grp.seq_rows(shift), mix,
            wrkv[0], wrkv[1], wrkv[2], w1, w2, a1, a2, g1, g2, vec, ones]
    specs = [grp.row_spec(d), _const_spec((1, d)), grp.seq_spec(d), grp.seq_spec(d), grp.seq_spec(d),
             _const_spec(mix.shape)] + [_const_spec(a.shape) for a in args[6:]]
    if has_vfirst:
        v1, v2 = _lora_pair(P['rw_v1'][j - 1], P['rw_v2'][j - 1])
        args += [v1, v2, v_first]
        specs += [_const_spec(v1.shape), _const_spec(v2.shape), grp.row_spec(d)]
    out = jax.ShapeDtypeStruct(x.shape, F32)
    if t_minor:
        assert grp.per_seq and grp.tile % LANES == 0
        tps = grp.tiles_per_seq
        scan_specs = [pl.BlockSpec((1, d, grp.tile), lambda i: (i // tps, 0, i % tps))] * 6
        scan_shapes = [jax.ShapeDtypeStruct((grp.n_seq, d, grp.seq_len), F32)] * 6
    else:
        scan_specs = [grp.row_spec(d)] * 5
        scan_shapes = [out] * 5
    return pl.pallas_call(
        functools.partial(_rwkv_proj_kernel, has_vfirst, t_minor, grp.seq_len),
        grid=(grp.n_tiles,),
        in_specs=specs,
        out_specs=scan_specs + [grp.row_spec(d)] * 4,
        out_shape=scan_shapes + [out] * 4,
        scratch_shapes=[pltpu.VMEM((1, d), F32)],
        compiler_params=_cparams("arbitrary"),
        name="rwkv_proj",
    )(*args)


def _wkv_kernel(r_ref, w_ref, k_ref, a_ref, b_ref, v_ref, s0_ref, y_ref, s_ref):
    c = pl.program_id(1)
    tc, nk, _ = r_ref.shape

    @pl.when(c == 0)
    def _():
        s_ref[...] = s0_ref[...]

    def step(t, carry):
        vt = v_ref[t]
        parts = [None] * 4
        for kk in range(nk):
            term = s_ref[kk] * a_ref[t, pl.ds(kk, 1), :]
            parts[kk % 4] = term if parts[kk % 4] is None else parts[kk % 4] + term
        sa = (parts[0] + parts[1]) + (parts[2] + parts[3])
        parts = [None] * 4
        for kk in range(nk):
            s_new = (s_ref[kk] * w_ref[t, pl.ds(kk, 1), :] + sa * b_ref[t, pl.ds(kk, 1), :]
                     + vt * k_ref[t, pl.ds(kk, 1), :])
            s_ref[kk] = s_new
            term = s_new * r_ref[t, pl.ds(kk, 1), :]
            parts[kk % 4] = term if parts[kk % 4] is None else parts[kk % 4] + term
        y_ref[t] = (parts[0] + parts[1]) + (parts[2] + parts[3])
        return carry

    lax.fori_loop(0, tc, step, 0)


def _wkv_scan(r, w, k, a, b, v, s0, tc):
    t_len, nk, lanes = r.shape
    nv = v.shape[1]
    tc = min(tc, t_len)
    kspec = pl.BlockSpec((tc, nk, LANES), lambda l, c: (c, 0, l))
    vspec = pl.BlockSpec((tc, nv, LANES), lambda l, c: (c, 0, l))
    sspec = pl.BlockSpec((nk, nv, LANES), lambda l, c: (0, 0, l))
    return pl.pallas_call(
        _wkv_kernel,
        grid=(lanes // LANES, t_len // tc),
        in_specs=[kspec] * 5 + [vspec, sspec],
        out_specs=[vspec, sspec],
        out_shape=[jax.ShapeDtypeStruct(v.shape, F32), jax.ShapeDtypeStruct(s0.shape, F32)],
        compiler_params=_cparams("parallel", "arbitrary"),
        name="wkv_scan",
    )(r, w, k, a, b, v, s0)


def _to_klanes(x, n_seq, seq_len, dup):
    h = x.shape[-1] // RW_HEAD
    y = x.reshape(n_seq, seq_len, h, RW_HEAD).transpose(1, 3, 0, 2).reshape(seq_len, RW_HEAD, n_seq * h)
    return jnp.concatenate([y] * dup, axis=-1) if dup > 1 else y


def _to_vlanes(x, n_seq, seq_len, dup):
    h = x.shape[-1] // RW_HEAD
    y = x.reshape(n_seq, seq_len, h, dup, RW_HEAD // dup).transpose(1, 4, 3, 0, 2)
    return y.reshape(seq_len, RW_HEAD // dup, dup * n_seq * h)


def _from_vlanes(y, n_seq, seq_len, dup):
    h = y.shape[-1] // (dup * n_seq)
    y = y.reshape(seq_len, RW_HEAD // dup, dup, n_seq, h).transpose(3, 0, 4, 2, 1)
    return y.reshape(n_seq * seq_len, h * RW_HEAD)


def _state_to_lanes(s, dup):
    n_seq, h = s.shape[:2]
    y = s.reshape(n_seq, h, dup, RW_HEAD // dup, RW_HEAD).transpose(4, 3, 2, 0, 1)
    return y.reshape(RW_HEAD, RW_HEAD // dup, dup * n_seq * h)


def _state_from_lanes(s, n_seq, dup):
    h = s.shape[-1] // (dup * n_seq)
    y = s.reshape(RW_HEAD, RW_HEAD // dup, dup, n_seq, h).transpose(3, 4, 2, 1, 0)
    return y.reshape(n_seq, h, RW_HEAD, RW_HEAD)


def _wkv(grp, r, w, k, v, a, b, s0):
    n_seq, seq_len = grp.n_seq, grp.seq_len
    heads = r.shape[-1] // RW_HEAD
    dup = max(1, LANES // (n_seq * heads))
    kl = [_to_klanes(t, n_seq, seq_len, dup) for t in (r, w, k, a, b)]
    y, s = _wkv_scan(*kl, _to_vlanes(v, n_seq, seq_len, dup), _state_to_lanes(s0, dup), tc=64)
    return _from_vlanes(y, n_seq, seq_len, dup), _state_from_lanes(s, n_seq, dup)


def _wkv_t_kernel(r_ref, w_ref, k_ref, a_ref, b_ref, v_ref, s0_ref, y_ref, s_ref, ks_sc, vs_sc, ys_sc):
    c = pl.program_id(0)
    n_seq, d, tc = r_ref.shape
    nk = RW_HEAD
    heads = d // nk
    dup = LANES // (n_seq * heads)
    nv = nk // dup

    @pl.when(c == 0)
    def _():
        s_ref[...] = s0_ref[...]

    def head_rows(ref, ch):
        return [ref[b, pl.ds(ch, heads, stride=nk), :] for b in range(n_seq)]

    def load_k(kk, carry):
        for n, ref in enumerate((r_ref, w_ref, k_ref, a_ref, b_ref)):
            ks_sc[n, kk] = jnp.concatenate(head_rows(ref, kk) * dup, axis=0).T
        return carry

    lax.fori_loop(0, nk, load_k, 0, unroll=4)

    def load_v(vr, carry):
        rows = []
        for part in range(dup):
            rows += head_rows(v_ref, part * nv + vr)
        vs_sc[pl.ds(vr, tc, stride=nv), :] = jnp.concatenate(rows, axis=0).T
        return carry

    lax.fori_loop(0, nv, load_v, 0, unroll=2)

    def tree(parts):
        return (parts[0] + parts[1]) + (parts[2] + parts[3])

    parts = [None] * 4
    for kk in range(nk):
        term = s_ref[kk] * ks_sc[3, kk, 0:1, :]
        parts[kk % 4] = term if parts[kk % 4] is None else parts[kk % 4] + term

    def step(t, sa):
        trow = pl.ds(t, 1)
        nrow = pl.ds(jnp.minimum(t + 1, tc - 1), 1)
        vt = vs_sc[pl.ds(pl.multiple_of(t * nv, nv), nv), :]
        ys = [None] * 4
        sn = [None] * 4
        for kk in range(nk):
            s_new = (s_ref[kk] * ks_sc[1, kk, trow, :] + sa * ks_sc[4, kk, trow, :]
                     + vt * ks_sc[2, kk, trow, :])
            s_ref[kk] = s_new
            ty = s_new * ks_sc[0, kk, trow, :]
            ts = s_new * ks_sc[3, kk, nrow, :]
            ys[kk % 4] = ty if ys[kk % 4] is None else ys[kk % 4] + ty
            sn[kk % 4] = ts if sn[kk % 4] is None else sn[kk % 4] + ts
        ys_sc[pl.ds(pl.multiple_of(t * nv, nv), nv), :] = tree(ys)
        return tree(sn)

    lax.fori_loop(0, tc, step, tree(parts))

    def store_y(vr, carry):
        yt = ys_sc[pl.ds(vr, tc, stride=nv), :].T
        for part in range(dup):
            for b in range(n_seq):
                r0 = (part * n_seq + b) * heads
                y_ref[b, pl.ds(part * nv + vr, heads, stride=nk), :] = yt[r0:r0 + heads]
        return carry

    lax.fori_loop(0, nv, store_y, 0, unroll=2)


WKV_CHUNK = 128


def _wkv_t(r, w, k, a, b, v, s0):
    n_seq, d, t_len = r.shape
    heads = d // RW_HEAD
    assert LANES % (n_seq * heads) == 0 and t_len % WKV_CHUNK == 0
    dup = LANES // (n_seq * heads)
    nv = RW_HEAD // dup
    tc = WKV_CHUNK
    one = pl.Buffered(1)
    xspec = pl.BlockSpec((n_seq, d, tc), lambda c: (0, 0, c), pipeline_mode=one)
    sspec = pl.BlockSpec((RW_HEAD, nv, LANES), lambda c: (0, 0, 0), pipeline_mode=one)
    y, s = pl.pallas_call(
        _wkv_t_kernel,
        grid=(t_len // tc,),
        in_specs=[xspec] * 6 + [sspec],
        out_specs=[pl.BlockSpec((n_seq, d, tc), lambda c: (0, 0, c)),
                   pl.BlockSpec((RW_HEAD, nv, LANES), lambda c: (0, 0, 0))],
        out_shape=[jax.ShapeDtypeStruct(r.shape, F32), jax.ShapeDtypeStruct((RW_HEAD, nv, LANES), F32)],
        scratch_shapes=[pltpu.VMEM((5, RW_HEAD, tc, LANES), F32), pltpu.VMEM((tc * nv, LANES), F32),
                        pltpu.VMEM((tc * nv, LANES), F32)],
        compiler_params=_cparams("arbitrary"),
        name="wkv_scan_t",
    )(r, w, k, a, b, v, _state_to_lanes(s0, dup))
    return y, _state_from_lanes(s, n_seq, dup)


def _rwkv_post_kernel(t_minor, x_ref, y_ref, bonus_ref, g_ref, gate_ref, vec_ref, ones_ref, wo_ref, o_ref):
    vec = vec_ref[...]
    lnx_w, lnx_b = vec[0:1, :], vec[1:2, :]
    y = y_ref[0].T if t_minor else y_ref[...]
    ones = ones_ref[...]
    inv = 1.0 / RW_HEAD
    mu = _seg_sum64(y, ones) * inv
    yc = y - mu
    var = _seg_sum64(yc * yc, ones) * inv
    yn = yc * lax.rsqrt(var + GN_EPS) * lnx_w + lnx_b
    out = jnp.dot(((yn + bonus_ref[...]) * g_ref[...]).astype(BF16), wo_ref[...], preferred_element_type=F32)
    o_ref[...] = x_ref[...] + gate_ref[0] * out


def _rwkv_post(grp, x, y, bonus, g, gate, P, j, t_minor):
    d = x.shape[-1]
    vec = _pad_to(jnp.stack([P['rw_lnx_w'][j], P['rw_lnx_b'][j]]), 0, 8)
    ones = _block_ones(MXU_DIM)
    wo = P['rw_w_o'][j].astype(BF16)
    tps = grp.tiles_per_seq
    yspec = (pl.BlockSpec((1, d, grp.tile), lambda i: (i // tps, 0, i % tps)) if t_minor
             else grp.row_spec(d))
    return pl.pallas_call(
        functools.partial(_rwkv_post_kernel, t_minor),
        grid=(grp.n_tiles,),
        in_specs=[grp.row_spec(d), yspec, grp.row_spec(d), grp.row_spec(d), grp.seq_spec(d),
                  _const_spec(vec.shape), _const_spec(ones.shape), _const_spec(wo.shape)],
        out_specs=grp.row_spec(d),
        out_shape=jax.ShapeDtypeStruct(x.shape, F32),
        compiler_params=_cparams("parallel"),
        name="rwkv_post",
    )(x, y, bonus, g, grp.seq_rows(gate), vec, ones, wo)


Q_W = ATT_HEADS * ATT_HEAD_DIM
KV_W = KV_HEADS * ATT_HEAD_DIM
IQ_W = IDX_HEADS * IDX_DIM
OFF_K = Q_W
OFF_V = Q_W + KV_W
OFF_QI = Q_W + 2 * KV_W
OFF_KI = OFF_QI + IQ_W


def _rope_tables(pos):
    rd = ATT_HEAD_DIM // ROT_FRAC
    half = rd // 2
    inv = ROPE_THETA ** (-jnp.arange(half, dtype=F32) * 2.0 / rd)
    ang = pos.astype(F32)[:, None] * inv[None, :]
    cos, sin = jnp.cos(ang), jnp.sin(ang)
    n = pos.shape[0]
    rest = ATT_HEAD_DIM - rd
    c = jnp.concatenate([cos, cos, jnp.ones((n, rest), F32)], axis=-1)
    s_up = jnp.concatenate([-sin, jnp.zeros((n, half + rest), F32)], axis=-1)
    s_dn = jnp.concatenate([jnp.zeros((n, half), F32), sin, jnp.zeros((n, rest), F32)], axis=-1)
    return [jnp.concatenate([t, t], axis=-1) for t in (c, s_up, s_dn)]


def _rope128(x, c, s_up, s_dn):
    half = ATT_HEAD_DIM // ROT_FRAC // 2
    return x * c + pltpu.roll(x, LANES - half, axis=1) * s_up + pltpu.roll(x, half, axis=1) * s_dn


def _dsa_proj_kernel(x_ref, g_ref, sc_ref, sh_ref, w_ref, c_ref, su_ref, sd_ref, qg_ref, kg_ref, ig_ref,
                     ones_ref, q_o, k_o, v_o, qi_o, kiwi_o, *cache_o):
    h = _rms_mod(x_ref[...], g_ref[...], sc_ref[0], sh_ref[0])
    z = jnp.dot(h.astype(BF16), w_ref[...], preferred_element_type=F32)
    c, s_up, s_dn = c_ref[...], su_ref[...], sd_ref[...]
    ones = ones_ref[...]
    inv = 1.0 / ATT_HEAD_DIM
    lane = lax.broadcasted_iota(I32, (z.shape[0], LANES), 1)
    low = lane < ATT_HEAD_DIM

    def head_norm(t, gain, blk=ones):
        return t * lax.rsqrt(_seg_sum64(t * t, blk) * inv + NORM_EPS) * gain

    def pairs(t):
        return [t[:, n * LANES:(n + 1) * LANES] for n in range(t.shape[-1] // LANES)]

    q = head_norm(z[:, :Q_W], qg_ref[...]) * (LOG2E * ATT_HEAD_DIM ** -0.5)
    for n, t in enumerate(pairs(q)):
        t = _rope128(t, c, s_up, s_dn).astype(BF16)
        q_o[0, 2 * n] = t[:, :ATT_HEAD_DIM]
        q_o[0, 2 * n + 1] = t[:, ATT_HEAD_DIM:]
    k = head_norm(z[:, OFF_K:OFF_V], kg_ref[...])
    k = jnp.concatenate([_rope128(t, c, s_up, s_dn) for t in pairs(k)], axis=-1)
    v = z[:, OFF_V:OFF_QI]
    k_o[...] = k
    v_o[...] = v
    qi = z[:, OFF_QI:OFF_KI] * (IDX_DIM ** -0.5)
    for n, t in enumerate(pairs(qi)):
        t = _rope128(t, c, s_up, s_dn)
        qi_o[0, 2 * n] = jnp.where(low, t, 0.0).astype(BF16)
        qi_o[0, 2 * n + 1] = jnp.where(low, pltpu.roll(t, ATT_HEAD_DIM, axis=1), 0.0).astype(BF16)
    kw = z[:, OFF_KI:OFF_KI + LANES]
    ki = _rope128(head_norm(kw, ig_ref[...], ones[:LANES, :LANES]), c, s_up, s_dn)
    kiwi_o[...] = jnp.where(low, ki, kw * (IDX_HEADS ** -0.5))
    if cache_o:
        kt_o, vt_o, kit_o = cache_o
        kt_o[0] = k.T
        vt_o[0] = v.T
        kit_o[0] = ki.T[:IDX_DIM, :]


def _dsa_proj(grp, x, g, sc, sh, pos, P, j):
    d = x.shape[-1]
    w = _pad_to(P['att_w_in'][j], 1, LANES).astype(BF16)
    tabs = [grp.pos_rows(t) for t in _rope_tables(pos)]
    qg = jnp.tile(P['att_q_norm'][j], ATT_HEADS).reshape(1, Q_W)
    kg = jnp.tile(P['att_k_norm'][j], KV_HEADS).reshape(1, KV_W)
    ig = _pad_to(P['idx_k_norm'][j], 0, LANES).reshape(1, LANES)
    ones = _block_ones(MXU_DIM)
    nsb = grp.n_seq if grp.per_seq else grp.n_tiles
    rows = grp.seq_len if grp.per_seq else grp.tile
    tps = grp.tiles_per_seq
    tt = grp.tile

    def head_spec(nh, wd):
        return pl.BlockSpec((1, nh, tt, wd), lambda i: (i // tps, 0, i % tps, 0))

    def feat_spec(wd):
        return pl.BlockSpec((1, wd, tt), lambda i: (i // tps, 0, i % tps))

    feature_major = grp.per_seq and tt % LANES == 0
    cache_specs = [feat_spec(KV_W), feat_spec(KV_W), feat_spec(IDX_DIM)] if feature_major else []
    cache_shapes = [jax.ShapeDtypeStruct((nsb, wd, rows), F32) for wd in (KV_W, KV_W, IDX_DIM)] if feature_major else []
    return pl.pallas_call(
        _dsa_proj_kernel,
        grid=(grp.n_tiles,),
        in_specs=[grp.row_spec(d), _const_spec((1, d)), grp.seq_spec(d), grp.seq_spec(d), _const_spec(w.shape),
                  grp.pos_spec(LANES), grp.pos_spec(LANES), grp.pos_spec(LANES),
                  _const_spec(qg.shape), _const_spec(kg.shape), _const_spec(ig.shape), _const_spec(ones.shape)],
        out_specs=[head_spec(ATT_HEADS, ATT_HEAD_DIM), grp.row_spec(KV_W), grp.row_spec(KV_W),
                   head_spec(IDX_HEADS, LANES), grp.row_spec(LANES)] + cache_specs,
        out_shape=[jax.ShapeDtypeStruct((nsb, ATT_HEADS, rows, ATT_HEAD_DIM), BF16),
                   jax.ShapeDtypeStruct((grp.rows, KV_W), F32),
                   jax.ShapeDtypeStruct((grp.rows, KV_W), F32),
                   jax.ShapeDtypeStruct((nsb, IDX_HEADS, rows, LANES), BF16),
                   jax.ShapeDtypeStruct((grp.rows, LANES), F32)] + cache_shapes,
        compiler_params=_cparams("parallel"),
        name="dsa_proj",
    )(x, g, grp.seq_rows(sc), grp.seq_rows(sh), w, *tabs, qg, kg, ig, ones)


def _score_key(score, admissible):
    bits = pltpu.bitcast(score + 0.0, I32)
    key = jnp.where(bits >= 0, bits, bits ^ 0x7FFFFFFF)
    return jnp.where(admissible, key, NEG_INF_KEY)


def _topk_threshold(count, shape, topk, idx_bits):
    kf = float(topk)
    theta = jnp.where(count(lambda k, i: k >= 0) >= kf, jnp.full(shape, 0, I32), jnp.full(shape, INT_MIN, I32))

    def value_bit(n, theta):
        cand = theta + jnp.left_shift(jnp.int32(1), 30 - n)
        return jnp.where(count(lambda k, i: k >= cand) >= kf, cand, theta)

    theta = lax.fori_loop(0, 31, value_bit, theta)
    need = kf - count(lambda k, i: k > theta)

    def index_bit(n, cut):
        cand = cut + jnp.left_shift(jnp.int32(1), idx_bits - 1 - n)
        below = count(lambda k, i: jnp.where(k == theta, i, cand) < cand)
        return jnp.where(below < need, cand, cut)

    surplus = count(lambda k, i: k == theta) - need
    tied = jnp.where(theta > NEG_INF_KEY, surplus, 0.0)
    cut = lax.cond(jnp.max(tied) > 0.0,
                   lambda: lax.fori_loop(0, idx_bits, index_bit, jnp.zeros(shape, I32)),
                   lambda: jnp.full(shape, 2 ** 30, I32))
    return theta, cut


def _select_bias(key, idx, theta, cut):
    tie = jnp.where(idx <= cut, 0.0, MASK_BIAS)
    bias = jnp.where(key > theta, 0.0, jnp.where(key == theta, tie, MASK_BIAS))
    return jnp.where(key > NEG_INF_KEY, bias, MASK_BIAS)


ATT_ROW_BLOCK = 32


def _dsa_prompt_kernel(topk, q_ref, qi_ref, kiwiq_ref, x_ref, gate_ref, k_ref, v_ref, kiwi_ref, place_ref,
                       wo_ref, o_ref, key_sc, bias_sc, qblk_sc, kb_sc, vb_sc, s_sc, p_sc, m_sc, l_sc, alpha_sc,
                       acc_sc):
    qb = pl.program_id(1)
    tq = x_ref.shape[0]
    assert tq == LANES
    n_chunks, kc, _ = key_sc.shape
    lt = kc // LANES
    nkc = (qb * tq + tq - 1) // kc + 1
    qpos = qb * tq + lax.broadcasted_iota(I32, (kc, LANES), 1)
    krow = lax.broadcasted_iota(I32, (kc, LANES), 0)

    qi = qi_ref[0].reshape(IDX_HEADS * tq, LANES)
    wi_t = kiwiq_ref[...].T

    def score_chunk(c, carry):
        kic = kiwi_ref[pl.ds(pl.multiple_of(c * kc, kc), kc), :]
        logits = _bdot_nt(kic, qi)
        acc = None
        for h in range(IDX_HEADS):
            t = jnp.maximum(logits[:, h * tq:(h + 1) * tq], 0.0) * wi_t[IDX_DIM + h:IDX_DIM + h + 1, :]
            acc = t if acc is None else acc + t
        key_sc[c] = _score_key(acc, c * kc + krow <= qpos)
        return carry

    lax.fori_loop(0, nkc, score_chunk, 0)

    def count(pred):
        def body(c, acc):
            w = jnp.where(pred(key_sc[c], c * kc + krow), 1.0, 0.0)
            for j in range(lt):
                acc = acc + w[j * LANES:(j + 1) * LANES, :]
            return acc
        acc = lax.fori_loop(0, nkc, body, jnp.zeros((LANES, LANES), F32))
        return jnp.sum(acc, axis=0, keepdims=True)

    theta, cut = _topk_threshold(count, (1, LANES), topk, (n_chunks * kc - 1).bit_length())

    def bias_chunk(c, carry):
        bias_t = _select_bias(key_sc[c], c * kc + krow, theta, cut)
        bias_sc[c] = jnp.concatenate([bias_t[j * LANES:(j + 1) * LANES, :].T for j in range(lt)], axis=-1)
        return carry

    lax.fori_loop(0, nkc, bias_chunk, 0)

    gq = GROUPS * tq
    for g in range(KV_HEADS):
        qg = q_ref[0, g * GROUPS:(g + 1) * GROUPS].reshape(gq, ATT_HEAD_DIM)
        qblk_sc[g] = jnp.dot(qg, place_ref[g], preferred_element_type=F32).astype(BF16)
    m_sc[...] = jnp.full_like(m_sc, MASK_BIAS)
    l_sc[...] = jnp.zeros_like(l_sc)
    acc_sc[...] = jnp.zeros_like(acc_sc)

    def att_chunk(c, carry):
        rows = pl.ds(pl.multiple_of(c * kc, kc), kc)
        kb_sc[...] = k_ref[rows, :].astype(BF16)
        vb_sc[...] = v_ref[rows, :].astype(BF16)

        for g in range(KV_HEADS):
            buf = g % 2
            base = g * gq
            s_sc[buf] = _bdot_nt(qblk_sc[g], kb_sc[...])
            for r0 in range(0, gq, ATT_ROW_BLOCK):
                t0 = r0 % tq
                rb = slice(r0, r0 + ATT_ROW_BLOCK)
                ms = slice(base + r0, base + r0 + ATT_ROW_BLOCK)
                mx = None
                for j in range(lt):
                    cols = slice(j * LANES, (j + 1) * LANES)
                    t = s_sc[buf, rb, cols] + bias_sc[c, t0:t0 + ATT_ROW_BLOCK, cols]
                    s_sc[buf, rb, cols] = t
                    mx = t if mx is None else jnp.maximum(mx, t)
                m_old = m_sc[ms, :]
                m_new = jnp.maximum(m_old, jnp.max(mx, axis=1, keepdims=True))
                alpha_sc[buf, rb, :] = jnp.exp2(m_old - m_new)
                m_sc[ms, :] = m_new
            for r0 in range(0, gq, ATT_ROW_BLOCK):
                rb = slice(r0, r0 + ATT_ROW_BLOCK)
                ms = slice(base + r0, base + r0 + ATT_ROW_BLOCK)
                m_new = m_sc[ms, :]
                tot = None
                for j in range(lt):
                    cols = slice(j * LANES, (j + 1) * LANES)
                    p = jnp.exp2(s_sc[buf, rb, cols] - m_new)
                    p_sc[buf, rb, cols] = p.astype(BF16)
                    tot = p if tot is None else tot + p
                l_sc[ms, :] = alpha_sc[buf, rb, :] * l_sc[ms, :] + jnp.sum(tot, axis=1, keepdims=True)
            a = alpha_sc[buf]
            rs = slice(base, base + gq)
            acc_sc[rs, :] = (acc_sc[rs, :] * jnp.concatenate([a] * (KV_W // LANES), axis=-1)
                             + jnp.dot(p_sc[buf], vb_sc[...], preferred_element_type=F32))
        return carry

    lax.fori_loop(0, nkc, att_chunk, 0)
    pieces = []
    for head in range(ATT_HEADS):
        g = head // GROUPS
        rs = slice(head * tq, (head + 1) * tq)
        inv_l = 1.0 / l_sc[rs, :ATT_HEAD_DIM]
        pieces.append(acc_sc[rs, g * ATT_HEAD_DIM:(g + 1) * ATT_HEAD_DIM] * inv_l)
    o = jnp.concatenate(pieces, axis=-1).astype(BF16)
    out = jnp.dot(o, wo_ref[...], preferred_element_type=F32)
    o_ref[...] = x_ref[...] + gate_ref[0] * out


def _head_placement():
    p = np.zeros((KV_HEADS, ATT_HEAD_DIM, KV_W), np.float32)
    for g in range(KV_HEADS):
        p[g, np.arange(ATT_HEAD_DIM), g * ATT_HEAD_DIM + np.arange(ATT_HEAD_DIM)] = 1.0
    return jnp.asarray(p, dtype=BF16)


def _dsa_prompt(n_seq, seq_len, x, gate, q, k, v, qi, kiwi, w_o):
    d = x.shape[-1]
    tq, kc = 128, 512
    assert seq_len % kc == 0
    topk = min(TOPK_MAX, seq_len // 4)
    nq = seq_len // tq
    gq = GROUPS * tq
    wo = w_o.astype(BF16)
    place = _head_placement()
    return pl.pallas_call(
        functools.partial(_dsa_prompt_kernel, topk),
        grid=(n_seq, nq),
        in_specs=[pl.BlockSpec((1, ATT_HEADS, tq, ATT_HEAD_DIM), lambda b, i: (b, 0, i, 0)),
                  pl.BlockSpec((1, IDX_HEADS, tq, LANES), lambda b, i: (b, 0, i, 0)),
                  pl.BlockSpec((tq, LANES), lambda b, i: (b * nq + i, 0)),
                  pl.BlockSpec((tq, d), lambda b, i: (b * nq + i, 0)),
                  pl.BlockSpec((1, 1, d), lambda b, i: (b, 0, 0)),
                  pl.BlockSpec((seq_len, KV_W), lambda b, i: (b, 0)),
                  pl.BlockSpec((seq_len, KV_W), lambda b, i: (b, 0)),
                  pl.BlockSpec((seq_len, LANES), lambda b, i: (b, 0)),
                  _const_spec(place.shape), _const_spec(wo.shape)],
        out_specs=pl.BlockSpec((tq, d), lambda b, i: (b * nq + i, 0)),
        out_shape=jax.ShapeDtypeStruct(x.shape, F32),
        scratch_shapes=[pltpu.VMEM((seq_len // kc, kc, tq), I32), pltpu.VMEM((seq_len // kc, tq, kc), F32),
                        pltpu.VMEM((KV_HEADS, gq, KV_W), BF16), pltpu.VMEM((kc, KV_W), BF16),
                        pltpu.VMEM((kc, KV_W), BF16), pltpu.VMEM((2, gq, kc), F32),
                        pltpu.VMEM((2, gq, kc), BF16), pltpu.VMEM((ATT_HEADS * tq, LANES), F32),
                        pltpu.VMEM((ATT_HEADS * tq, LANES), F32), pltpu.VMEM((2, gq, LANES), F32),
                        pltpu.VMEM((ATT_HEADS * tq, KV_W), F32)],
        compiler_params=_cparams("parallel", "arbitrary"),
        name="dsa_prompt",
    )(q, qi, kiwi, x, gate.reshape(n_seq, 1, d), k, v, kiwi, place, wo)


PAGES_PER_STEP = 16


def _dsa_sel_kernel(pt_ref, qi_ref, wib_ref, *refs):
    page_refs = refs[:PAGES_PER_STEP]
    kinew_ref, key_ref = refs[PAGES_PER_STEP:]
    p = pl.program_id(1)
    nsteps = pl.num_programs(1)
    npg = key_ref.shape[1] - 1
    t_new = key_ref.shape[2]
    qi = qi_ref[0]
    wib = wib_ref[0]

    def score(keys):
        t = jnp.maximum(_bdot(qi, keys), 0.0) * wib
        acc = t[0:t_new]
        for h in range(1, IDX_HEADS):
            acc = acc + t[h * t_new:(h + 1) * t_new]
        return acc

    for n, page_ref in enumerate(page_refs):
        key_ref[0, p * PAGES_PER_STEP + n] = _score_key(score(page_ref[0, 0]), True)

    @pl.when(p == nsteps - 1)
    def _():
        lane = lax.broadcasted_iota(I32, (t_new, LANES), 1)
        row = lax.broadcasted_iota(I32, (t_new, LANES), 0)
        key_ref[0, npg] = _score_key(score(kinew_ref[0]), lane <= row)


def _dsa_search_kernel(topk, key_ref, bias_ref):
    n_seq, n_slots, t_new, _ = key_ref.shape
    lane = lax.broadcasted_iota(I32, (n_seq, t_new, LANES), 2)

    def count(pred):
        def body(s, acc):
            return acc + jnp.where(pred(key_ref[:, s], s * LANES + lane), 1.0, 0.0)
        acc = lax.fori_loop(0, n_slots, body, jnp.zeros((n_seq, t_new, LANES), F32))
        return jnp.sum(acc, axis=2, keepdims=True)

    theta, cut = _topk_threshold(count, (n_seq, t_new, LANES), topk, (n_slots * LANES - 1).bit_length())

    def write(s, carry):
        bias_ref[:, s] = _select_bias(key_ref[:, s], s * LANES + lane, theta, cut)
        return carry

    lax.fori_loop(0, n_slots, write, 0)


def _page_spec(width, n, layer):
    return pl.BlockSpec((1, 1, width, PAGE),
                        lambda b, p, pt: (layer, pt[b, p * PAGES_PER_STEP + n], 0, 0))


def _dsa_sel(page_table, qi, wib, cache_ki, layer, kinew, t_new):
    n_seq, npg = page_table.shape
    assert npg % PAGES_PER_STEP == 0
    topk = min(TOPK_MAX, (npg * PAGE + t_new) // 4)
    rows = qi.shape[1]
    grid_spec = pltpu.PrefetchScalarGridSpec(
        num_scalar_prefetch=1,
        grid=(n_seq, npg // PAGES_PER_STEP),
        in_specs=[pl.BlockSpec((1, rows, IDX_DIM), lambda b, p, pt: (b, 0, 0)),
                  pl.BlockSpec((1, rows, LANES), lambda b, p, pt: (b, 0, 0))]
        + [_page_spec(IDX_DIM, n, layer) for n in range(PAGES_PER_STEP)]
        + [pl.BlockSpec((1, IDX_DIM, PAGE), lambda b, p, pt: (b, 0, 0))],
        out_specs=pl.BlockSpec((1, npg + 1, t_new, LANES), lambda b, p, pt: (b, 0, 0, 0)))
    keys = pl.pallas_call(
        _dsa_sel_kernel,
        grid_spec=grid_spec,
        out_shape=jax.ShapeDtypeStruct((n_seq, npg + 1, t_new, LANES), I32),
        compiler_params=_cparams("parallel", "arbitrary"),
        name="dsa_sel",
    )(page_table, qi, wib, *([cache_ki] * PAGES_PER_STEP), kinew)
    return pl.pallas_call(
        functools.partial(_dsa_search_kernel, topk),
        out_shape=jax.ShapeDtypeStruct(keys.shape, F32),
        compiler_params=pltpu.CompilerParams(vmem_limit_bytes=VMEM_LIMIT),
        name="dsa_search",
    )(keys)


def _dsa_att_kernel(pt_ref, q_ref, *refs):
    kpage_refs = refs[:PAGES_PER_STEP]
    vpage_refs = refs[PAGES_PER_STEP:2 * PAGES_PER_STEP]
    knew_ref, vnew_ref, bias_ref, biasnew_ref, o_ref, m_sc, l_sc, acc_sc = refs[2 * PAGES_PER_STEP:]
    p = pl.program_id(1)
    last = pl.num_programs(1) - 1
    rows = q_ref.shape[1]
    t_new = bias_ref.shape[2]

    @pl.when(p == 0)
    def _():
        m_sc[...] = jnp.full_like(m_sc, MASK_BIAS)
        l_sc[...] = jnp.zeros_like(l_sc)
        acc_sc[...] = jnp.zeros_like(acc_sc)

    def attend(kv_bias):
        q = q_ref[0]
        ss = []
        for kk, _, bias in kv_bias:
            s = _bdot(q, kk)
            ss.append((s.reshape(rows // t_new, t_new, LANES) + bias[None]).reshape(rows, LANES))
        mx = ss[0]
        for s in ss[1:]:
            mx = jnp.maximum(mx, s)
        m_old = m_sc[...]
        m_new = jnp.maximum(m_old, jnp.max(mx, axis=1, keepdims=True))
        alpha = jnp.exp2(m_old - m_new)
        tot = None
        pv = None
        for s, (_, vv, _) in zip(ss, kv_bias):
            pr = jnp.exp2(s - m_new)
            tot = pr if tot is None else tot + pr
            t = _bdot_nt(pr, vv)
            pv = t if pv is None else pv + t
        l_sc[...] = alpha * l_sc[...] + jnp.sum(tot, axis=1, keepdims=True)
        acc_sc[...] = alpha * acc_sc[...] + pv
        m_sc[...] = m_new

    attend([(kpage_refs[n][0, 0], vpage_refs[n][0, 0], bias_ref[0, n]) for n in range(PAGES_PER_STEP)])

    @pl.when(p == last)
    def _():
        attend([(knew_ref[0], vnew_ref[0], biasnew_ref[0, 0])])
        o_ref[0] = acc_sc[...] / l_sc[...]


def _dsa_att(page_table, qblk, cache_k, cache_v, layer, knew, vnew, bias):
    n_seq, npg = page_table.shape
    rows = qblk.shape[1]
    t_new = bias.shape[2]
    grid_spec = pltpu.PrefetchScalarGridSpec(
        num_scalar_prefetch=1,
        grid=(n_seq, npg // PAGES_PER_STEP),
        in_specs=[pl.BlockSpec((1, rows, KV_W), lambda b, p, pt: (b, 0, 0))]
        + [_page_spec(KV_W, n, layer) for n in range(PAGES_PER_STEP)] * 2
        + [pl.BlockSpec((1, KV_W, PAGE), lambda b, p, pt: (b, 0, 0)),
           pl.BlockSpec((1, KV_W, PAGE), lambda b, p, pt: (b, 0, 0)),
           pl.BlockSpec((1, PAGES_PER_STEP, t_new, LANES), lambda b, p, pt: (b, p, 0, 0)),
           pl.BlockSpec((1, 1, t_new, LANES), lambda b, p, pt: (b, npg, 0, 0))],
        out_specs=pl.BlockSpec((1, rows, KV_W), lambda b, p, pt: (b, 0, 0)),
        scratch_shapes=[pltpu.VMEM((rows, 1), F32), pltpu.VMEM((rows, 1), F32), pltpu.VMEM((rows, KV_W), F32)])
    return pl.pallas_call(
        _dsa_att_kernel,
        grid_spec=grid_spec,
        out_shape=jax.ShapeDtypeStruct((n_seq, rows, KV_W), F32),
        compiler_params=_cparams("parallel", "arbitrary"),
        name="dsa_att",
    )(page_table, qblk, *([cache_k] * PAGES_PER_STEP), *([cache_v] * PAGES_PER_STEP), knew, vnew, bias, bias)


def _linear_res_kernel(x_ref, a_ref, gate_ref, w_ref, o_ref):
    o_ref[...] = x_ref[...] + gate_ref[0] * _bdot(a_ref[...], w_ref[...])


def _linear_res(grp, x, a, gate, w):
    d = x.shape[-1]
    wb = w.astype(BF16)
    return pl.pallas_call(
        _linear_res_kernel,
        grid=(grp.n_tiles,),
        in_specs=[grp.row_spec(d), grp.row_spec(a.shape[-1]), grp.seq_spec(d), _const_spec(wb.shape)],
        out_specs=grp.row_spec(d),
        out_shape=jax.ShapeDtypeStruct(x.shape, F32),
        compiler_params=_cparams("parallel"),
        name="linear_res",
    )(x, a, grp.seq_rows(gate), wb)


def _dsa_sample(grp, x, gate, q, k, v, qi, kiwi, cache_k, cache_v, cache_ki, layer, page_table, w_o):
    n_seq, t_new = grp.n_seq, grp.seq_len
    assert t_new <= PAGE and grp.n_tiles == 1

    def new_page(t):
        return _pad_to(t.reshape(n_seq, t_new, t.shape[-1]), 1, PAGE).transpose(0, 2, 1)

    qi_b = qi[0, :, :, :IDX_DIM].reshape(IDX_HEADS, n_seq, t_new, IDX_DIM).transpose(1, 0, 2, 3)
    qi_b = qi_b.reshape(n_seq, IDX_HEADS * t_new, IDX_DIM)
    wi = kiwi[:, IDX_DIM:IDX_DIM + IDX_HEADS].reshape(n_seq, t_new, IDX_HEADS).transpose(0, 2, 1)
    wib = jnp.broadcast_to(wi.reshape(n_seq, IDX_HEADS * t_new, 1), (n_seq, IDX_HEADS * t_new, LANES))
    bias = _dsa_sel(page_table, qi_b, wib, cache_ki, layer, new_page(kiwi[:, :IDX_DIM]), t_new)
    q_b = q[0].reshape(KV_HEADS, GROUPS, n_seq, t_new, ATT_HEAD_DIM).transpose(2, 0, 1, 3, 4)
    eye = jnp.eye(KV_HEADS, dtype=q_b.dtype)
    qblk = (q_b[:, :, :, :, None, :] * eye[None, :, None, None, :, None])
    qblk = qblk.reshape(n_seq, ATT_HEADS * t_new, KV_W)
    o = _dsa_att(page_table, qblk, cache_k, cache_v, layer, new_page(k), new_page(v), bias)
    o = o.reshape(n_seq, KV_HEADS, GROUPS, t_new, KV_HEADS, ATT_HEAD_DIM)
    o = jnp.stack([o[:, g, :, :, g, :] for g in range(KV_HEADS)], axis=1)
    o = o.transpose(0, 3, 1, 2, 4).reshape(n_seq * t_new, Q_W)
    return _linear_res(grp, x, o, gate, w_o)


TOKEN_TILE = 256
FFN_TILE = 1024


def _trunk(x, mods, pos, wkv0, shift0, attn_fn, P):
    n_seq, seq_len, d = x.shape
    depth = mods.shape[0]
    grp = _Group(n_seq, seq_len, TOKEN_TILE)
    grp_ffn = _Group(n_seq, seq_len, FFN_TILE)
    x = x.reshape(n_seq * seq_len, d)
    v_first = None
    ks, vs, kis, wkvs, shifts = [], [], [], [], []
    for i in range(depth):
        j = i // 2
        sh_a, sc_a, g_a, sh_f, sc_f, g_f = jnp.split(mods[i], 6, axis=-1)
        g_att = P['norm_g'][i, 0].reshape(1, d)
        g_ffn = P['norm_g'][i, 1].reshape(1, d)
        if i % 2 == 0:
            t_minor = grp.per_seq and seq_len % WKV_CHUNK == 0 and LANES % (n_seq * (d // RW_HEAD)) == 0
            outs = _rwkv_proj(grp, x, g_att, sc_a, sh_a, shift0[j], P, j, v_first, t_minor)
            v, gate, bonus, h = outs[-4:]
            if j == 0:
                v_first = v
            if t_minor:
                y, state = _wkv_t(*outs[:6], wkv0[j])
            else:
                r, w, k, a, b = outs[:5]
                y, state = _wkv(grp, r, w, k, v, a, b, wkv0[j])
            x = _rwkv_post(grp, x, y, bonus, gate, g_a, P, j, t_minor)
            wkvs.append(state)
            shifts.append(h.reshape(n_seq, seq_len, d)[:, -1])
        else:
            q, k, v, qi, kiwi, *feat = _dsa_proj(grp, x, g_att, sc_a, sh_a, pos, P, j)
            x = attn_fn(j, grp, x, g_a, q, k, v, qi, kiwi)
            if feat:
                kt, vt, kit = feat
                ks.append(kt.reshape(n_seq, KV_HEADS, ATT_HEAD_DIM, seq_len).transpose(0, 3, 1, 2))
                vs.append(vt.reshape(n_seq, KV_HEADS, ATT_HEAD_DIM, seq_len).transpose(0, 3, 1, 2))
                kis.append(kit.transpose(0, 2, 1))
            else:
                ks.append(k.reshape(n_seq, seq_len, KV_HEADS, ATT_HEAD_DIM))
                vs.append(v.reshape(n_seq, seq_len, KV_HEADS, ATT_HEAD_DIM))
                kis.append(kiwi[:, :IDX_DIM].reshape(n_seq, seq_len, IDX_DIM))
        x = _ffn(grp_ffn, x, g_ffn, sc_f, sh_f, g_f, P['w_up'][i], P['w_down'][i])
    return (x.reshape(n_seq, seq_len, d), jnp.stack(ks), jnp.stack(vs), jnp.stack(kis),
            jnp.stack(wkvs), jnp.stack(shifts))


def kernel(x_prompt, x_sample, cache_k, cache_v, cache_idx_k, state_wkv, state_shift, page_table,
           c_prompt, c_sample, norm_g, w_ada, b_ada, w_up, w_down, rw_mix, rw_w_rkv, rw_w_o,
           rw_w0, rw_w1, rw_w2, rw_a0, rw_a1, rw_a2, rw_v0, rw_v1, rw_v2, rw_g1, rw_g2,
           rw_k_k, rw_k_a, rw_r_k, rw_lnx_w, rw_lnx_b, att_w_in, att_w_o, att_q_norm,
           att_k_norm, idx_k_norm):
    P = dict(norm_g=norm_g, w_up=w_up.astype(BF16), w_down=w_down.astype(BF16),
             rw_mix=rw_mix, rw_w_rkv=rw_w_rkv, rw_w_o=rw_w_o, rw_w0=rw_w0, rw_w1=rw_w1,
             rw_w2=rw_w2, rw_a0=rw_a0, rw_a1=rw_a1, rw_a2=rw_a2, rw_v0=rw_v0, rw_v1=rw_v1,
             rw_v2=rw_v2, rw_g1=rw_g1, rw_g2=rw_g2, rw_k_k=rw_k_k, rw_k_a=rw_k_a, rw_r_k=rw_r_k,
             rw_lnx_w=rw_lnx_w, rw_lnx_b=rw_lnx_b, att_w_in=att_w_in, att_w_o=att_w_o,
             att_q_norm=att_q_norm, att_k_norm=att_k_norm, idx_k_norm=idx_k_norm)
    n_p, seq, d = x_prompt.shape
    n_s, dec_seq, _ = x_sample.shape
    n_rwkv = state_wkv.shape[0]
    heads = d // RW_HEAD
    past_len = page_table.shape[1] * PAGE

    mods = _ada(jnp.concatenate([c_prompt, c_sample], axis=0), w_ada, b_ada)
    n_dsa, pool = cache_k.shape[:2]
    pages_k = cache_k.transpose(0, 1, 3, 4, 2).reshape(n_dsa, pool, KV_W, PAGE)
    pages_v = cache_v.transpose(0, 1, 3, 4, 2).reshape(n_dsa, pool, KV_W, PAGE)
    pages_ki = cache_idx_k.transpose(0, 1, 3, 2)

    def prompt_attn(j, grp, x, gate, q, k, v, qi, kiwi):
        return _dsa_prompt(grp.n_seq, grp.seq_len, x, gate, q, k, v, qi, kiwi, att_w_o[j])

    def sample_attn(j, grp, x, gate, q, k, v, qi, kiwi):
        return _dsa_sample(grp, x, gate, q, k, v, qi, kiwi, pages_k, pages_v, pages_ki, j,
                           page_table, att_w_o[j])

    wkv0 = jnp.zeros((n_rwkv, n_p, heads, RW_HEAD, RW_HEAD), F32)
    shift0 = jnp.zeros((n_rwkv, n_p, d), F32)
    y_p, k_p, v_p, ki_p, wkv_p, shift_p = _trunk(
        x_prompt, mods[:, :n_p], jnp.arange(seq), wkv0, shift0, prompt_attn, P)
    y_s, k_s, v_s, ki_s, wkv_s, shift_s = _trunk(
        x_sample, mods[:, n_p:], past_len + jnp.arange(dec_seq), state_wkv, state_shift, sample_attn, P)
    return (y_p, y_s, k_p, v_p, ki_p, wkv_p, shift_p, k_s, v_s, ki_s, wkv_s, shift_s)
```
